```python
import jax
import jax.numpy as jnp
from jax import lax
import numpy as np

D_MODEL = 2048
BATCH = 2
SEQ = 4096
DEPTH = 1

GRID_W = 64
CTX_LEN = 256
N_MOD = 6
NORM_EPS = 1e-6

RW_WIDTH = D_MODEL // 2
RW_HEAD_DIM = 64
RW_HEADS = RW_WIDTH // RW_HEAD_DIM
RW_DECAY_RANK = 64
RW_ICL_RANK = 64
RW_GATE_RANK = 160
RW_GN_EPS = 64e-5
RW_SIZES = (RW_WIDTH, RW_WIDTH, RW_WIDTH, RW_DECAY_RANK, RW_DECAY_RANK, RW_ICL_RANK, RW_ICL_RANK, RW_GATE_RANK)
RW_COLS = 3 * RW_WIDTH + 2 * RW_DECAY_RANK + 2 * RW_ICL_RANK + RW_GATE_RANK

HG_WIDTH = D_MODEL // 2
HG_KEY_DIM = 128
HG_HEADS = HG_WIDTH // HG_KEY_DIM
HG_VAL_DIM = HG_WIDTH // HG_HEADS
HG_CHUNK = 64
HG_SIZES = (HG_WIDTH, HG_WIDTH, HG_WIDTH, HG_WIDTH, HG_WIDTH)
HG_COLS = 5 * HG_WIDTH

IN_COLS = RW_COLS + HG_COLS + 2 * D_MODEL

PEER_HEADS = 8
PEER_NKEYS = 128
PEER_EXPERTS = PEER_NKEYS * PEER_NKEYS
PEER_KEY_DIM = 256
PEER_TOPK = 16
PEER_BLOCK = 128

kernel_name = "hybrid_rwkv7_hgrn2_peer_dit_block"


def rms_norm(x, gain):
    xf = x.astype(jnp.float32)
    y = xf * lax.rsqrt(jnp.mean(xf * xf, axis=-1, keepdims=True) + NORM_EPS)
    return (y * gain.astype(jnp.float32)).astype(x.dtype)


def modulate(h, shift, scale):
    return h * (1 + scale) + shift


def split_cols(z, sizes):
    return jnp.split(z, [int(s) for s in np.cumsum(sizes)[:-1]], axis=-1)


def to_heads(t, d):
    return t.reshape(t.shape[:-1] + (t.shape[-1] // d, d))


def shift_conv_grid(z, w, rows):
    b, l, ch = z.shape
    zg = z.reshape(b, rows, GRID_W, ch)
    y = lax.conv_general_dilated(zg, w[:, :, None, :].astype(z.dtype), window_strides=(1, 1), padding="SAME",
                                 dimension_numbers=("NHWC", "HWIO", "NHWC"), feature_group_count=ch)
    return y.reshape(b, l, ch)


def shift_conv_seq(z, w):
    zp = jnp.pad(z, ((0, 0), (1, 1), (0, 0)))
    wr = w[1].astype(z.dtype)
    return zp[:, :-2] * wr[0] + zp[:, 1:-1] * wr[1] + zp[:, 2:] * wr[2]


def rwkv_streams(z, w0, w_lora_b, a0, a_lora_b, g_lora_b, k_k, k_a):
    f32 = jnp.float32
    r, k, v, wa_f, wa_b, aa_f, aa_b, ga = split_cols(z, RW_SIZES)
    wa = jnp.stack([wa_f, wa_b])
    aa = jnp.stack([aa_f, aa_b])
    w_pre = (w0[:, None, None, :] + jnp.einsum("dblr,drc->dblc", jnp.tanh(wa), w_lora_b)).astype(f32)
    decay = jnp.exp(-jnp.exp(-jax.nn.softplus(-w_pre) - 0.5))
    a = jax.nn.sigmoid((a0[:, None, None, :] + jnp.einsum("dblr,drc->dblc", aa, a_lora_b)).astype(f32))
    g = jnp.einsum("blr,rc->blc", jax.nn.sigmoid(ga), g_lora_b)
    kk = to_heads((k * k_k).astype(f32), RW_HEAD_DIM)
    kk = kk / jnp.maximum(jnp.sqrt(jnp.sum(kk * kk, axis=-1, keepdims=True)), 1e-12)
    k_dir = k.astype(f32)[None] * (1 + (a - 1) * k_a.astype(f32))
    return {"r": to_heads(r.astype(f32), RW_HEAD_DIM), "v": to_heads(v.astype(f32), RW_HEAD_DIM),
            "k": to_heads(k_dir, RW_HEAD_DIM), "decay": to_heads(decay, RW_HEAD_DIM), "kk": kk,
            "kka": kk[None] * to_heads(a, RW_HEAD_DIM), "g": g}


def rwkv_scan(state0, r, decay, k, v, kk, kka):
    def step(state, inp):
        r_t, w_t, k_t, v_t, kk_t, kka_t = inp
        removal = jnp.einsum("bhvk,bhk->bhv", state, kk_t)
        state = (state * w_t[:, :, None, :] - removal[..., None] * kka_t[:, :, None, :]
                 + v_t[..., None] * k_t[:, :, None, :])
        return state, jnp.einsum("bhvk,bhk->bhv", state, r_t)
    xs = tuple(jnp.swapaxes(t, 0, 1) for t in (r, decay, k, v, kk, kka))
    state, out = lax.scan(step, state0, xs)
    return jnp.swapaxes(out, 0, 1), state


def rwkv_bidir(s, states0):
    flip = lambda t: jnp.flip(t, axis=1)
    o_f, s_f = rwkv_scan(states0[0], s["r"], s["decay"][0], s["k"][0], s["v"], s["kk"], s["kka"][0])
    o_b, s_b = rwkv_scan(states0[1], flip(s["r"]), flip(s["decay"][1]), flip(s["k"][1]), flip(s["v"]),
                         flip(s["kk"]), flip(s["kka"][1]))
    return o_f + flip(o_b), (s_f, s_b)


def hgrn_streams(z, lb):
    f32 = jnp.float32
    q, f_f, f_b, i, og = split_cols(z, HG_SIZES)
    lbb = lb[:, None, None, :]
    forget = lbb + (1 - lbb) * jax.nn.sigmoid(jnp.stack([f_f, f_b]).astype(f32))
    return {"q": to_heads(jax.nn.silu(q.astype(f32)), HG_KEY_DIM), "logf": to_heads(jnp.log(forget), HG_KEY_DIM),
            "k": to_heads(1 - forget, HG_KEY_DIM), "v": to_heads(i.astype(f32), HG_VAL_DIM), "og": og}


def hgrn_chunk_scan(state0, q, logf, k, v):
    b, l, h, _ = q.shape
    dv = v.shape[-1]
    n = l // HG_CHUNK
    blk = lambda t: t.reshape(b, n, HG_CHUNK, h, t.shape[-1])
    q, logf, k, v = blk(q), blk(logf), blk(k), blk(v)
    cum = jnp.cumsum(logf, axis=2)
    ref = cum[:, :, HG_CHUNK // 2][:, :, None]
    last = cum[:, :, -1]
    tri = jnp.tril(jnp.ones((HG_CHUNK, HG_CHUNK), dtype=bool))
    scores = jnp.einsum("bnthk,bnshk->bnhts", q * jnp.exp(cum - ref), k * jnp.exp(ref - cum))
    scores = jnp.where(tri, scores, 0.0)
    o_intra = jnp.einsum("bnhts,bnshv->bnthv", scores, v)
    d_state = jnp.einsum("bnshk,bnshv->bnhkv", k * jnp.exp(last[:, :, None] - cum), v)

    def step(state, inp):
        dec, ds = inp
        return dec[..., None] * state + ds, state

    state, starts = lax.scan(step, state0, (jnp.swapaxes(jnp.exp(last), 0, 1), jnp.swapaxes(d_state, 0, 1)))
    starts = jnp.swapaxes(starts, 0, 1)
    o_inter = jnp.einsum("bnthk,bnhkv->bnthv", q * jnp.exp(cum), starts)
    return (o_intra + o_inter).reshape(b, l, h, dv), state


def hgrn_bidir(s, states0):
    flip = lambda t: jnp.flip(t, axis=1)
    o_f, s_f = hgrn_chunk_scan(states0[0], s["q"], s["logf"][0], s["k"][0], s["v"])
    o_b, s_b = hgrn_chunk_scan(states0[1], flip(s["q"]), flip(s["logf"][1]), flip(s["k"][1]), flip(s["v"]))
    return o_f + flip(o_b), (s_f, s_b)


def token_mixer_core(h, rows, states0, lp):
    z = jnp.einsum("bld,dc->blc", h, lp["w_in"])
    z_rw, z_hg, gate_rw, gate_hg = split_cols(z, (RW_COLS, HG_COLS, D_MODEL, D_MODEL))
    if rows is None:
        z_rw = shift_conv_seq(z_rw, lp["rw_conv"])
    else:
        z_rw = shift_conv_grid(z_rw, lp["rw_conv"], rows)
    s_rw = rwkv_streams(z_rw, lp["rw_w0"], lp["rw_w_lora_b"], lp["rw_a0"], lp["rw_a_lora_b"],
                        lp["rw_g_lora_b"], lp["rw_k_k"], lp["rw_k_a"])
    s_hg = hgrn_streams(z_hg, lp["hg_lb"])
    o_rw, st_rw = rwkv_bidir(s_rw, states0[0])
    o_hg, st_hg = hgrn_bidir(s_hg, states0[1])
    return (o_rw, s_rw, o_hg, s_hg, gate_rw, gate_hg), (st_rw, st_hg)


def token_mixer_readout(feats, lp, dt):
    f32 = jnp.float32
    o_rw, s_rw, o_hg, s_hg, gate_rw, gate_hg = feats
    bl = o_rw.shape[:2]
    mu = jnp.mean(o_rw, axis=-1, keepdims=True)
    var = jnp.mean(jnp.square(o_rw - mu), axis=-1, keepdims=True)
    o_rw_n = ((o_rw - mu) * lax.rsqrt(var + RW_GN_EPS)).reshape(bl + (RW_WIDTH,))
    bonus = jnp.sum(s_rw["r"][None] * s_rw["k"] * lp["rw_r_k"].astype(f32), axis=(0, -1))[..., None] * s_rw["v"]
    y_rw = (o_rw_n * lp["rw_ln_w"].astype(f32) + lp["rw_ln_b"].astype(f32)
            + bonus.reshape(bl + (RW_WIDTH,))) * s_rw["g"].astype(f32)
    y_hg = o_hg * lax.rsqrt(jnp.mean(o_hg * o_hg, axis=-1, keepdims=True) + NORM_EPS) * lp["hg_norm"].astype(f32)
    y_hg = y_hg.reshape(bl + (HG_WIDTH,)) * jax.nn.silu(s_hg["og"].astype(f32))
    branch_rw = jnp.einsum("blc,cd->bld", y_rw.astype(dt), lp["w_up_rw"])
    branch_hg = jnp.einsum("blc,cd->bld", y_hg.astype(dt), lp["w_up_hg"])
    merged = jax.nn.sigmoid(gate_rw) * branch_rw + jax.nn.sigmoid(gate_hg) * branch_hg
    return jnp.einsum("bld,de->ble", merged, lp["w_out"])


def peer_ffn(h, wq, k1, k2, u_tab, v_tab):
    b, l, d = h.shape
    q = jnp.einsum("bld,dq->blq", h, wq).reshape(b, l, PEER_HEADS, 2, PEER_KEY_DIM // 2)
    s1 = jnp.einsum("blhd,nd->blhn", q[..., 0, :], k1).astype(jnp.float32)
    s2 = jnp.einsum("blhd,nd->blhn", q[..., 1, :], k2).astype(jnp.float32)
    v1, i1 = lax.top_k(s1, PEER_TOPK)
    v2, i2 = lax.top_k(s2, PEER_TOPK)
    cand = (v1[..., :, None] + v2[..., None, :]).reshape(b, l, PEER_HEADS, PEER_TOPK * PEER_TOPK)
    best, pos = lax.top_k(cand, PEER_TOPK)
    e1 = jnp.take_along_axis(i1, pos // PEER_TOPK, axis=-1)
    e2 = jnp.take_along_axis(i2, pos % PEER_TOPK, axis=-1)
    experts = e1 * PEER_NKEYS + e2
    gates = jax.nn.softmax(best, axis=-1).astype(h.dtype)
    n_sel = PEER_HEADS * PEER_TOPK
    nb = (b * l) // PEER_BLOCK
    hb = h.reshape(nb, PEER_BLOCK, d)
    eb = experts.reshape(nb, PEER_BLOCK, n_sel)
    gb = gates.reshape(nb, PEER_BLOCK, n_sel)

    def block(args):
        hx, ex, gx = args
        act = jax.nn.gelu(jnp.einsum("td,tkd->tk", hx, jnp.take(u_tab, ex, axis=0)), approximate=False)
        return jnp.einsum("tk,tkd->td", gx * act, jnp.take(v_tab, ex, axis=0))

    return lax.map(block, (hb, eb, gb)).reshape(b, l, d)


def setup_inputs(seed: int = 0) -> dict:
    key = jax.random.key(seed)
    ks = jax.random.split(key, 31)
    f32 = jnp.float32

    def nrm(k, shape, scale):
        return jax.random.normal(k, shape, f32) * scale

    d = D_MODEL
    rw_conv = nrm(ks[10], (DEPTH, 3, 3, RW_COLS), 0.1).at[:, 1, 1, :].add(0.5)
    return {
        "x": nrm(ks[0], (BATCH, SEQ, d), 1.0),
        "c": nrm(ks[1], (BATCH, d), 1.0),
        "ctx": nrm(ks[2], (BATCH, CTX_LEN, d), 1.0),
        "c_ctx": nrm(ks[3], (d,), 1.0),
        "w_ada": nrm(ks[4], (DEPTH, d, N_MOD * d), 0.5 * d ** -0.5),
        "b_ada": nrm(ks[5], (DEPTH, N_MOD * d), 0.02),
        "norm_mix": 1.0 + nrm(ks[6], (DEPTH, d), 0.05),
        "norm_ffn": 1.0 + nrm(ks[7], (DEPTH, d), 0.05),
        "norm_final": 1.0 + nrm(ks[8], (d,), 0.05),
        "w_in": nrm(ks[9], (DEPTH, d, IN_COLS), d ** -0.5),
        "rw_conv": rw_conv,
        "rw_w0": jax.random.uniform(ks[11], (DEPTH, 2, RW_WIDTH), f32, -5.0, 1.0),
        "rw_w_lora_b": nrm(ks[12], (DEPTH, 2, RW_DECAY_RANK, RW_WIDTH), 0.1 * RW_DECAY_RANK ** -0.5),
        "rw_a0": nrm(ks[13], (DEPTH, 2, RW_WIDTH), 0.5),
        "rw_a_lora_b": nrm(ks[14], (DEPTH, 2, RW_ICL_RANK, RW_WIDTH), 0.5 * RW_ICL_RANK ** -0.5),
        "rw_g_lora_b": nrm(ks[15], (DEPTH, RW_GATE_RANK, RW_WIDTH), RW_GATE_RANK ** -0.5),
        "rw_k_k": 0.85 + nrm(ks[16], (DEPTH, RW_WIDTH), 0.1),
        "rw_k_a": 1.0 + nrm(ks[17], (DEPTH, RW_WIDTH), 0.1),
        "rw_r_k": nrm(ks[18], (DEPTH, RW_HEADS, RW_HEAD_DIM), 0.5),
        "rw_ln_w": 1.0 + nrm(ks[19], (DEPTH, RW_WIDTH), 0.05),
        "rw_ln_b": nrm(ks[20], (DEPTH, RW_WIDTH), 0.02),
        "hg_lb": nrm(ks[21], (DEPTH + 1, 2, HG_WIDTH), 0.5),
        "hg_norm": 1.0 + nrm(ks[22], (DEPTH, HG_VAL_DIM), 0.05),
        "w_up_rw": nrm(ks[23], (DEPTH, RW_WIDTH, d), RW_WIDTH ** -0.5),
        "w_up_hg": nrm(ks[24], (DEPTH, HG_WIDTH, d), HG_WIDTH ** -0.5),
        "w_out": nrm(ks[25], (DEPTH, d, d), d ** -0.5),
        "peer_wq": nrm(ks[26], (DEPTH, d, PEER_HEADS * PEER_KEY_DIM), d ** -0.5),
        "peer_k1": nrm(ks[27], (DEPTH, PEER_NKEYS, PEER_KEY_DIM // 2), (PEER_KEY_DIM // 2) ** -0.5),
        "peer_k2": nrm(ks[28], (DEPTH, PEER_NKEYS, PEER_KEY_DIM // 2), (PEER_KEY_DIM // 2) ** -0.5),
        "peer_u": nrm(ks[29], (DEPTH, PEER_EXPERTS, d), d ** -0.5),
        "peer_v": nrm(ks[30], (DEPTH, PEER_EXPERTS, d), 0.5),
    }


def reference(x, c, ctx, c_ctx, w_ada, b_ada, norm_mix, norm_ffn, norm_final, w_in, rw_conv, rw_w0,
              rw_w_lora_b, rw_a0, rw_a_lora_b, rw_g_lora_b, rw_k_k, rw_k_a, rw_r_k, rw_ln_w, rw_ln_b,
              hg_lb, hg_norm, w_up_rw, w_up_hg, w_out, peer_wq, peer_k1, peer_k2, peer_u, peer_v):
    f32 = jnp.float32
    dt = x.dtype
    b = x.shape[0]
    rows = x.shape[1] // GRID_W
    lower_bounds = jnp.cumsum(jax.nn.softmax(hg_lb.astype(f32), axis=0), axis=0)
    zero_rw = jnp.zeros((b, RW_HEADS, RW_HEAD_DIM, RW_HEAD_DIM), f32)
    zero_hg = jnp.zeros((b, HG_HEADS, HG_KEY_DIM, HG_VAL_DIM), f32)
    for l in range(DEPTH):
        lp = {"w_in": w_in[l], "rw_conv": rw_conv[l], "rw_w0": rw_w0[l], "rw_w_lora_b": rw_w_lora_b[l],
              "rw_a0": rw_a0[l], "rw_a_lora_b": rw_a_lora_b[l], "rw_g_lora_b": rw_g_lora_b[l],
              "rw_k_k": rw_k_k[l], "rw_k_a": rw_k_a[l], "rw_r_k": rw_r_k[l], "rw_ln_w": rw_ln_w[l],
              "rw_ln_b": rw_ln_b[l], "hg_lb": lower_bounds[l], "hg_norm": hg_norm[l],
              "w_up_rw": w_up_rw[l], "w_up_hg": w_up_hg[l], "w_out": w_out[l]}
        mod_x = jnp.einsum("bd,de->be", jax.nn.silu(c), w_ada[l]) + b_ada[l]
        mod_c = jnp.einsum("d,de->e", jax.nn.silu(c_ctx), w_ada[l]) + b_ada[l]
        sh_mx, sc_mx, g_mx, sh_fx, sc_fx, g_fx = jnp.split(mod_x[:, None, :], N_MOD, axis=-1)
        sh_mc, sc_mc, g_mc, sh_fc, sc_fc, g_fc = jnp.split(mod_c[None, None, :], N_MOD, axis=-1)

        hc = modulate(rms_norm(ctx, norm_mix[l]), sh_mc, sc_mc)
        feats_c, ctx_states = token_mixer_core(hc, None, ((zero_rw, zero_rw), (zero_hg, zero_hg)), lp)

        hx = modulate(rms_norm(x, norm_mix[l]), sh_mx, sc_mx)
        feats_x, _ = token_mixer_core(hx, rows, ctx_states, lp)
        x = x + g_mx * token_mixer_readout(feats_x, lp, dt)
        hx = modulate(rms_norm(x, norm_ffn[l]), sh_fx, sc_fx)
        x = x + g_fx * peer_ffn(hx, peer_wq[l], peer_k1[l], peer_k2[l], peer_u[l], peer_v[l])

        if l < DEPTH - 1:
            ctx = ctx + g_mc * token_mixer_readout(feats_c, lp, dt)
            hc = modulate(rms_norm(ctx, norm_ffn[l]), sh_fc, sc_fc)
            ctx = ctx + g_fc * peer_ffn(hc, peer_wq[l], peer_k1[l], peer_k2[l], peer_u[l], peer_v[l])
    return rms_norm(x, norm_final)
```

```python
import functools

import jax
import jax.numpy as jnp
import numpy as np
from jax import lax
from jax.experimental import pallas as pl
from jax.experimental.pallas import tpu as pltpu

F32 = jnp.float32
BF16 = jnp.bfloat16
HIGHEST = lax.Precision.HIGHEST

D_MODEL = 2048
N_MOD = 6
NORM_EPS = 1e-6
CHUNK = 64
SUB = 16
RW_WIDTH = 1024
RW_HEAD = 64
RW_HEADS = 16
RW_LORA = 64
RW_GATE_RANK = 160
RW_COLS = 3488
RW_PAD = 3584
RW_GN_EPS = 64e-5
HG_WIDTH = 1024
HG_HEAD = 128
HG_HEADS = 8
PEER_HEADS = 8
PEER_NKEYS = 128
PEER_TOPK = 16
EXP_M05 = float(np.exp(-0.5))
VMEM_LIMIT = 56 * 1024 * 1024


def _cp(sem):
    return pltpu.CompilerParams(dimension_semantics=sem, vmem_limit_bytes=VMEM_LIMIT)


def _mm(a, b, exact=False):
    if exact:
        return jnp.dot(a, b, preferred_element_type=F32, precision=HIGHEST)
    return jnp.dot(a.astype(BF16), b.astype(BF16), preferred_element_type=F32)


def _mm_nt(a, b, exact=False):
    dims = (((1,), (1,)), ((), ()))
    if exact:
        return lax.dot_general(a, b, dims, preferred_element_type=F32, precision=HIGHEST)
    return lax.dot_general(a.astype(BF16), b.astype(BF16), dims, preferred_element_type=F32)


def _mm_tn(a, b, exact=False):
    dims = (((0,), (0,)), ((), ()))
    if exact:
        return lax.dot_general(a, b, dims, preferred_element_type=F32, precision=HIGHEST)
    return lax.dot_general(a.astype(BF16), b.astype(BF16), dims, preferred_element_type=F32)


def _sigmoid(x):
    return 1.0 / (1.0 + jnp.exp(-x))


def _norm_mod(x, gain, shift, scale):
    y = x * lax.rsqrt(jnp.mean(x * x, axis=-1, keepdims=True) + NORM_EPS) * gain
    return y * (1.0 + scale) + shift


def _ada_kernel(c_ref, w_ref, b_ref, o_ref):
    cv = c_ref[...]
    o_ref[...] = _mm(cv * _sigmoid(cv), w_ref[...], exact=True) + b_ref[...]


def _ada(cvec, w, b, tn=512):
    k, n = w.shape
    return pl.pallas_call(
        _ada_kernel,
        out_shape=jax.ShapeDtypeStruct((cvec.shape[0], n), F32),
        grid=(n // tn,),
        in_specs=[pl.BlockSpec((cvec.shape[0], k), lambda j: (0, 0)),
                  pl.BlockSpec((k, tn), lambda j: (0, j)),
                  pl.BlockSpec((1, tn), lambda j: (0, j))],
        out_specs=pl.BlockSpec((cvec.shape[0], tn), lambda j: (0, j)),
        compiler_params=_cp(("parallel",)),
        name="ada",
    )(cvec, w, b)


def _inproj_kernel(x_ref, g_ref, sh_ref, sc_ref, w_ref, o_ref, h_scr):
    @pl.when(pl.program_id(1) == 0)
    def _():
        h_scr[...] = _norm_mod(x_ref[...], g_ref[...], sh_ref[...], sc_ref[...]).astype(BF16)

    o_ref[...] = jnp.dot(h_scr[...], w_ref[...], preferred_element_type=F32)


def _inproj(x2d, gain, shift, scale, w, rows_per_mod, tm=512, tn=512):
    m, k = x2d.shape
    n = w.shape[1]
    bpm = rows_per_mod // tm
    mod_map = lambda i, j: (i // bpm, 0, 0)
    return pl.pallas_call(
        _inproj_kernel,
        out_shape=jax.ShapeDtypeStruct((m, n), F32),
        grid=(m // tm, n // tn),
        in_specs=[pl.BlockSpec((tm, k), lambda i, j: (i, 0)),
                  pl.BlockSpec((1, k), lambda i, j: (0, 0)),
                  pl.BlockSpec((None, 1, k), mod_map),
                  pl.BlockSpec((None, 1, k), mod_map),
                  pl.BlockSpec((k, tn), lambda i, j: (0, j))],
        out_specs=pl.BlockSpec((tm, tn), lambda i, j: (i, j)),
        scratch_shapes=[pltpu.VMEM((tm, k), BF16)],
        compiler_params=_cp(("parallel", "arbitrary")),
        name="inproj",
    )(x2d, gain, shift, scale, w)


def _tri_masks(reverse):
    row = lax.broadcasted_iota(jnp.int32, (CHUNK, CHUNK), 0)
    col = lax.broadcasted_iota(jnp.int32, (CHUNK, CHUNK), 1)
    if reverse:
        strict, incl = col > row, col >= row
    else:
        strict, incl = col < row, col <= row
    same = (row // SUB) == (col // SUB)
    return strict, incl, same, row == col


def _rw_chunk(r, k, v, a, b, cum, cum_prev, reverse):
    strict, incl, same, eye = _tri_masks(reverse)
    mid = CHUNK // 2 - 1 if reverse else CHUNK // 2
    end = 0 if reverse else CHUNK - 1
    rho = cum[mid:mid + 1]
    tot = cum[end:end + 1]
    e_out = jnp.exp(rho - cum)
    a_t = a * jnp.exp(cum_prev - rho)
    r_t = r * jnp.exp(cum - rho)
    b_t = b * e_out
    k_t = k * e_out
    a_0 = a * jnp.exp(cum_prev)
    r_0 = r * jnp.exp(cum)
    e_end = jnp.exp(tot - cum)
    b_end = b * e_end
    k_end = k * e_end

    s4 = _mm_nt(jnp.concatenate([a_t, r_t], axis=0), jnp.concatenate([b_t, k_t], axis=0))
    m_ab = jnp.where(strict, s4[:CHUNK, :CHUNK], 0.0)
    m_ak = jnp.where(strict, s4[:CHUNK, CHUNK:], 0.0)
    m_rb = jnp.where(incl, s4[CHUNK:, :CHUNK], 0.0)
    m_rk = jnp.where(incl, s4[CHUNK:, CHUNK:], 0.0)

    ident = jnp.where(eye, 1.0, 0.0)
    m_bd = jnp.where(same, m_ab, 0.0)
    m_off = jnp.where(same, 0.0, m_ab)
    m2 = _mm(m_bd, m_bd)
    m4 = _mm(m2, m2)
    m8 = _mm(m4, m4)
    p = ident + m_bd
    p = p + _mm(p, m2)
    p = p + _mm(p, m4)
    p = p + _mm(p, m8)
    n1 = _mm(p, m_off)
    n2 = _mm(n1, n1)
    t_inv = _mm(ident + n1 + n2 + _mm(n1, n2), p)

    x = _mm(m_ak, v)
    wu = _mm(t_inv, jnp.concatenate([a_0, x], axis=1))
    rhs = jnp.concatenate([wu, jnp.concatenate([jnp.zeros_like(v), v], axis=1)], axis=0)
    qo = _mm(jnp.concatenate([m_rb, m_rk], axis=1), rhs)
    gd = _mm_tn(jnp.concatenate([b_end, k_end], axis=0), rhs)
    q_hat = r_0 + qo[:, :RW_HEAD]
    o_hat = qo[:, RW_HEAD:]
    g_t = gd[:, :RW_HEAD] + jnp.where(eye, jnp.exp(tot), 0.0)
    d_t = gd[:, RW_HEAD:]
    return q_hat, o_hat, g_t, d_t


def _rwkv_pre_kernel(zp_ref, zc_ref, zn_ref, conv_ref, w0_ref, wl_ref, a0_ref, al_ref, gl_ref, kk_ref, ka_ref,
                     rk_ref, qf_ref, of_ref, gf_ref, df_ref, qb_ref, ob_ref, gb_ref, db_ref, g_ref, bonus_ref,
                     r_s, k_s, v_s, kkr_s, a_s, lw_s, cum_s, *, is_ctx, nchunks):
    c = pl.program_id(1)
    zc = zc_ref[...]
    row = lax.broadcasted_iota(jnp.int32, (CHUNK, 1), 0)
    has_p = (c > 0).astype(F32)
    has_n = (c < nchunks - 1).astype(F32)

    def left(z, fill):
        return jnp.where(row == 0, fill, pltpu.roll(z, 1, 0))

    def right(z, fill):
        return jnp.where(row == CHUNK - 1, fill, pltpu.roll(z, CHUNK - 1, 0))

    w = conv_ref[...]
    if is_ctx:
        conv = (left(zc, zp_ref[CHUNK - 1:CHUNK, :] * has_p) * w[3:4] + zc * w[4:5]
                + right(zc, zn_ref[0:1, :] * has_n) * w[5:6])
    else:
        conv = None
        for dy, z in enumerate((zp_ref[...] * has_p, zc, zn_ref[...] * has_n)):
            t = left(z, 0.0) * w[3 * dy:3 * dy + 1] + z * w[3 * dy + 1:3 * dy + 2] + right(z, 0.0) * w[3 * dy + 2:3 * dy + 3]
            conv = t if conv is None else conv + t

    r_s[...] = conv[:, 0:RW_WIDTH]
    kc = conv[:, RW_WIDTH:2 * RW_WIDTH]
    k_s[...] = kc
    kkr_s[...] = kc * kk_ref[...]
    v_s[...] = conv[:, 2 * RW_WIDTH:3 * RW_WIDTH]
    lo = conv[:, 3 * RW_WIDTH:RW_PAD]
    g_ref[...] = _mm(_sigmoid(lo[:, 4 * RW_LORA:]), gl_ref[...], exact=True)

    trow = lax.broadcasted_iota(jnp.int32, (CHUNK, CHUNK), 0)
    tcol = lax.broadcasted_iota(jnp.int32, (CHUNK, CHUNK), 1)
    for d in range(2):
        wa = lo[:, d * RW_LORA:(d + 1) * RW_LORA]
        aa = lo[:, (2 + d) * RW_LORA:(3 + d) * RW_LORA]
        w_pre = w0_ref[d:d + 1, :] + _mm(jnp.tanh(wa), wl_ref[d], exact=True)
        logw = -EXP_M05 * _sigmoid(w_pre)
        lw_s[d] = logw
        a_s[d] = _sigmoid(a0_ref[d:d + 1, :] + _mm(aa, al_ref[d], exact=True))
        tri = jnp.where((tcol >= trow) if d == 1 else (tcol <= trow), 1.0, 0.0)
        cum_s[d] = _mm(tri, logw, exact=True)

    lane = lax.broadcasted_iota(jnp.int32, (1, 2 * RW_HEAD), 1)
    first = lane < RW_HEAD
    outs = ((qf_ref, of_ref, gf_ref, df_ref), (qb_ref, ob_ref, gb_ref, db_ref))

    def pair_sum(t):
        s0 = jnp.sum(jnp.where(first, t, 0.0), axis=-1, keepdims=True)
        s1 = jnp.sum(jnp.where(first, 0.0, t), axis=-1, keepdims=True)
        return jnp.where(first, s0, s1)

    def body(p, carry):
        sl = pl.ds(pl.multiple_of(p * 2 * RW_HEAD, 2 * RW_HEAD), 2 * RW_HEAD)
        r2 = r_s[:, sl]
        k2 = k_s[:, sl]
        v2 = v_s[:, sl]
        kkr = kkr_s[:, sl]
        ka2 = ka_ref[:, sl]
        rk2 = rk_ref[:, sl]
        kk2 = kkr / jnp.maximum(jnp.sqrt(pair_sum(kkr * kkr)), 1e-12)
        bonus = jnp.zeros_like(r2)
        for d in range(2):
            a2 = a_s[d, :, sl]
            cum2 = cum_s[d, :, sl]
            prev2 = cum2 - lw_s[d, :, sl]
            kd2 = k2 * (1.0 + (a2 - 1.0) * ka2)
            kka2 = kk2 * a2
            bonus = bonus + pair_sum(r2 * kd2 * rk2)
            res = []
            for half in range(2):
                hs = slice(half * RW_HEAD, (half + 1) * RW_HEAD)
                res.append(_rw_chunk(r2[:, hs], kd2[:, hs], v2[:, hs], -kk2[:, hs], kka2[:, hs],
                                     cum2[:, hs], prev2[:, hs], reverse=(d == 1)))
            for ref, x0, x1 in zip(outs[d], res[0], res[1]):
                ref[:, sl] = jnp.concatenate([x0, x1], axis=1)
        bonus_ref[:, sl] = bonus * v2
        return carry

    lax.fori_loop(0, RW_HEADS // 2, body, 0)


def _rwkv_pre(z_rw, conv_w, w0, wl, a0, al, gl, k_k, k_a, r_k, batch, nchunks, is_ctx):
    rows = z_rw.shape[0]
    last = nchunks - 1
    zspec = lambda f: pl.BlockSpec((CHUNK, RW_PAD), lambda b, c: (b * nchunks + f(c), 0))
    full = lambda shape: pl.BlockSpec(shape, lambda b, c: (0,) * len(shape))
    ospec = pl.BlockSpec((CHUNK, RW_WIDTH), lambda b, c: (b * nchunks + c, 0))
    out = jax.ShapeDtypeStruct((rows, RW_WIDTH), F32)
    slab = pltpu.VMEM((CHUNK, RW_WIDTH), F32)
    slab2 = pltpu.VMEM((2, CHUNK, RW_WIDTH), F32)
    return pl.pallas_call(
        functools.partial(_rwkv_pre_kernel, is_ctx=is_ctx, nchunks=nchunks),
        out_shape=[out] * 10,
        grid=(batch, nchunks),
        in_specs=[zspec(lambda c: jnp.maximum(c - 1, 0)), zspec(lambda c: c), zspec(lambda c: jnp.minimum(c + 1, last)),
                  full((9, RW_PAD)), full((2, RW_WIDTH)), full((2, RW_LORA, RW_WIDTH)), full((2, RW_WIDTH)),
                  full((2, RW_LORA, RW_WIDTH)), full((RW_PAD - 3 * RW_WIDTH - 4 * RW_LORA, RW_WIDTH)), full((1, RW_WIDTH)),
                  full((1, RW_WIDTH)), full((1, RW_WIDTH))],
        out_specs=[ospec] * 10,
        scratch_shapes=[slab, slab, slab, slab, slab2, slab2, slab2],
        compiler_params=_cp(("parallel", "parallel")),
        name="rwkv_pre",
    )(z_rw, z_rw, z_rw, conv_w, w0, wl, a0, al, gl, k_k, k_a, r_k)


def _rwkv_seq_kernel(x0_ref, q_ref, o_ref, g_ref, d_ref, out_ref, xfin_ref, x_s, *, nchunks):
    c = pl.program_id(1)

    @pl.when(c == 0)
    def _():
        x_s[...] = x0_ref[...]

    for h in range(RW_HEADS):
        hs = slice(h * RW_HEAD, (h + 1) * RW_HEAD)
        res = _mm(jnp.concatenate([q_ref[:, hs], g_ref[:, hs]], axis=0), x_s[h], exact=True)
        out_ref[:, hs] = res[:CHUNK] + o_ref[:, hs]
        x_s[h] = res[CHUNK:] + d_ref[:, hs]

    @pl.when(c == nchunks - 1)
    def _():
        xfin_ref[...] = x_s[...]


def _rwkv_seq(x0, qh, oh, gt, dt, batch, nchunks, reverse):
    pos = (lambda c: nchunks - 1 - c) if reverse else (lambda c: c)
    spec = pl.BlockSpec((CHUNK, RW_WIDTH), lambda b, c: (b * nchunks + pos(c), 0))
    sspec = pl.BlockSpec((None, RW_HEADS, RW_HEAD, RW_HEAD), lambda b, c: (b, 0, 0, 0))
    return pl.pallas_call(
        functools.partial(_rwkv_seq_kernel, nchunks=nchunks),
        out_shape=[jax.ShapeDtypeStruct(qh.shape, F32), jax.ShapeDtypeStruct(x0.shape, F32)],
        grid=(batch, nchunks),
        in_specs=[sspec, spec, spec, spec, spec],
        out_specs=[spec, sspec],
        scratch_shapes=[pltpu.VMEM((RW_HEADS, RW_HEAD, RW_HEAD), F32)],
        compiler_params=_cp(("parallel", "arbitrary")),
        name="rwkv_seq",
    )(x0, qh, oh, gt, dt)


def _hgrn_kernel(s0_ref, q_ref, f_ref, i_ref, lb_ref, out_ref, sfin_ref, st_s, *, nchunks, reverse):
    c = pl.program_id(1)

    @pl.when(c == 0)
    def _():
        st_s[...] = s0_ref[...]

    row = lax.broadcasted_iota(jnp.int32, (CHUNK, CHUNK), 0)
    col = lax.broadcasted_iota(jnp.int32, (CHUNK, CHUNK), 1)
    incl = (col >= row) if reverse else (col <= row)
    mid = CHUNK // 2 - 1 if reverse else CHUNK // 2
    end = 0 if reverse else CHUNK - 1

    lb = lb_ref[...]
    forget = lb + (1.0 - lb) * _sigmoid(f_ref[...])
    kf = 1.0 - forget
    qraw = q_ref[...]
    q = qraw * _sigmoid(qraw)
    v = i_ref[...]
    cum = _mm(jnp.where(incl, 1.0, 0.0), jnp.log(forget), exact=True)
    ref = cum[mid:mid + 1]
    tot = cum[end:end + 1]
    qs = q * jnp.exp(cum - ref)
    ks = kf * jnp.exp(ref - cum)
    qe = q * jnp.exp(cum)
    ke = kf * jnp.exp(tot - cum)
    dec = jnp.exp(tot)
    for h in range(HG_HEADS):
        sl = slice(h * HG_HEAD, (h + 1) * HG_HEAD)
        st = st_s[h]
        scores = jnp.where(incl, _mm_nt(qs[:, sl], ks[:, sl]), 0.0)
        out_ref[:, sl] = _mm(scores, v[:, sl]) + _mm_nt(qe[:, sl], st)
        st_s[h] = st * dec[:, sl] + _mm_tn(v[:, sl], ke[:, sl])

    @pl.when(c == nchunks - 1)
    def _():
        sfin_ref[...] = st_s[...]


def _hgrn(s0, z_hg, lb, batch, nchunks, reverse):
    d = 1 if reverse else 0
    pos = (lambda c: nchunks - 1 - c) if reverse else (lambda c: c)
    zspec = lambda blk: pl.BlockSpec((CHUNK, HG_WIDTH), lambda b, c: (b * nchunks + pos(c), blk))
    sspec = pl.BlockSpec((None, HG_HEADS, HG_HEAD, HG_HEAD), lambda b, c: (b, 0, 0, 0))
    return pl.pallas_call(
        functools.partial(_hgrn_kernel, nchunks=nchunks, reverse=reverse),
        out_shape=[jax.ShapeDtypeStruct((z_hg.shape[0], HG_WIDTH), F32), jax.ShapeDtypeStruct(s0.shape, F32)],
        grid=(batch, nchunks),
        in_specs=[sspec, zspec(0), zspec(1 + d), zspec(3), pl.BlockSpec((1, HG_WIDTH), lambda b, c: (0, 0))],
        out_specs=[zspec(0), sspec],
        scratch_shapes=[pltpu.VMEM((HG_HEADS, HG_HEAD, HG_HEAD), F32)],
        compiler_params=_cp(("parallel", "arbitrary")),
        name="hgrn",
    )(s0, z_hg, z_hg, z_hg, lb[d:d + 1])


def _readout_kernel(x_ref, of_ref, ob_ref, bonus_ref, g_ref, hf_ref, hb_ref, og_ref, grw_ref, ghg_ref, gmx_ref,
                    lnw_ref, lnb_ref, hgn_ref, wrw_ref, whg_ref, wout_ref, o_ref, y_s):
    o_rw = of_ref[...] + ob_ref[...]
    for h in range(RW_HEADS):
        hs = slice(h * RW_HEAD, (h + 1) * RW_HEAD)
        oh = o_rw[:, hs]
        mu = jnp.mean(oh, axis=-1, keepdims=True)
        dv = oh - mu
        var = jnp.mean(dv * dv, axis=-1, keepdims=True)
        y_s[:, hs] = dv * lax.rsqrt(var + RW_GN_EPS)
    y_rw = (y_s[...] * lnw_ref[...] + lnb_ref[...] + bonus_ref[...]) * g_ref[...]
    branch_rw = _mm(y_rw, wrw_ref[...])

    o_hg = hf_ref[...] + hb_ref[...]
    for h in range(HG_HEADS):
        sl = slice(h * HG_HEAD, (h + 1) * HG_HEAD)
        oh = o_hg[:, sl]
        y_s[:, sl] = oh * lax.rsqrt(jnp.mean(oh * oh, axis=-1, keepdims=True) + NORM_EPS)
    og = og_ref[...]
    y_hg = y_s[...] * hgn_ref[...] * (og * _sigmoid(og))
    branch_hg = _mm(y_hg, whg_ref[...])

    merged = _sigmoid(grw_ref[...]) * branch_rw + _sigmoid(ghg_ref[...]) * branch_hg
    o_ref[...] = x_ref[...] + gmx_ref[...] * _mm(merged, wout_ref[...])


def _readout(x2d, o_f, o_b, bonus, g, h_f, h_b, z_hg, z_g, g_mx, ln_w, ln_b, hg_norm, w_rw, w_hg, w_out,
             rows_per_mod, tm=256):
    m, dm = x2d.shape
    bpm = rows_per_mod // tm
    tok = lambda width, blk=0: pl.BlockSpec((tm, width), lambda i: (i, blk))
    const = lambda shape: pl.BlockSpec(shape, lambda i: (0,) * len(shape), pipeline_mode=pl.Buffered(1))
    return pl.pallas_call(
        _readout_kernel,
        out_shape=jax.ShapeDtypeStruct((m, dm), F32),
        grid=(m // tm,),
        in_specs=[tok(dm), tok(RW_WIDTH), tok(RW_WIDTH), tok(RW_WIDTH), tok(RW_WIDTH), tok(HG_WIDTH), tok(HG_WIDTH),
                  tok(HG_WIDTH, 4), tok(dm, 0), tok(dm, 1),
                  pl.BlockSpec((None, 1, dm), lambda i: (i // bpm, 0, 0)),
                  const((1, RW_WIDTH)), const((1, RW_WIDTH)), const((1, HG_WIDTH)),
                  const((RW_WIDTH, dm)), const((HG_WIDTH, dm)), const((dm, dm))],
        out_specs=tok(dm),
        scratch_shapes=[pltpu.VMEM((tm, RW_WIDTH), F32)],
        compiler_params=_cp(("parallel",)),
        name="readout",
    )(x2d, o_f, o_b, bonus, g, h_f, h_b, z_hg, z_g, z_g, g_mx, ln_w, ln_b, hg_norm, w_rw, w_hg, w_out)


def _cand_groups():
    return [(i, PEER_TOPK // (i + 1)) for i in range(PEER_TOPK)]


def _topk_ranks(s, vals_ref):
    io = lax.broadcasted_iota(jnp.int32, s.shape, 0)

    def body(r, carry):
        work, rank = carry
        m = jnp.max(work, axis=0, keepdims=True)
        idx = jnp.min(jnp.where(work == m, io, s.shape[0]), axis=0, keepdims=True)
        sel = io == idx
        vals_ref[pl.ds(r, 1), :] = m
        return jnp.where(sel, -jnp.inf, work), jnp.where(sel, r.astype(F32), rank)

    _, rank = lax.fori_loop(0, PEER_TOPK, body, (s, jnp.full(s.shape, float(PEER_TOPK), F32)))
    return rank


def _peer_sel_kernel(x_ref, gain_ref, sh_ref, sc_ref, wq_ref, k1_ref, k2_ref, hx_ref, cnt_ref, rk_ref, e1_ref,
                     e2_ref, v1_s, v2_s):
    hb = _norm_mod(x_ref[...], gain_ref[...], sh_ref[...], sc_ref[...]).astype(BF16)
    hx_ref[...] = hb
    q = jnp.dot(hb, wq_ref[...], preferred_element_type=F32)
    tm = q.shape[0]
    groups = _cand_groups()
    ncand = sum(n for _, n in groups)
    npad = -ncand % 8
    for hd in range(PEER_HEADS):
        q1 = q[:, (2 * hd) * PEER_NKEYS:(2 * hd + 1) * PEER_NKEYS]
        q2 = q[:, (2 * hd + 1) * PEER_NKEYS:(2 * hd + 2) * PEER_NKEYS]
        s1 = _mm_nt(k1_ref[...], q1, exact=True)
        s2 = _mm_nt(k2_ref[...], q2, exact=True)
        rank1 = _topk_ranks(s1, v1_s)
        rank2 = _topk_ranks(s2, v2_s)
        v1 = v1_s[...]
        v2 = v2_s[...]
        rows = [v1[i:i + 1] + v2[0:n] for i, n in groups]
        if npad:
            rows.append(jnp.full((npad, tm), -jnp.inf, F32))
        cand = jnp.concatenate(rows, axis=0)
        io = lax.broadcasted_iota(jnp.int32, cand.shape, 0)
        best0 = v1[0:1] + v2[0:1]

        def body(r, carry):
            work, z = carry
            m = jnp.max(work, axis=0, keepdims=True)
            idx = jnp.min(jnp.where(work == m, io, cand.shape[0]), axis=0, keepdims=True)
            return jnp.where(io == idx, -jnp.inf, work), z + jnp.exp(m - best0)

        work, z = lax.fori_loop(0, PEER_TOPK, body, (cand, jnp.zeros((1, tm), F32)))
        taken = jnp.where((work == -jnp.inf) & (io < ncand), 1.0, 0.0)
        cnt1 = jnp.zeros_like(rank1)
        start = 0
        for i, n in groups:
            c_i = jnp.sum(taken[start:start + n], axis=0, keepdims=True)
            cnt1 = cnt1 + jnp.where(rank1 == float(i), c_i, 0.0)
            start += n
        cnt_ref[hd] = cnt1
        rk_ref[hd] = rank2
        e1_ref[hd] = jnp.exp(s1 - v1[0:1])
        e2_ref[hd] = jnp.exp(s2 - v2[0:1]) / z


def _peer_sel(x2d, gain, shift, scale, wq, k1, k2, rows_per_mod, tm=256):
    m, dm = x2d.shape
    bpm = rows_per_mod // tm
    mod_map = lambda i: (i // bpm, 0, 0)
    const = lambda shape: pl.BlockSpec(shape, lambda i: (0,) * len(shape))
    sel = jax.ShapeDtypeStruct((PEER_HEADS, PEER_NKEYS, m), F32)
    sel_spec = pl.BlockSpec((PEER_HEADS, PEER_NKEYS, tm), lambda i: (0, 0, i))
    return pl.pallas_call(
        _peer_sel_kernel,
        out_shape=[jax.ShapeDtypeStruct((m, dm), BF16), sel, sel, sel, sel],
        grid=(m // tm,),
        in_specs=[pl.BlockSpec((tm, dm), lambda i: (i, 0)), const((1, dm)),
                  pl.BlockSpec((None, 1, dm), mod_map), pl.BlockSpec((None, 1, dm), mod_map),
                  const(wq.shape), const(k1.shape), const(k2.shape)],
        out_specs=[pl.BlockSpec((tm, dm), lambda i: (i, 0)), sel_spec, sel_spec, sel_spec, sel_spec],
        scratch_shapes=[pltpu.VMEM((PEER_TOPK, tm), F32), pltpu.VMEM((PEER_TOPK, tm), F32)],
        compiler_params=_cp(("parallel",)),
        name="peer_sel",
    )(x2d, gain, shift, scale, wq, k1, k2)


def _gelu(x):
    return 0.5 * x * (1.0 + lax.erf(x * float(np.sqrt(0.5))))


def _peer_dense_kernel(hx_ref, u_ref, vt_ref, cnt_ref, rk_ref, e1_ref, e2_ref, o_ref, acc_s, w_s, *, te, nblk):
    e = pl.program_id(1)

    @pl.when(e == 0)
    def _():
        acc_s[...] = jnp.zeros_like(acc_s)

    act = _gelu(lax.dot_general(u_ref[...], hx_ref[...], (((1,), (1,)), ((), ())), preferred_element_type=F32))
    for ii in range(te // PEER_NKEYS):
        key1 = e * (te // PEER_NKEYS) + ii
        gate = None
        for hd in range(PEER_HEADS):
            cnt = cnt_ref[hd, pl.ds(key1, 1), :]
            e1 = e1_ref[hd, pl.ds(key1, 1), :]
            t = jnp.where(rk_ref[hd] < cnt, e2_ref[hd] * e1, 0.0)
            gate = t if gate is None else gate + t
        w_s[ii * PEER_NKEYS:(ii + 1) * PEER_NKEYS, :] = (gate * act[ii * PEER_NKEYS:(ii + 1) * PEER_NKEYS]).astype(BF16)
    acc_s[...] += jnp.dot(vt_ref[...], w_s[...], preferred_element_type=F32)

    @pl.when(e == nblk - 1)
    def _():
        o_ref[...] = acc_s[...].T


def _peer_dense(hx, u, vt, cnt, rk, e1, e2, tm=512, te=512):
    m, dm = hx.shape
    ne = u.shape[0]
    nblk = ne // te
    sel_spec = pl.BlockSpec((PEER_HEADS, PEER_NKEYS, tm), lambda i, e: (0, 0, i))
    return pl.pallas_call(
        functools.partial(_peer_dense_kernel, te=te, nblk=nblk),
        out_shape=jax.ShapeDtypeStruct((m, dm), F32),
        grid=(m // tm, nblk),
        in_specs=[pl.BlockSpec((tm, dm), lambda i, e: (i, 0)),
                  pl.BlockSpec((te, dm), lambda i, e: (e, 0)),
                  pl.BlockSpec((dm, te), lambda i, e: (0, e)),
                  sel_spec, sel_spec, sel_spec, sel_spec],
        out_specs=pl.BlockSpec((tm, dm), lambda i, e: (i, 0)),
        scratch_shapes=[pltpu.VMEM((dm, tm), F32), pltpu.VMEM((te, tm), BF16)],
        compiler_params=_cp(("parallel", "arbitrary")),
        name="peer_dense",
    )(hx, u, vt, cnt, rk, e1, e2)


def _final_kernel(x_ref, p_ref, g_ref, gain_ref, o_ref):
    x = x_ref[...] + g_ref[...] * p_ref[...]
    o_ref[...] = x * lax.rsqrt(jnp.mean(x * x, axis=-1, keepdims=True) + NORM_EPS) * gain_ref[...]


def _final(x2d, p2d, g_fx, gain, rows_per_mod, tm=512):
    m, dm = x2d.shape
    bpm = rows_per_mod // tm
    tok = pl.BlockSpec((tm, dm), lambda i: (i, 0))
    return pl.pallas_call(
        _final_kernel,
        out_shape=jax.ShapeDtypeStruct((m, dm), F32),
        grid=(m // tm,),
        in_specs=[tok, tok, pl.BlockSpec((None, 1, dm), lambda i: (i // bpm, 0, 0)),
                  pl.BlockSpec((1, dm), lambda i: (0, 0))],
        out_specs=tok,
        compiler_params=_cp(("parallel",)),
        name="final",
    )(x2d, p2d, g_fx, gain)


def kernel(x, c, ctx, c_ctx, w_ada, b_ada, norm_mix, norm_ffn, norm_final, w_in, rw_conv, rw_w0, rw_w_lora_b, rw_a0, rw_a_lora_b, rw_g_lora_b, rw_k_k, rw_k_a, rw_r_k, rw_ln_w, rw_ln_b, hg_lb, hg_norm, w_up_rw, w_up_hg, w_out, peer_wq, peer_k1, peer_k2, peer_u, peer_v):
    batch, seq, dm = x.shape
    ctx_len = ctx.shape[1]
    depth = w_in.shape[0]
    assert depth == 1 and dm == D_MODEL and seq % CHUNK == 0 and ctx_len % CHUNK == 0
    n_x = seq // CHUNK
    n_c = ctx_len // CHUNK
    l = 0

    rw_end, hg_end = RW_COLS, RW_COLS + 5 * HG_WIDTH
    w_rw_in = jnp.pad(w_in[l, :, :rw_end], ((0, 0), (0, RW_PAD - RW_COLS))).astype(BF16)
    w_hg_in = w_in[l, :, rw_end:hg_end].astype(BF16)
    w_g_in = w_in[l, :, hg_end:].astype(BF16)
    conv_w = jnp.pad(rw_conv[l].reshape(9, RW_COLS), ((0, 0), (0, RW_PAD - RW_COLS)))
    lower_bounds = jnp.cumsum(jax.nn.softmax(hg_lb.astype(F32), axis=0), axis=0)[l]
    row = lambda t: t.reshape(1, -1)
    hg_norm_row = jnp.tile(hg_norm[l], HG_HEADS).reshape(1, HG_WIDTH)
    u_bf = peer_u[l].astype(BF16)
    vt_bf = peer_v[l].astype(BF16).T

    cvec = jnp.concatenate([c, c_ctx[None, :], jnp.zeros((8 - batch - 1, dm), F32)], axis=0)
    mod = _ada(cvec, w_ada[l], row(b_ada[l]))
    mods = mod.reshape(8, N_MOD, 1, dm)
    sh_mx, sc_mx, g_mx, sh_fx, sc_fx, g_fx = (mods[:batch, i] for i in range(N_MOD))
    sh_mc, sc_mc = mods[batch:batch + 1, 0], mods[batch:batch + 1, 1]

    x2d = x.reshape(batch * seq, dm)
    c2d = ctx.reshape(batch * ctx_len, dm)
    gain_mix = row(norm_mix[l])

    zc_rw = _inproj(c2d, gain_mix, sh_mc, sc_mc, w_rw_in, batch * ctx_len, tm=batch * ctx_len)
    zc_hg = _inproj(c2d, gain_mix, sh_mc, sc_mc, w_hg_in, batch * ctx_len, tm=batch * ctx_len)
    g_lora = jnp.pad(rw_g_lora_b[l], ((0, RW_PAD - RW_COLS), (0, 0)))
    rw_args = (conv_w, rw_w0[l], rw_w_lora_b[l], rw_a0[l], rw_a_lora_b[l], g_lora, row(rw_k_k[l]),
               row(rw_k_a[l]), row(rw_r_k[l]))
    pre_c = _rwkv_pre(zc_rw, *rw_args, batch=batch, nchunks=n_c, is_ctx=True)
    zero_rw = jnp.zeros((batch, RW_HEADS, RW_HEAD, RW_HEAD), F32)
    zero_hg = jnp.zeros((batch, HG_HEADS, HG_HEAD, HG_HEAD), F32)
    _, st_rw_f = _rwkv_seq(zero_rw, *pre_c[0:4], batch=batch, nchunks=n_c, reverse=False)
    _, st_rw_b = _rwkv_seq(zero_rw, *pre_c[4:8], batch=batch, nchunks=n_c, reverse=True)
    _, st_hg_f = _hgrn(zero_hg, zc_hg, lower_bounds, batch, n_c, reverse=False)
    _, st_hg_b = _hgrn(zero_hg, zc_hg, lower_bounds, batch, n_c, reverse=True)

    z_rw = _inproj(x2d, gain_mix, sh_mx, sc_mx, w_rw_in, seq)
    z_hg = _inproj(x2d, gain_mix, sh_mx, sc_mx, w_hg_in, seq)
    z_g = _inproj(x2d, gain_mix, sh_mx, sc_mx, w_g_in, seq)
    pre = _rwkv_pre(z_rw, *rw_args, batch=batch, nchunks=n_x, is_ctx=False)
    o_f, _ = _rwkv_seq(st_rw_f, *pre[0:4], batch=batch, nchunks=n_x, reverse=False)
    o_b, _ = _rwkv_seq(st_rw_b, *pre[4:8], batch=batch, nchunks=n_x, reverse=True)
    h_f, _ = _hgrn(st_hg_f, z_hg, lower_bounds, batch, n_x, reverse=False)
    h_b, _ = _hgrn(st_hg_b, z_hg, lower_bounds, batch, n_x, reverse=True)
    x1 = _readout(x2d, o_f, o_b, pre[9], pre[8], h_f, h_b, z_hg, z_g, g_mx, row(rw_ln_w[l]), row(rw_ln_b[l]),
                  hg_norm_row, w_up_rw[l].astype(BF16), w_up_hg[l].astype(BF16), w_out[l].astype(BF16), seq)

    hx, cnt, rk, e1, e2 = _peer_sel(x1, row(norm_ffn[l]), sh_fx, sc_fx, peer_wq[l].astype(BF16), peer_k1[l],
                                    peer_k2[l], seq)
    p = _peer_dense(hx, u_bf, vt_bf, cnt, rk, e1, e2)
    out = _final(x1, p, g_fx, row(norm_final), seq)
    return out.reshape(batch, seq, dm)
```

```python
import functools

import jax
import jax.numpy as jnp
import numpy as np
from jax import lax
from jax.experimental import pallas as pl
from jax.experimental.pallas import tpu as pltpu

F32 = jnp.float32
BF16 = jnp.bfloat16
HIGHEST = lax.Precision.HIGHEST

D_MODEL = 2048
N_MOD = 6
NORM_EPS = 1e-6
CHUNK = 64
SUB = 16
RW_WIDTH = 1024
RW_HEAD = 64
RW_HEADS = 16
RW_LORA = 64
RW_GATE_RANK = 160
RW_COLS = 3488
RW_PAD = 3584
CONV_COLS = 512
RW_GN_EPS = 64e-5
HG_WIDTH = 1024
HG_HEAD = 128
HG_HEADS = 8
PEER_HEADS = 8
PEER_NKEYS = 128
PEER_TOPK = 16
BF16_ROWS = 16
EXP_M05 = float(np.exp(-0.5))
VMEM_LIMIT = 56 * 1024 * 1024


def _cp(sem):
    return pltpu.CompilerParams(dimension_semantics=sem, vmem_limit_bytes=VMEM_LIMIT)


def _mm(a, b, exact=False):
    if exact:
        return jnp.dot(a, b, preferred_element_type=F32, precision=HIGHEST)
    return jnp.dot(a.astype(BF16), b.astype(BF16), preferred_element_type=F32)


def _mm_nt(a, b, exact=False):
    dims = (((1,), (1,)), ((), ()))
    if exact:
        return lax.dot_general(a, b, dims, preferred_element_type=F32, precision=HIGHEST)
    return lax.dot_general(a.astype(BF16), b.astype(BF16), dims, preferred_element_type=F32)


def _mm_tn(a, b, exact=False):
    dims = (((0,), (0,)), ((), ()))
    if exact:
        return lax.dot_general(a, b, dims, preferred_element_type=F32, precision=HIGHEST)
    return lax.dot_general(a.astype(BF16), b.astype(BF16), dims, preferred_element_type=F32)


def _sigmoid(x):
    return 1.0 / (1.0 + jnp.exp(-x))


def _norm_mod(x, gain, shift, scale):
    y = x * lax.rsqrt(jnp.mean(x * x, axis=-1, keepdims=True) + NORM_EPS) * gain
    return y * (1.0 + scale) + shift


def _ada_kernel(c_ref, w_ref, b_ref, o_ref):
    cv = c_ref[...]
    o_ref[...] = _mm(cv * _sigmoid(cv), w_ref[...], exact=True) + b_ref[...]


def _ada(cvec, w, b, tn=512):
    k, n = w.shape
    return pl.pallas_call(
        _ada_kernel,
        out_shape=jax.ShapeDtypeStruct((cvec.shape[0], n), F32),
        grid=(n // tn,),
        in_specs=[pl.BlockSpec((cvec.shape[0], k), lambda j: (0, 0)),
                  pl.BlockSpec((k, tn), lambda j: (0, j)),
                  pl.BlockSpec((1, tn), lambda j: (0, j))],
        out_specs=pl.BlockSpec((cvec.shape[0], tn), lambda j: (0, j)),
        compiler_params=_cp(("parallel",)),
        name="ada",
    )(cvec, w, b)


def _inproj_kernel(x_ref, g_ref, sh_ref, sc_ref, w_ref, o_ref, h_scr):
    @pl.when(pl.program_id(1) == 0)
    def _():
        h_scr[...] = _norm_mod(x_ref[...], g_ref[...], sh_ref[...], sc_ref[...]).astype(BF16)

    o_ref[...] = jnp.dot(h_scr[...], w_ref[...], preferred_element_type=F32)


def _inproj(x2d, gain, shift, scale, w, rows_per_mod, tm=512, tn=512):
    m, k = x2d.shape
    n = w.shape[1]
    bpm = rows_per_mod // tm
    mod_map = lambda i, j: (i // bpm, 0, 0)
    return pl.pallas_call(
        _inproj_kernel,
        out_shape=jax.ShapeDtypeStruct((m, n), F32),
        grid=(m // tm, n // tn),
        in_specs=[pl.BlockSpec((tm, k), lambda i, j: (i, 0)),
                  pl.BlockSpec((1, k), lambda i, j: (0, 0)),
                  pl.BlockSpec((None, 1, k), mod_map),
                  pl.BlockSpec((None, 1, k), mod_map),
                  pl.BlockSpec((k, tn), lambda i, j: (0, j))],
        out_specs=pl.BlockSpec((tm, tn), lambda i, j: (i, j)),
        scratch_shapes=[pltpu.VMEM((tm, k), BF16)],
        compiler_params=_cp(("parallel", "arbitrary")),
        name="inproj",
    )(x2d, gain, shift, scale, w)


def _tri_masks(reverse):
    row = lax.broadcasted_iota(jnp.int32, (CHUNK, CHUNK), 0)
    col = lax.broadcasted_iota(jnp.int32, (CHUNK, CHUNK), 1)
    if reverse:
        strict, incl = col > row, col >= row
    else:
        strict, incl = col < row, col <= row
    same = (row // SUB) == (col // SUB)
    return strict, incl, same, row == col


def _rw_heads(ar_t, bk_t, a_0, r_0, bk_end, v_b, decay_tot, reverse, out_refs):
    strict, incl, same, eye = _tri_masks(reverse)
    row2 = lax.broadcasted_iota(jnp.int32, (CHUNK, 2 * CHUNK), 0)
    col2 = lax.broadcasted_iota(jnp.int32, (CHUNK, 2 * CHUNK), 1) % CHUNK
    incl2 = (col2 >= row2) if reverse else (col2 <= row2)
    ident = jnp.where(eye, 1.0, 0.0)
    heads = range(RW_HEADS)
    hs = lambda h: slice(h * RW_HEAD, (h + 1) * RW_HEAD)
    bf = lambda xs: [x.astype(BF16) for x in xs]
    mm = lambda xs, ys: [jnp.dot(x, y, preferred_element_type=F32) for x, y in zip(xs, ys)]
    nt_dims = (((1,), (1,)), ((), ()))
    tn_dims = (((0,), (0,)), ((), ()))

    s4 = [lax.dot_general(ar_t[:, hs(h)], bk_t[:, hs(h)], nt_dims, preferred_element_type=F32) for h in heads]
    m_ab = [jnp.where(strict, s[:CHUNK, :CHUNK], 0.0) for s in s4]
    m_ak = bf([jnp.where(strict, s[:CHUNK, CHUNK:], 0.0) for s in s4])
    m_r = bf([jnp.where(incl2, s[CHUNK:], 0.0) for s in s4])

    m_bd = [jnp.where(same, m, 0.0) for m in m_ab]
    m_off = bf([jnp.where(same, 0.0, m) for m in m_ab])
    m1 = bf(m_bd)
    m2 = bf(mm(m1, m1))
    m4 = bf(mm(m2, m2))
    m8 = bf(mm(m4, m4))
    p = [ident + m for m in m_bd]
    for mk in (m2, m4, m8):
        p = [x + y for x, y in zip(p, mm(bf(p), mk))]
    p_b = bf(p)
    n1 = mm(p_b, m_off)
    n1_b = bf(n1)
    n2 = mm(n1_b, n1_b)
    n3 = mm(n1_b, bf(n2))
    t_inv = bf(mm(bf([ident + a + b + c for a, b, c in zip(n1, n2, n3)]), p_b))

    x = bf(mm(m_ak, [v_b[:, hs(h)] for h in heads]))
    wu = bf(mm(t_inv, [jnp.concatenate([a_0[:, hs(h)], xh], axis=1) for h, xh in zip(heads, x)]))
    zero = jnp.zeros((CHUNK, RW_HEAD), BF16)
    rhs = [jnp.concatenate([w, jnp.concatenate([zero, v_b[:, hs(h)]], axis=1)], axis=0) for h, w in zip(heads, wu)]
    qo = mm(m_r, rhs)
    gd = [lax.dot_general(bk_end[:, hs(h)], r, tn_dims, preferred_element_type=F32) for h, r in zip(heads, rhs)]
    q_ref, o_ref, g_ref, d_ref = out_refs
    for h in heads:
        q_ref[:, hs(h)] = r_0[:, hs(h)] + qo[h][:, :RW_HEAD]
        o_ref[:, hs(h)] = qo[h][:, RW_HEAD:]
        g_ref[:, hs(h)] = gd[h][:, :RW_HEAD] + jnp.where(eye, decay_tot[:, hs(h)], 0.0)
        d_ref[:, hs(h)] = gd[h][:, RW_HEAD:]


def _rwkv_pre_kernel(zp_ref, zc_ref, zn_ref, conv_ref, w0_ref, wl_ref, a0_ref, al_ref, gl_ref, kk_ref, ka_ref,
                     rk_ref, qf_ref, of_ref, gf_ref, df_ref, qb_ref, ob_ref, gb_ref, db_ref, g_ref, bonus_ref,
                     conv_s, *, is_ctx, nchunks):
    c = pl.program_id(1)
    row = lax.broadcasted_iota(jnp.int32, (CHUNK, 1), 0)
    has_p = (c > 0).astype(F32)
    has_n = (c < nchunks - 1).astype(F32)

    def left(z, fill):
        return jnp.where(row == 0, fill, pltpu.roll(z, 1, 0))

    def right(z, fill):
        return jnp.where(row == CHUNK - 1, fill, pltpu.roll(z, CHUNK - 1, 0))

    for cb in range(RW_PAD // CONV_COLS):
        cs = slice(cb * CONV_COLS, (cb + 1) * CONV_COLS)
        w = conv_ref[:, cs]
        zc = zc_ref[:, cs]
        if is_ctx:
            conv = (left(zc, zp_ref[CHUNK - 1:CHUNK, cs] * has_p) * w[3:4] + zc * w[4:5]
                    + right(zc, zn_ref[0:1, cs] * has_n) * w[5:6])
        else:
            conv = None
            for dy, z in enumerate((zp_ref[:, cs] * has_p, zc, zn_ref[:, cs] * has_n)):
                t = (left(z, 0.0) * w[3 * dy:3 * dy + 1] + z * w[3 * dy + 1:3 * dy + 2]
                     + right(z, 0.0) * w[3 * dy + 2:3 * dy + 3])
                conv = t if conv is None else conv + t
        conv_s[:, cs] = conv

    r = conv_s[:, 0:RW_WIDTH]
    k = conv_s[:, RW_WIDTH:2 * RW_WIDTH]
    v = conv_s[:, 2 * RW_WIDTH:3 * RW_WIDTH]
    lo = conv_s[:, 3 * RW_WIDTH:RW_PAD]
    g_ref[...] = _mm(_sigmoid(lo[:, 4 * RW_LORA:]), gl_ref[...], exact=True)

    first = lax.broadcasted_iota(jnp.int32, (1, 2 * RW_HEAD), 1) < RW_HEAD

    def head_sum(t):
        tiles = []
        for i in range(RW_WIDTH // (2 * RW_HEAD)):
            ti = t[:, i * 2 * RW_HEAD:(i + 1) * 2 * RW_HEAD]
            s0 = jnp.sum(jnp.where(first, ti, 0.0), axis=-1, keepdims=True)
            s1 = jnp.sum(jnp.where(first, 0.0, ti), axis=-1, keepdims=True)
            tiles.append(jnp.where(first, s0, s1))
        return jnp.concatenate(tiles, axis=1)

    kkr = k * kk_ref[...]
    kk = kkr / jnp.maximum(jnp.sqrt(head_sum(kkr * kkr)), 1e-12)
    v_b = v.astype(BF16)
    outs = ((qf_ref, of_ref, gf_ref, df_ref), (qb_ref, ob_ref, gb_ref, db_ref))
    trow = lax.broadcasted_iota(jnp.int32, (CHUNK, CHUNK), 0)
    tcol = lax.broadcasted_iota(jnp.int32, (CHUNK, CHUNK), 1)
    bonus = None
    for d in range(2):
        wa = lo[:, d * RW_LORA:(d + 1) * RW_LORA]
        aa = lo[:, (2 + d) * RW_LORA:(3 + d) * RW_LORA]
        w_pre = w0_ref[d:d + 1, :] + _mm(jnp.tanh(wa), wl_ref[d], exact=True)
        logw = -EXP_M05 * _sigmoid(w_pre)
        a = _sigmoid(a0_ref[d:d + 1, :] + _mm(aa, al_ref[d], exact=True))
        tri = jnp.where((tcol >= trow) if d == 1 else (tcol <= trow), 1.0, 0.0)
        cum = _mm(tri, logw, exact=True)
        prev = cum - logw
        mid = CHUNK // 2 - 1 if d == 1 else CHUNK // 2
        end = 0 if d == 1 else CHUNK - 1
        rho = cum[mid:mid + 1]
        tot = cum[end:end + 1]
        kd = k * (1.0 + (a - 1.0) * ka_ref[...])
        kka = kk * a
        hb = head_sum(r * kd * rk_ref[...])
        bonus = hb if bonus is None else bonus + hb
        e_out = jnp.exp(rho - cum)
        e_end = jnp.exp(tot - cum)
        ar_t = jnp.concatenate([-kk * jnp.exp(prev - rho), r * jnp.exp(cum - rho)], axis=0).astype(BF16)
        bk_t = jnp.concatenate([kka * e_out, kd * e_out], axis=0).astype(BF16)
        bk_end = jnp.concatenate([kka * e_end, kd * e_end], axis=0).astype(BF16)
        a_0 = (-kk * jnp.exp(prev)).astype(BF16)
        r_0 = r * jnp.exp(cum)
        _rw_heads(ar_t, bk_t, a_0, r_0, bk_end, v_b, jnp.exp(tot), d == 1, outs[d])
    bonus_ref[...] = bonus * v


def _rwkv_pre(z_rw, conv_w, w0, wl, a0, al, gl, k_k, k_a, r_k, batch, nchunks, is_ctx):
    rows = z_rw.shape[0]
    last = nchunks - 1
    zspec = lambda f: pl.BlockSpec((CHUNK, RW_PAD), lambda b, c: (b * nchunks + f(c), 0))
    full = lambda shape: pl.BlockSpec(shape, lambda b, c: (0,) * len(shape))
    ospec = pl.BlockSpec((CHUNK, RW_WIDTH), lambda b, c: (b * nchunks + c, 0))
    out = jax.ShapeDtypeStruct((rows, RW_WIDTH), F32)
    return pl.pallas_call(
        functools.partial(_rwkv_pre_kernel, is_ctx=is_ctx, nchunks=nchunks),
        out_shape=[out] * 10,
        grid=(batch, nchunks),
        in_specs=[zspec(lambda c: jnp.maximum(c - 1, 0)), zspec(lambda c: c), zspec(lambda c: jnp.minimum(c + 1, last)),
                  full((9, RW_PAD)), full((2, RW_WIDTH)), full((2, RW_LORA, RW_WIDTH)), full((2, RW_WIDTH)),
                  full((2, RW_LORA, RW_WIDTH)), full((RW_PAD - 3 * RW_WIDTH - 4 * RW_LORA, RW_WIDTH)), full((1, RW_WIDTH)),
                  full((1, RW_WIDTH)), full((1, RW_WIDTH))],
        out_specs=[ospec] * 10,
        scratch_shapes=[pltpu.VMEM((CHUNK, RW_PAD), F32)],
        compiler_params=_cp(("parallel", "parallel")),
        name="rwkv_pre",
    )(z_rw, z_rw, z_rw, conv_w, w0, wl, a0, al, gl, k_k, k_a, r_k)


def _rwkv_seq_kernel(x0_ref, q_ref, o_ref, g_ref, d_ref, out_ref, xfin_ref, x_s, *, nchunks):
    c = pl.program_id(1)

    @pl.when(c == 0)
    def _():
        x_s[...] = x0_ref[...]

    for h in range(RW_HEADS):
        hs = slice(h * RW_HEAD, (h + 1) * RW_HEAD)
        res = _mm(jnp.concatenate([q_ref[:, hs], g_ref[:, hs]], axis=0), x_s[h], exact=True)
        out_ref[:, hs] = res[:CHUNK] + o_ref[:, hs]
        x_s[h] = res[CHUNK:] + d_ref[:, hs]

    @pl.when(c == nchunks - 1)
    def _():
        xfin_ref[...] = x_s[...]


def _rwkv_seq(x0, qh, oh, gt, dt, batch, nchunks, reverse):
    pos = (lambda c: nchunks - 1 - c) if reverse else (lambda c: c)
    spec = pl.BlockSpec((CHUNK, RW_WIDTH), lambda b, c: (b * nchunks + pos(c), 0))
    sspec = pl.BlockSpec((None, RW_HEADS, RW_HEAD, RW_HEAD), lambda b, c: (b, 0, 0, 0))
    return pl.pallas_call(
        functools.partial(_rwkv_seq_kernel, nchunks=nchunks),
        out_shape=[jax.ShapeDtypeStruct(qh.shape, F32), jax.ShapeDtypeStruct(x0.shape, F32)],
        grid=(batch, nchunks),
        in_specs=[sspec, spec, spec, spec, spec],
        out_specs=[spec, sspec],
        scratch_shapes=[pltpu.VMEM((RW_HEADS, RW_HEAD, RW_HEAD), F32)],
        compiler_params=_cp(("parallel", "arbitrary")),
        name="rwkv_seq",
    )(x0, qh, oh, gt, dt)


def _hgrn_kernel(s0_ref, q_ref, f_ref, i_ref, lb_ref, out_ref, sfin_ref, st_s, *, nchunks, reverse):
    c = pl.program_id(1)

    @pl.when(c == 0)
    def _():
        st_s[...] = s0_ref[...]

    row = lax.broadcasted_iota(jnp.int32, (CHUNK, CHUNK), 0)
    col = lax.broadcasted_iota(jnp.int32, (CHUNK, CHUNK), 1)
    incl = (col >= row) if reverse else (col <= row)
    mid = CHUNK // 2 - 1 if reverse else CHUNK // 2
    end = 0 if reverse else CHUNK - 1

    lb = lb_ref[...]
    forget = lb + (1.0 - lb) * _sigmoid(f_ref[...])
    kf = 1.0 - forget
    qraw = q_ref[...]
    q = qraw * _sigmoid(qraw)
    v = i_ref[...]
    cum = _mm(jnp.where(incl, 1.0, 0.0), jnp.log(forget), exact=True)
    ref = cum[mid:mid + 1]
    tot = cum[end:end + 1]
    qs = q * jnp.exp(cum - ref)
    ks = kf * jnp.exp(ref - cum)
    qe = q * jnp.exp(cum)
    ke = kf * jnp.exp(tot - cum)
    dec = jnp.exp(tot)
    for h in range(HG_HEADS):
        sl = slice(h * HG_HEAD, (h + 1) * HG_HEAD)
        st = st_s[h]
        scores = jnp.where(incl, _mm_nt(qs[:, sl], ks[:, sl]), 0.0)
        out_ref[:, sl] = _mm(scores, v[:, sl]) + _mm_nt(qe[:, sl], st)
        st_s[h] = st * dec[:, sl] + _mm_tn(v[:, sl], ke[:, sl])

    @pl.when(c == nchunks - 1)
    def _():
        sfin_ref[...] = st_s[...]


def _hgrn(s0, z_hg, lb, batch, nchunks, reverse):
    d = 1 if reverse else 0
    pos = (lambda c: nchunks - 1 - c) if reverse else (lambda c: c)
    zspec = lambda blk: pl.BlockSpec((CHUNK, HG_WIDTH), lambda b, c: (b * nchunks + pos(c), blk))
    sspec = pl.BlockSpec((None, HG_HEADS, HG_HEAD, HG_HEAD), lambda b, c: (b, 0, 0, 0))
    return pl.pallas_call(
        functools.partial(_hgrn_kernel, nchunks=nchunks, reverse=reverse),
        out_shape=[jax.ShapeDtypeStruct((z_hg.shape[0], HG_WIDTH), F32), jax.ShapeDtypeStruct(s0.shape, F32)],
        grid=(batch, nchunks),
        in_specs=[sspec, zspec(0), zspec(1 + d), zspec(3), pl.BlockSpec((1, HG_WIDTH), lambda b, c: (0, 0))],
        out_specs=[zspec(0), sspec],
        scratch_shapes=[pltpu.VMEM((HG_HEADS, HG_HEAD, HG_HEAD), F32)],
        compiler_params=_cp(("parallel", "arbitrary")),
        name="hgrn",
    )(s0, z_hg, z_hg, z_hg, lb[d:d + 1])


def _readout_kernel(x_ref, of_ref, ob_ref, bonus_ref, g_ref, hf_ref, hb_ref, og_ref, grw_ref, ghg_ref, gmx_ref,
                    lnw_ref, lnb_ref, hgn_ref, wrw_ref, whg_ref, wout_ref, o_ref, y_s):
    o_rw = of_ref[...] + ob_ref[...]
    for h in range(RW_HEADS):
        hs = slice(h * RW_HEAD, (h + 1) * RW_HEAD)
        oh = o_rw[:, hs]
        mu = jnp.mean(oh, axis=-1, keepdims=True)
        dv = oh - mu
        var = jnp.mean(dv * dv, axis=-1, keepdims=True)
        y_s[:, hs] = dv * lax.rsqrt(var + RW_GN_EPS)
    y_rw = (y_s[...] * lnw_ref[...] + lnb_ref[...] + bonus_ref[...]) * g_ref[...]
    branch_rw = _mm(y_rw, wrw_ref[...])

    o_hg = hf_ref[...] + hb_ref[...]
    for h in range(HG_HEADS):
        sl = slice(h * HG_HEAD, (h + 1) * HG_HEAD)
        oh = o_hg[:, sl]
        y_s[:, sl] = oh * lax.rsqrt(jnp.mean(oh * oh, axis=-1, keepdims=True) + NORM_EPS)
    og = og_ref[...]
    y_hg = y_s[...] * hgn_ref[...] * (og * _sigmoid(og))
    branch_hg = _mm(y_hg, whg_ref[...])

    merged = _sigmoid(grw_ref[...]) * branch_rw + _sigmoid(ghg_ref[...]) * branch_hg
    o_ref[...] = x_ref[...] + gmx_ref[...] * _mm(merged, wout_ref[...])


def _readout(x2d, o_f, o_b, bonus, g, h_f, h_b, z_hg, z_g, g_mx, ln_w, ln_b, hg_norm, w_rw, w_hg, w_out,
             rows_per_mod, tm=256):
    m, dm = x2d.shape
    bpm = rows_per_mod // tm
    tok = lambda width, blk=0: pl.BlockSpec((tm, width), lambda i: (i, blk))
    const = lambda shape: pl.BlockSpec(shape, lambda i: (0,) * len(shape), pipeline_mode=pl.Buffered(1))
    return pl.pallas_call(
        _readout_kernel,
        out_shape=jax.ShapeDtypeStruct((m, dm), F32),
        grid=(m // tm,),
        in_specs=[tok(dm), tok(RW_WIDTH), tok(RW_WIDTH), tok(RW_WIDTH), tok(RW_WIDTH), tok(HG_WIDTH), tok(HG_WIDTH),
                  tok(HG_WIDTH, 4), tok(dm, 0), tok(dm, 1),
                  pl.BlockSpec((None, 1, dm), lambda i: (i // bpm, 0, 0)),
                  const((1, RW_WIDTH)), const((1, RW_WIDTH)), const((1, HG_WIDTH)),
                  const((RW_WIDTH, dm)), const((HG_WIDTH, dm)), const((dm, dm))],
        out_specs=tok(dm),
        scratch_shapes=[pltpu.VMEM((tm, RW_WIDTH), F32)],
        compiler_params=_cp(("parallel",)),
        name="readout",
    )(x2d, o_f, o_b, bonus, g, h_f, h_b, z_hg, z_g, z_g, g_mx, ln_w, ln_b, hg_norm, w_rw, w_hg, w_out)


def _cand_groups():
    return [(i, PEER_TOPK // (i + 1)) for i in range(PEER_TOPK)]


def _topk_ranks(s, vals_ref):
    io = lax.broadcasted_iota(jnp.int32, s.shape, 0)

    def body(r, carry):
        work, rank = carry
        m = jnp.max(work, axis=0, keepdims=True)
        idx = jnp.min(jnp.where(work == m, io, s.shape[0]), axis=0, keepdims=True)
        sel = io == idx
        vals_ref[pl.ds(r, 1), :] = m
        return jnp.where(sel, -jnp.inf, work), jnp.where(sel, r.astype(F32), rank)

    _, rank = lax.fori_loop(0, PEER_TOPK, body, (s, jnp.full(s.shape, float(PEER_TOPK), F32)))
    return rank


def _peer_sel_kernel(x_ref, gain_ref, sh_ref, sc_ref, wq_ref, k1_ref, k2_ref, hx_ref, cnt_ref, rk_ref, e1_ref,
                     e2_ref, v1_s, v2_s):
    hb = _norm_mod(x_ref[...], gain_ref[...], sh_ref[...], sc_ref[...]).astype(BF16)
    hx_ref[...] = hb
    q = jnp.dot(hb, wq_ref[...], preferred_element_type=F32)
    tm = q.shape[0]
    groups = _cand_groups()
    ncand = sum(n for _, n in groups)
    npad = -ncand % 8
    for hd in range(PEER_HEADS):
        q1 = q[:, (2 * hd) * PEER_NKEYS:(2 * hd + 1) * PEER_NKEYS]
        q2 = q[:, (2 * hd + 1) * PEER_NKEYS:(2 * hd + 2) * PEER_NKEYS]
        s1 = _mm_nt(k1_ref[...], q1, exact=True)
        s2 = _mm_nt(k2_ref[...], q2, exact=True)
        rank1 = _topk_ranks(s1, v1_s)
        rank2 = _topk_ranks(s2, v2_s)
        v1 = v1_s[...]
        v2 = v2_s[...]
        rows = [v1[i:i + 1] + v2[0:n] for i, n in groups]
        if npad:
            rows.append(jnp.full((npad, tm), -jnp.inf, F32))
        cand = jnp.concatenate(rows, axis=0)
        io = lax.broadcasted_iota(jnp.int32, cand.shape, 0)
        best0 = v1[0:1] + v2[0:1]

        def body(r, carry):
            work, z = carry
            m = jnp.max(work, axis=0, keepdims=True)
            idx = jnp.min(jnp.where(work == m, io, cand.shape[0]), axis=0, keepdims=True)
            return jnp.where(io == idx, -jnp.inf, work), z + jnp.exp(m - best0)

        work, z = lax.fori_loop(0, PEER_TOPK, body, (cand, jnp.zeros((1, tm), F32)))
        taken = jnp.where((work == -jnp.inf) & (io < ncand), 1.0, 0.0)
        cnt1 = jnp.zeros_like(rank1)
        start = 0
        for i, n in groups:
            c_i = jnp.sum(taken[start:start + n], axis=0, keepdims=True)
            cnt1 = cnt1 + jnp.where(rank1 == float(i), c_i, 0.0)
            start += n
        cnt_ref[hd] = cnt1
        rk_ref[hd] = rank2.astype(BF16)
        e1_ref[hd] = jnp.exp(s1 - v1[0:1])
        e2_ref[hd] = (jnp.exp(s2 - v2[0:1]) / z).astype(BF16)


def _peer_sel(x2d, gain, shift, scale, wq, k1, k2, rows_per_mod, tm=256):
    m, dm = x2d.shape
    bpm = rows_per_mod // tm
    mod_map = lambda i: (i // bpm, 0, 0)
    const = lambda shape: pl.BlockSpec(shape, lambda i: (0,) * len(shape))
    sel = jax.ShapeDtypeStruct((PEER_HEADS, PEER_NKEYS, m), F32)
    sel_bf = jax.ShapeDtypeStruct((PEER_HEADS, PEER_NKEYS, m), BF16)
    sel_spec = pl.BlockSpec((PEER_HEADS, PEER_NKEYS, tm), lambda i: (0, 0, i))
    return pl.pallas_call(
        _peer_sel_kernel,
        out_shape=[jax.ShapeDtypeStruct((m, dm), BF16), sel, sel_bf, sel, sel_bf],
        grid=(m // tm,),
        in_specs=[pl.BlockSpec((tm, dm), lambda i: (i, 0)), const((1, dm)),
                  pl.BlockSpec((None, 1, dm), mod_map), pl.BlockSpec((None, 1, dm), mod_map),
                  const(wq.shape), const(k1.shape), const(k2.shape)],
        out_specs=[pl.BlockSpec((tm, dm), lambda i: (i, 0)), sel_spec, sel_spec, sel_spec, sel_spec],
        scratch_shapes=[pltpu.VMEM((PEER_TOPK, tm), F32), pltpu.VMEM((PEER_TOPK, tm), F32)],
        compiler_params=_cp(("parallel",)),
        name="peer_sel",
    )(x2d, gain, shift, scale, wq, k1, k2)


def _gelu(x):
    return 0.5 * x * (1.0 + lax.erf(x * float(np.sqrt(0.5))))


def _peer_dense_kernel(hx_ref, u_ref, vt_ref, cnt_ref, rk_ref, e1_ref, e2_ref, o_ref, acc_s, wa_s, wb_s, ga_s, gb_s,
                       *, te, nblk):
    e = pl.program_id(1)
    half = te // 2
    nsub = half // PEER_NKEYS
    tm = hx_ref.shape[0]

    def gates(half_blk, dst):
        packed = (PEER_NKEYS // BF16_ROWS, BF16_ROWS, tm)
        for ii in range(nsub):
            key1 = half_blk * nsub + ii
            gate = None
            for hd in range(PEER_HEADS):
                cnt = jnp.broadcast_to(cnt_ref[hd, pl.ds(key1, 1), :], packed[1:]).astype(BF16)
                e1 = jnp.broadcast_to(e1_ref[hd, pl.ds(key1, 1), :], packed[1:]).astype(BF16)
                t = jnp.where(rk_ref[hd].reshape(packed) < cnt[None], e2_ref[hd].reshape(packed) * e1[None],
                              jnp.zeros((), BF16))
                gate = t if gate is None else gate + t
            dst[ii * PEER_NKEYS:(ii + 1) * PEER_NKEYS, :] = gate.reshape(PEER_NKEYS, tm)

    @pl.when(e == 0)
    def _():
        acc_s[...] = jnp.zeros_like(acc_s)
        gates(0, ga_s)

    nt_dims = (((1,), (1,)), ((), ()))
    hx = hx_ref[...]
    gates(2 * e + 1, gb_s)
    act_a = _gelu(lax.dot_general(u_ref[:half, :], hx, nt_dims, preferred_element_type=F32))
    wa_s[...] = ga_s[...] * act_a.astype(BF16)
    act_b = _gelu(lax.dot_general(u_ref[half:, :], hx, nt_dims, preferred_element_type=F32))
    acc_s[...] += jnp.dot(vt_ref[:, :half], wa_s[...], preferred_element_type=F32)
    wb_s[...] = gb_s[...] * act_b.astype(BF16)
    gates(2 * ((e + 1) % nblk), ga_s)
    acc_s[...] += jnp.dot(vt_ref[:, half:], wb_s[...], preferred_element_type=F32)

    @pl.when(e == nblk - 1)
    def _():
        o_ref[...] = acc_s[...].T


def _peer_dense(hx, u, vt, cnt, rk, e1, e2, tm=512, te=1024):
    m, dm = hx.shape
    ne = u.shape[0]
    nblk = ne // te
    sel_spec = pl.BlockSpec((PEER_HEADS, PEER_NKEYS, tm), lambda i, e: (0, 0, i))
    return pl.pallas_call(
        functools.partial(_peer_dense_kernel, te=te, nblk=nblk),
        out_shape=jax.ShapeDtypeStruct((m, dm), F32),
        grid=(m // tm, nblk),
        in_specs=[pl.BlockSpec((tm, dm), lambda i, e: (i, 0)),
                  pl.BlockSpec((te, dm), lambda i, e: (e, 0)),
                  pl.BlockSpec((dm, te), lambda i, e: (0, e)),
                  sel_spec, sel_spec, sel_spec, sel_spec],
        out_specs=pl.BlockSpec((tm, dm), lambda i, e: (i, 0)),
        scratch_shapes=[pltpu.VMEM((dm, tm), F32)] + [pltpu.VMEM((te // 2, tm), BF16)] * 4,
        compiler_params=_cp(("parallel", "arbitrary")),
        name="peer_dense",
    )(hx, u, vt, cnt, rk, e1, e2)


def _final_kernel(x_ref, p_ref, g_ref, gain_ref, o_ref):
    x = x_ref[...] + g_ref[...] * p_ref[...]
    o_ref[...] = x * lax.rsqrt(jnp.mean(x * x, axis=-1, keepdims=True) + NORM_EPS) * gain_ref[...]


def _final(x2d, p2d, g_fx, gain, rows_per_mod, tm=512):
    m, dm = x2d.shape
    bpm = rows_per_mod // tm
    tok = pl.BlockSpec((tm, dm), lambda i: (i, 0))
    return pl.pallas_call(
        _final_kernel,
        out_shape=jax.ShapeDtypeStruct((m, dm), F32),
        grid=(m // tm,),
        in_specs=[tok, tok, pl.BlockSpec((None, 1, dm), lambda i: (i // bpm, 0, 0)),
                  pl.BlockSpec((1, dm), lambda i: (0, 0))],
        out_specs=tok,
        compiler_params=_cp(("parallel",)),
        name="final",
    )(x2d, p2d, g_fx, gain)


def kernel(x, c, ctx, c_ctx, w_ada, b_ada, norm_mix, norm_ffn, norm_final, w_in, rw_conv, rw_w0, rw_w_lora_b, rw_a0, rw_a_lora_b, rw_g_lora_b, rw_k_k, rw_k_a, rw_r_k, rw_ln_w, rw_ln_b, hg_lb, hg_norm, w_up_rw, w_up_hg, w_out, peer_wq, peer_k1, peer_k2, peer_u, peer_v):
    batch, seq, dm = x.shape
    ctx_len = ctx.shape[1]
    depth = w_in.shape[0]
    assert depth == 1 and dm == D_MODEL and seq % CHUNK == 0 and ctx_len % CHUNK == 0
    n_x = seq // CHUNK
    n_c = ctx_len // CHUNK
    l = 0

    rw_end, hg_end = RW_COLS, RW_COLS + 5 * HG_WIDTH
    w_rw_in = jnp.pad(w_in[l, :, :rw_end], ((0, 0), (0, RW_PAD - RW_COLS))).astype(BF16)
    w_hg_in = w_in[l, :, rw_end:hg_end].astype(BF16)
    w_g_in = w_in[l, :, hg_end:].astype(BF16)
    conv_w = jnp.pad(rw_conv[l].reshape(9, RW_COLS), ((0, 0), (0, RW_PAD - RW_COLS)))
    lower_bounds = jnp.cumsum(jax.nn.softmax(hg_lb.astype(F32), axis=0), axis=0)[l]
    row = lambda t: t.reshape(1, -1)
    hg_norm_row = jnp.tile(hg_norm[l], HG_HEADS).reshape(1, HG_WIDTH)
    u_bf = peer_u[l].astype(BF16)
    vt_bf = peer_v[l].astype(BF16).T

    cvec = jnp.concatenate([c, c_ctx[None, :], jnp.zeros((8 - batch - 1, dm), F32)], axis=0)
    mod = _ada(cvec, w_ada[l], row(b_ada[l]))
    mods = mod.reshape(8, N_MOD, 1, dm)
    sh_mx, sc_mx, g_mx, sh_fx, sc_fx, g_fx = (mods[:batch, i] for i in range(N_MOD))
    sh_mc, sc_mc = mods[batch:batch + 1, 0], mods[batch:batch + 1, 1]

    x2d = x.reshape(batch * seq, dm)
    c2d = ctx.reshape(batch * ctx_len, dm)
    gain_mix = row(norm_mix[l])

    zc_rw = _inproj(c2d, gain_mix, sh_mc, sc_mc, w_rw_in, batch * ctx_len, tm=batch * ctx_len)
    zc_hg = _inproj(c2d, gain_mix, sh_mc, sc_mc, w_hg_in, batch * ctx_len, tm=batch * ctx_len)
    g_lora = jnp.pad(rw_g_lora_b[l], ((0, RW_PAD - RW_COLS), (0, 0)))
    rw_args = (conv_w, rw_w0[l], rw_w_lora_b[l], rw_a0[l], rw_a_lora_b[l], g_lora, row(rw_k_k[l]),
               row(rw_k_a[l]), row(rw_r_k[l]))
    pre_c = _rwkv_pre(zc_rw, *rw_args, batch=batch, nchunks=n_c, is_ctx=True)
    zero_rw = jnp.zeros((batch, RW_HEADS, RW_HEAD, RW_HEAD), F32)
    zero_hg = jnp.zeros((batch, HG_HEADS, HG_HEAD, HG_HEAD), F32)
    _, st_rw_f = _rwkv_seq(zero_rw, *pre_c[0:4], batch=batch, nchunks=n_c, reverse=False)
    _, st_rw_b = _rwkv_seq(zero_rw, *pre_c[4:8], batch=batch, nchunks=n_c, reverse=True)
    _, st_hg_f = _hgrn(zero_hg, zc_hg, lower_bounds, batch, n_c, reverse=False)
    _, st_hg_b = _hgrn(zero_hg, zc_hg, lower_bounds, batch, n_c, reverse=True)

    z_rw = _inproj(x2d, gain_mix, sh_mx, sc_mx, w_rw_in, seq)
    z_hg = _inproj(x2d, gain_mix, sh_mx, sc_mx, w_hg_in, seq)
    z_g = _inproj(x2d, gain_mix, sh_mx, sc_mx, w_g_in, seq)
    pre = _rwkv_pre(z_rw, *rw_args, batch=batch, nchunks=n_x, is_ctx=False)
    o_f, _ = _rwkv_seq(st_rw_f, *pre[0:4], batch=batch, nchunks=n_x, reverse=False)
    o_b, _ = _rwkv_seq(st_rw_b, *pre[4:8], batch=batch, nchunks=n_x, reverse=True)
    h_f, _ = _hgrn(st_hg_f, z_hg, lower_bounds, batch, n_x, reverse=False)
    h_b, _ = _hgrn(st_hg_b, z_hg, lower_bounds, batch, n_x, reverse=True)
    x1 = _readout(x2d, o_f, o_b, pre[9], pre[8], h_f, h_b, z_hg, z_g, g_mx, row(rw_ln_w[l]), row(rw_ln_b[l]),
                  hg_norm_row, w_up_rw[l].astype(BF16), w_up_hg[l].astype(BF16), w_out[l].astype(BF16), seq)

    hx, cnt, rk, e1, e2 = _peer_sel(x1, row(norm_ffn[l]), sh_fx, sc_fx, peer_wq[l].astype(BF16), peer_k1[l],
                                    peer_k2[l], seq)
    p = _peer_dense(hx, u_bf, vt_bf, cnt, rk, e1, e2)
    out = _final(x1, p, g_fx, row(norm_final), seq)
    return out.reshape(batch, seq, dm)
```

```python
import functools

import jax
import jax.numpy as jnp
import numpy as np
from jax import lax
from jax.experimental import pallas as pl
from jax.experimental.pallas import tpu as pltpu

F32 = jnp.float32
BF16 = jnp.bfloat16

D_MODEL = 2048
N_MOD = 6
NORM_EPS = 1e-6
CHUNK = 64
SUB = 16
SEQ_CHUNKS = 4
RW_WIDTH = 1024
RW_HEAD = 64
RW_HEADS = 16
RW_LORA = 64
RW_GATE_RANK = 160
RW_COLS = 3488
RW_PAD = 3584
CONV_COLS = 512
RW_GN_EPS = 64e-5
HG_WIDTH = 1024
HG_HEAD = 128
HG_HEADS = 8
PEER_HEADS = 8
PEER_NKEYS = 128
PEER_TOPK = 16
BF16_ROWS = 16
EXP_M05 = float(np.exp(-0.5))
VMEM_LIMIT = 56 * 1024 * 1024


def _cp(sem):
    return pltpu.CompilerParams(dimension_semantics=sem, vmem_limit_bytes=VMEM_LIMIT)


def _split(x):
    hi = x.astype(BF16)
    return hi, (x - hi.astype(F32)).astype(BF16)


def _dot(a, b, dims, exact):
    dg = lambda p, q: lax.dot_general(p, q, dims, preferred_element_type=F32)
    if not exact:
        return dg(a.astype(BF16), b.astype(BF16))
    a_hi, a_lo = _split(a)
    b_hi, b_lo = _split(b)
    return dg(a_hi, b_hi) + (dg(a_hi, b_lo) + dg(a_lo, b_hi))


def _mm(a, b, exact=False):
    return _dot(a, b, (((1,), (0,)), ((), ())), exact)


def _mm_nt(a, b, exact=False):
    return _dot(a, b, (((1,), (1,)), ((), ())), exact)


def _mm_tn(a, b, exact=False):
    return _dot(a, b, (((0,), (0,)), ((), ())), exact)


def _mm_mask(mask01, x):
    m = mask01.astype(BF16)
    hi = x.astype(BF16)
    mid, lo = _split(x - hi.astype(F32))
    dg = lambda q: jnp.dot(m, q, preferred_element_type=F32)
    return dg(hi) + (dg(mid) + dg(lo))


def _sigmoid(x):
    return 1.0 / (1.0 + jnp.exp(-x))


def _norm_mod(x, gain, shift, scale):
    y = x * lax.rsqrt(jnp.mean(x * x, axis=-1, keepdims=True) + NORM_EPS) * gain
    return y * (1.0 + scale) + shift


def _ada_kernel(c_ref, w_ref, b_ref, o_ref):
    cv = c_ref[...]
    o_ref[...] = _mm(cv * _sigmoid(cv), w_ref[...], exact=True) + b_ref[...]


def _ada(cvec, w, b, tn=512):
    k, n = w.shape
    return pl.pallas_call(
        _ada_kernel,
        out_shape=jax.ShapeDtypeStruct((cvec.shape[0], n), F32),
        grid=(n // tn,),
        in_specs=[pl.BlockSpec((cvec.shape[0], k), lambda j: (0, 0)),
                  pl.BlockSpec((k, tn), lambda j: (0, j)),
                  pl.BlockSpec((1, tn), lambda j: (0, j))],
        out_specs=pl.BlockSpec((cvec.shape[0], tn), lambda j: (0, j)),
        compiler_params=_cp(("parallel",)),
        name="ada",
    )(cvec, w, b)


def _inproj_kernel(x_ref, g_ref, sh_ref, sc_ref, w_ref, o_ref, h_scr):
    @pl.when(pl.program_id(1) == 0)
    def _():
        h_scr[...] = _norm_mod(x_ref[...], g_ref[...], sh_ref[...], sc_ref[...]).astype(BF16)

    o_ref[...] = jnp.dot(h_scr[...], w_ref[...], preferred_element_type=F32)


def _inproj(x2d, gain, shift, scale, w, rows_per_mod, tm=1024, tn=512):
    m, k = x2d.shape
    n = w.shape[1]
    tm = min(tm, rows_per_mod)
    assert rows_per_mod % tm == 0 and m % tm == 0 and n % tn == 0
    bpm = rows_per_mod // tm
    mod_map = lambda i, j: (i // bpm, 0, 0)
    return pl.pallas_call(
        _inproj_kernel,
        out_shape=jax.ShapeDtypeStruct((m, n), F32),
        grid=(m // tm, n // tn),
        in_specs=[pl.BlockSpec((tm, k), lambda i, j: (i, 0)),
                  pl.BlockSpec((1, k), lambda i, j: (0, 0)),
                  pl.BlockSpec((None, 1, k), mod_map),
                  pl.BlockSpec((None, 1, k), mod_map),
                  pl.BlockSpec((k, tn), lambda i, j: (0, j))],
        out_specs=pl.BlockSpec((tm, tn), lambda i, j: (i, j)),
        scratch_shapes=[pltpu.VMEM((tm, k), BF16)],
        compiler_params=_cp(("parallel", "arbitrary")),
        name="inproj",
    )(x2d, gain, shift, scale, w)


def _tri_masks(reverse):
    row = lax.broadcasted_iota(jnp.int32, (CHUNK, CHUNK), 0)
    col = lax.broadcasted_iota(jnp.int32, (CHUNK, CHUNK), 1)
    if reverse:
        strict, incl = col > row, col >= row
    else:
        strict, incl = col < row, col <= row
    same = (row // SUB) == (col // SUB)
    return strict, incl, same, row == col


def _rw_heads(ar_t, bk_t, a_0, r_0, bk_end, v_b, decay_tot, reverse, out_refs):
    strict, incl, same, eye = _tri_masks(reverse)
    row2 = lax.broadcasted_iota(jnp.int32, (CHUNK, 2 * CHUNK), 0)
    col2 = lax.broadcasted_iota(jnp.int32, (CHUNK, 2 * CHUNK), 1) % CHUNK
    incl2 = (col2 >= row2) if reverse else (col2 <= row2)
    ident = jnp.where(eye, 1.0, 0.0)
    heads = range(RW_HEADS)
    hs = lambda h: slice(h * RW_HEAD, (h + 1) * RW_HEAD)
    bf = lambda xs: [x.astype(BF16) for x in xs]
    mm = lambda xs, ys: [jnp.dot(x, y, preferred_element_type=F32) for x, y in zip(xs, ys)]
    nt_dims = (((1,), (1,)), ((), ()))
    tn_dims = (((0,), (0,)), ((), ()))

    s4 = [lax.dot_general(ar_t[:, hs(h)], bk_t[:, hs(h)], nt_dims, preferred_element_type=F32) for h in heads]
    m_ab = [jnp.where(strict, s[:CHUNK, :CHUNK], 0.0) for s in s4]
    m_ak = bf([jnp.where(strict, s[:CHUNK, CHUNK:], 0.0) for s in s4])
    m_r = bf([jnp.where(incl2, s[CHUNK:], 0.0) for s in s4])

    m_bd = [jnp.where(same, m, 0.0) for m in m_ab]
    m_off = bf([jnp.where(same, 0.0, m) for m in m_ab])
    m1 = bf(m_bd)
    m2 = bf(mm(m1, m1))
    m4 = bf(mm(m2, m2))
    m8 = bf(mm(m4, m4))
    p = [ident + m for m in m_bd]
    for mk in (m2, m4, m8):
        p = [x + y for x, y in zip(p, mm(bf(p), mk))]
    p_b = bf(p)
    n1 = mm(p_b, m_off)
    n1_b = bf(n1)
    n2 = mm(n1_b, n1_b)
    n3 = mm(n1_b, bf(n2))
    t_inv = bf(mm(bf([ident + a + b + c for a, b, c in zip(n1, n2, n3)]), p_b))

    x = bf(mm(m_ak, [v_b[:, hs(h)] for h in heads]))
    wu = bf(mm(t_inv, [jnp.concatenate([a_0[:, hs(h)], xh], axis=1) for h, xh in zip(heads, x)]))
    zero = jnp.zeros((CHUNK, RW_HEAD), BF16)
    rhs = [jnp.concatenate([w, jnp.concatenate([zero, v_b[:, hs(h)]], axis=1)], axis=0) for h, w in zip(heads, wu)]
    qo = mm(m_r, rhs)
    gd = [lax.dot_general(bk_end[:, hs(h)], r, tn_dims, preferred_element_type=F32) for h, r in zip(heads, rhs)]
    q_ref, o_ref, g_ref, d_ref = out_refs
    for h in heads:
        q_ref[:, hs(h)] = r_0[:, hs(h)] + qo[h][:, :RW_HEAD]
        o_ref[:, hs(h)] = qo[h][:, RW_HEAD:]
        g_ref[:, hs(h)] = gd[h][:, :RW_HEAD] + jnp.where(eye, decay_tot[:, hs(h)], 0.0)
        d_ref[:, hs(h)] = gd[h][:, RW_HEAD:]


def _rwkv_pre_kernel(zp_ref, zc_ref, zn_ref, conv_ref, w0_ref, wl_ref, a0_ref, al_ref, gl_ref, kk_ref, ka_ref,
                     rk_ref, qf_ref, of_ref, gf_ref, df_ref, qb_ref, ob_ref, gb_ref, db_ref, g_ref, bonus_ref,
                     conv_s, *, is_ctx, nchunks):
    c = pl.program_id(1)
    row = lax.broadcasted_iota(jnp.int32, (CHUNK, 1), 0)
    has_p = (c > 0).astype(F32)
    has_n = (c < nchunks - 1).astype(F32)

    def left(z, fill):
        return jnp.where(row == 0, fill, pltpu.roll(z, 1, 0))

    def right(z, fill):
        return jnp.where(row == CHUNK - 1, fill, pltpu.roll(z, CHUNK - 1, 0))

    for cb in range(RW_PAD // CONV_COLS):
        cs = slice(cb * CONV_COLS, (cb + 1) * CONV_COLS)
        w = conv_ref[:, cs]
        zc = zc_ref[:, cs]
        if is_ctx:
            conv = (left(zc, zp_ref[CHUNK - 1:CHUNK, cs] * has_p) * w[3:4] + zc * w[4:5]
                    + right(zc, zn_ref[0:1, cs] * has_n) * w[5:6])
        else:
            conv = None
            for dy, z in enumerate((zp_ref[:, cs] * has_p, zc, zn_ref[:, cs] * has_n)):
                t = (left(z, 0.0) * w[3 * dy:3 * dy + 1] + z * w[3 * dy + 1:3 * dy + 2]
                     + right(z, 0.0) * w[3 * dy + 2:3 * dy + 3])
                conv = t if conv is None else conv + t
        conv_s[:, cs] = conv

    r = conv_s[:, 0:RW_WIDTH]
    k = conv_s[:, RW_WIDTH:2 * RW_WIDTH]
    v = conv_s[:, 2 * RW_WIDTH:3 * RW_WIDTH]
    lo = conv_s[:, 3 * RW_WIDTH:RW_PAD]
    g_ref[...] = _mm(_sigmoid(lo[:, 4 * RW_LORA:]), gl_ref[...], exact=True)

    first = lax.broadcasted_iota(jnp.int32, (1, 2 * RW_HEAD), 1) < RW_HEAD

    def head_sum(t):
        tiles = []
        for i in range(RW_WIDTH // (2 * RW_HEAD)):
            ti = t[:, i * 2 * RW_HEAD:(i + 1) * 2 * RW_HEAD]
            s0 = jnp.sum(jnp.where(first, ti, 0.0), axis=-1, keepdims=True)
            s1 = jnp.sum(jnp.where(first, 0.0, ti), axis=-1, keepdims=True)
            tiles.append(jnp.where(first, s0, s1))
        return jnp.concatenate(tiles, axis=1)

    kkr = k * kk_ref[...]
    kk = kkr / jnp.maximum(jnp.sqrt(head_sum(kkr * kkr)), 1e-12)
    v_b = v.astype(BF16)
    outs = ((qf_ref, of_ref, gf_ref, df_ref), (qb_ref, ob_ref, gb_ref, db_ref))
    trow = lax.broadcasted_iota(jnp.int32, (CHUNK, CHUNK), 0)
    tcol = lax.broadcasted_iota(jnp.int32, (CHUNK, CHUNK), 1)
    bonus = None
    for d in range(2):
        wa = lo[:, d * RW_LORA:(d + 1) * RW_LORA]
        aa = lo[:, (2 + d) * RW_LORA:(3 + d) * RW_LORA]
        w_pre = w0_ref[d:d + 1, :] + _mm(jnp.tanh(wa), wl_ref[d], exact=True)
        logw = -EXP_M05 * _sigmoid(w_pre)
        a = _sigmoid(a0_ref[d:d + 1, :] + _mm(aa, al_ref[d], exact=True))
        tri = jnp.where((tcol >= trow) if d == 1 else (tcol <= trow), 1.0, 0.0)
        cum = _mm_mask(tri, logw)
        prev = cum - logw
        mid = CHUNK // 2 - 1 if d == 1 else CHUNK // 2
        end = 0 if d == 1 else CHUNK - 1
        rho = cum[mid:mid + 1]
        tot = cum[end:end + 1]
        kd = k * (1.0 + (a - 1.0) * ka_ref[...])
        kka = kk * a
        hb = head_sum(r * kd * rk_ref[...])
        bonus = hb if bonus is None else bonus + hb
        e_out = jnp.exp(rho - cum)
        e_end = jnp.exp(tot - cum)
        ar_t = jnp.concatenate([-kk * jnp.exp(prev - rho), r * jnp.exp(cum - rho)], axis=0).astype(BF16)
        bk_t = jnp.concatenate([kka * e_out, kd * e_out], axis=0).astype(BF16)
        bk_end = jnp.concatenate([kka * e_end, kd * e_end], axis=0).astype(BF16)
        a_0 = (-kk * jnp.exp(prev)).astype(BF16)
        r_0 = r * jnp.exp(cum)
        _rw_heads(ar_t, bk_t, a_0, r_0, bk_end, v_b, jnp.exp(tot), d == 1, outs[d])
    bonus_ref[...] = bonus * v


def _rwkv_pre(z_rw, conv_w, w0, wl, a0, al, gl, k_k, k_a, r_k, batch, nchunks, is_ctx):
    rows = z_rw.shape[0]
    last = nchunks - 1
    zspec = lambda f: pl.BlockSpec((CHUNK, RW_PAD), lambda b, c: (b * nchunks + f(c), 0))
    full = lambda shape: pl.BlockSpec(shape, lambda b, c: (0,) * len(shape))
    ospec = pl.BlockSpec((CHUNK, RW_WIDTH), lambda b, c: (b * nchunks + c, 0))
    out = jax.ShapeDtypeStruct((rows, RW_WIDTH), F32)
    return pl.pallas_call(
        functools.partial(_rwkv_pre_kernel, is_ctx=is_ctx, nchunks=nchunks),
        out_shape=[out] * 10,
        grid=(batch, nchunks),
        in_specs=[zspec(lambda c: jnp.maximum(c - 1, 0)), zspec(lambda c: c), zspec(lambda c: jnp.minimum(c + 1, last)),
                  full((9, RW_PAD)), full((2, RW_WIDTH)), full((2, RW_LORA, RW_WIDTH)), full((2, RW_WIDTH)),
                  full((2, RW_LORA, RW_WIDTH)), full((RW_PAD - 3 * RW_WIDTH - 4 * RW_LORA, RW_WIDTH)), full((1, RW_WIDTH)),
                  full((1, RW_WIDTH)), full((1, RW_WIDTH))],
        out_specs=[ospec] * 10,
        scratch_shapes=[pltpu.VMEM((CHUNK, RW_PAD), F32)],
        compiler_params=_cp(("parallel", "parallel")),
        name="rwkv_pre",
    )(z_rw, z_rw, z_rw, conv_w, w0, wl, a0, al, gl, k_k, k_a, r_k)


def _rwkv_seq_kernel(x0f_ref, x0b_ref, qf_ref, of_ref, gf_ref, df_ref, qb_ref, ob_ref, gb_ref, db_ref,
                     outf_ref, outb_ref, xff_ref, xfb_ref, x_s, *, nsteps):
    c = pl.program_id(1)

    @pl.when(c == 0)
    def _():
        x_s[0] = x0f_ref[...]
        x_s[1] = x0b_ref[...]

    dirs = ((qf_ref, of_ref, gf_ref, df_ref, outf_ref), (qb_ref, ob_ref, gb_ref, db_ref, outb_ref))
    hs = lambda h: slice(h * RW_HEAD, (h + 1) * RW_HEAD)
    for j in range(SEQ_CHUNKS):
        for d, (q_ref, o_ref, g_ref, d_ref, out_ref) in enumerate(dirs):
            jj = SEQ_CHUNKS - 1 - j if d == 1 else j
            rows = slice(jj * CHUNK, (jj + 1) * CHUNK)
            res = [_mm(jnp.concatenate([q_ref[rows, hs(h)], g_ref[rows, hs(h)]], axis=0), x_s[d, h], exact=True)
                   for h in range(RW_HEADS)]
            for h in range(RW_HEADS):
                out_ref[rows, hs(h)] = res[h][:CHUNK] + o_ref[rows, hs(h)]
                x_s[d, h] = res[h][CHUNK:] + d_ref[rows, hs(h)]

    @pl.when(c == nsteps - 1)
    def _():
        xff_ref[...] = x_s[0]
        xfb_ref[...] = x_s[1]


def _rwkv_seq(x0f, x0b, pre, batch, nchunks):
    nsteps = nchunks // SEQ_CHUNKS
    blk = (SEQ_CHUNKS * CHUNK, RW_WIDTH)
    fspec = pl.BlockSpec(blk, lambda b, c: (b * nsteps + c, 0))
    bspec = pl.BlockSpec(blk, lambda b, c: (b * nsteps + nsteps - 1 - c, 0))
    sspec = pl.BlockSpec((None, RW_HEADS, RW_HEAD, RW_HEAD), lambda b, c: (b, 0, 0, 0))
    seq = jax.ShapeDtypeStruct(pre[0].shape, F32)
    state = jax.ShapeDtypeStruct(x0f.shape, F32)
    return pl.pallas_call(
        functools.partial(_rwkv_seq_kernel, nsteps=nsteps),
        out_shape=[seq, seq, state, state],
        grid=(batch, nsteps),
        in_specs=[sspec, sspec] + [fspec] * 4 + [bspec] * 4,
        out_specs=[fspec, bspec, sspec, sspec],
        scratch_shapes=[pltpu.VMEM((2, RW_HEADS, RW_HEAD, RW_HEAD), F32)],
        compiler_params=_cp(("parallel", "arbitrary")),
        name="rwkv_seq",
    )(x0f, x0b, *pre)


def _hgrn_chunk(st_s, d, q_ref, f_ref, i_ref, lb, out_ref, rows):
    reverse = d == 1
    row = lax.broadcasted_iota(jnp.int32, (CHUNK, CHUNK), 0)
    col = lax.broadcasted_iota(jnp.int32, (CHUNK, CHUNK), 1)
    incl = (col >= row) if reverse else (col <= row)
    mid = CHUNK // 2 - 1 if reverse else CHUNK // 2
    end = 0 if reverse else CHUNK - 1

    forget = lb + (1.0 - lb) * _sigmoid(f_ref[rows, :])
    kf = 1.0 - forget
    qraw = q_ref[rows, :]
    q = qraw * _sigmoid(qraw)
    v = i_ref[rows, :]
    cum = _mm_mask(jnp.where(incl, 1.0, 0.0), jnp.log(forget))
    ref = cum[mid:mid + 1]
    tot = cum[end:end + 1]
    qs = q * jnp.exp(cum - ref)
    ks = kf * jnp.exp(ref - cum)
    qe = q * jnp.exp(cum)
    ke = kf * jnp.exp(tot - cum)
    dec = jnp.exp(tot)
    for h in range(HG_HEADS):
        sl = slice(h * HG_HEAD, (h + 1) * HG_HEAD)
        st = st_s[d, h]
        scores = jnp.where(incl, _mm_nt(qs[:, sl], ks[:, sl]), 0.0)
        out_ref[rows, sl] = _mm(scores, v[:, sl]) + _mm_nt(qe[:, sl], st)
        st_s[d, h] = st * dec[:, sl] + _mm_tn(v[:, sl], ke[:, sl])


def _hgrn_kernel(s0f_ref, s0b_ref, qf_ref, ff_ref, if_ref, qb_ref, fb_ref, ib_ref, lb_ref, outf_ref, outb_ref,
                 sff_ref, sfb_ref, st_s, *, nsteps):
    c = pl.program_id(1)

    @pl.when(c == 0)
    def _():
        st_s[0] = s0f_ref[...]
        st_s[1] = s0b_ref[...]

    for j in range(SEQ_CHUNKS):
        rows = slice(j * CHUNK, (j + 1) * CHUNK)
        _hgrn_chunk(st_s, 0, qf_ref, ff_ref, if_ref, lb_ref[0:1, :], outf_ref, rows)
        jj = SEQ_CHUNKS - 1 - j
        rows = slice(jj * CHUNK, (jj + 1) * CHUNK)
        _hgrn_chunk(st_s, 1, qb_ref, fb_ref, ib_ref, lb_ref[1:2, :], outb_ref, rows)

    @pl.when(c == nsteps - 1)
    def _():
        sff_ref[...] = st_s[0]
        sfb_ref[...] = st_s[1]


def _hgrn(s0f, s0b, z_hg, lb, batch, nchunks):
    nsteps = nchunks // SEQ_CHUNKS
    blk = (SEQ_CHUNKS * CHUNK, HG_WIDTH)
    fspec = lambda col: pl.BlockSpec(blk, lambda b, c: (b * nsteps + c, col))
    bspec = lambda col: pl.BlockSpec(blk, lambda b, c: (b * nsteps + nsteps - 1 - c, col))
    sspec = pl.BlockSpec((None, HG_HEADS, HG_HEAD, HG_HEAD), lambda b, c: (b, 0, 0, 0))
    seq = jax.ShapeDtypeStruct((z_hg.shape[0], HG_WIDTH), F32)
    state = jax.ShapeDtypeStruct(s0f.shape, F32)
    return pl.pallas_call(
        functools.partial(_hgrn_kernel, nsteps=nsteps),
        out_shape=[seq, seq, state, state],
        grid=(batch, nsteps),
        in_specs=[sspec, sspec, fspec(0), fspec(1), fspec(3), bspec(0), bspec(2), bspec(3),
                  pl.BlockSpec((2, HG_WIDTH), lambda b, c: (0, 0))],
        out_specs=[fspec(0), bspec(0), sspec, sspec],
        scratch_shapes=[pltpu.VMEM((2, HG_HEADS, HG_HEAD, HG_HEAD), F32)],
        compiler_params=_cp(("parallel", "arbitrary")),
        name="hgrn",
    )(s0f, s0b, z_hg, z_hg, z_hg, z_hg, z_hg, z_hg, lb)


def _readout_kernel(x_ref, of_ref, ob_ref, bonus_ref, g_ref, hf_ref, hb_ref, og_ref, grw_ref, ghg_ref, gmx_ref,
                    lnw_ref, lnb_ref, hgn_ref, wrw_ref, whg_ref, wout_ref, o_ref, y_s):
    o_rw = of_ref[...] + ob_ref[...]
    for h in range(RW_HEADS):
        hs = slice(h * RW_HEAD, (h + 1) * RW_HEAD)
        oh = o_rw[:, hs]
        mu = jnp.mean(oh, axis=-1, keepdims=True)
        dv = oh - mu
        var = jnp.mean(dv * dv, axis=-1, keepdims=True)
        y_s[:, hs] = dv * lax.rsqrt(var + RW_GN_EPS)
    y_rw = (y_s[...] * lnw_ref[...] + lnb_ref[...] + bonus_ref[...]) * g_ref[...]
    branch_rw = _mm(y_rw, wrw_ref[...])

    o_hg = hf_ref[...] + hb_ref[...]
    for h in range(HG_HEADS):
        sl = slice(h * HG_HEAD, (h + 1) * HG_HEAD)
        oh = o_hg[:, sl]
        y_s[:, sl] = oh * lax.rsqrt(jnp.mean(oh * oh, axis=-1, keepdims=True) + NORM_EPS)
    og = og_ref[...]
    y_hg = y_s[...] * hgn_ref[...] * (og * _sigmoid(og))
    branch_hg = _mm(y_hg, whg_ref[...])

    merged = _sigmoid(grw_ref[...]) * branch_rw + _sigmoid(ghg_ref[...]) * branch_hg
    o_ref[...] = x_ref[...] + gmx_ref[...] * _mm(merged, wout_ref[...])


def _readout(x2d, o_f, o_b, bonus, g, h_f, h_b, z_hg, z_g, g_mx, ln_w, ln_b, hg_norm, w_rw, w_hg, w_out,
             rows_per_mod, tm=256):
    m, dm = x2d.shape
    bpm = rows_per_mod // tm
    tok = lambda width, blk=0: pl.BlockSpec((tm, width), lambda i: (i, blk))
    const = lambda shape: pl.BlockSpec(shape, lambda i: (0,) * len(shape), pipeline_mode=pl.Buffered(1))
    return pl.pallas_call(
        _readout_kernel,
        out_shape=jax.ShapeDtypeStruct((m, dm), F32),
        grid=(m // tm,),
        in_specs=[tok(dm), tok(RW_WIDTH), tok(RW_WIDTH), tok(RW_WIDTH), tok(RW_WIDTH), tok(HG_WIDTH), tok(HG_WIDTH),
                  tok(HG_WIDTH, 4), tok(dm, 0), tok(dm, 1),
                  pl.BlockSpec((None, 1, dm), lambda i: (i // bpm, 0, 0)),
                  const((1, RW_WIDTH)), const((1, RW_WIDTH)), const((1, HG_WIDTH)),
                  const((RW_WIDTH, dm)), const((HG_WIDTH, dm)), const((dm, dm))],
        out_specs=tok(dm),
        scratch_shapes=[pltpu.VMEM((tm, RW_WIDTH), F32)],
        compiler_params=_cp(("parallel",)),
        name="readout",
    )(x2d, o_f, o_b, bonus, g, h_f, h_b, z_hg, z_g, z_g, g_mx, ln_w, ln_b, hg_norm, w_rw, w_hg, w_out)


def _cand_groups():
    return [(i, PEER_TOPK // (i + 1)) for i in range(PEER_TOPK)]


def _topk_ranks(s, vals_ref):
    io = lax.broadcasted_iota(jnp.int32, s.shape, 0)

    def body(r, carry):
        work, rank = carry
        m = jnp.max(work, axis=0, keepdims=True)
        idx = jnp.min(jnp.where(work == m, io, s.shape[0]), axis=0, keepdims=True)
        sel = io == idx
        vals_ref[pl.ds(r, 1), :] = m
        return jnp.where(sel, -jnp.inf, work), jnp.where(sel, lax.convert_element_type(r, F32), rank)

    _, rank = lax.fori_loop(0, PEER_TOPK, body, (s, jnp.full(s.shape, float(PEER_TOPK), F32)))
    return rank


def _peer_sel_kernel(x_ref, gain_ref, sh_ref, sc_ref, wq_ref, k1_ref, k2_ref, hx_ref, cnt_ref, rk_ref, e1_ref,
                     e2_ref, v1_s, v2_s):
    hb = _norm_mod(x_ref[...], gain_ref[...], sh_ref[...], sc_ref[...]).astype(BF16)
    hx_ref[...] = hb
    q = jnp.dot(hb, wq_ref[...], preferred_element_type=F32)
    tm = q.shape[0]
    groups = _cand_groups()
    ncand = sum(n for _, n in groups)
    npad = -ncand % 8
    for hd in range(PEER_HEADS):
        q1 = q[:, (2 * hd) * PEER_NKEYS:(2 * hd + 1) * PEER_NKEYS]
        q2 = q[:, (2 * hd + 1) * PEER_NKEYS:(2 * hd + 2) * PEER_NKEYS]
        s1 = _mm_nt(k1_ref[...], q1, exact=True)
        s2 = _mm_nt(k2_ref[...], q2, exact=True)
        rank1 = _topk_ranks(s1, v1_s)
        rank2 = _topk_ranks(s2, v2_s)
        v1 = v1_s[...]
        v2 = v2_s[...]
        rows = [v1[i:i + 1] + v2[0:n] for i, n in groups]
        if npad:
            rows.append(jnp.full((npad, tm), -jnp.inf, F32))
        cand = jnp.concatenate(rows, axis=0)
        io = lax.broadcasted_iota(jnp.int32, cand.shape, 0)
        best0 = v1[0:1] + v2[0:1]

        def body(r, carry):
            work, z = carry
            m = jnp.max(work, axis=0, keepdims=True)
            idx = jnp.min(jnp.where(work == m, io, cand.shape[0]), axis=0, keepdims=True)
            return jnp.where(io == idx, -jnp.inf, work), z + jnp.exp(m - best0)

        work, z = lax.fori_loop(0, PEER_TOPK, body, (cand, jnp.zeros((1, tm), F32)))
        taken = jnp.where((work == -jnp.inf) & (io < ncand), 1.0, 0.0)
        cnt1 = jnp.zeros_like(rank1)
        start = 0
        for i, n in groups:
            c_i = jnp.sum(taken[start:start + n], axis=0, keepdims=True)
            cnt1 = cnt1 + jnp.where(rank1 == float(i), c_i, 0.0)
            start += n
        cnt_ref[hd] = cnt1
        rk_ref[hd] = rank2.astype(BF16)
        e1_ref[hd] = jnp.exp(s1 - v1[0:1])
        e2_ref[hd] = (jnp.exp(s2 - v2[0:1]) / z).astype(BF16)


def _peer_sel(x2d, gain, shift, scale, wq, k1, k2, rows_per_mod, tm=256):
    m, dm = x2d.shape
    bpm = rows_per_mod // tm
    mod_map = lambda i: (i // bpm, 0, 0)
    const = lambda shape: pl.BlockSpec(shape, lambda i: (0,) * len(shape))
    sel = jax.ShapeDtypeStruct((PEER_HEADS, PEER_NKEYS, m), F32)
    sel_bf = jax.ShapeDtypeStruct((PEER_HEADS, PEER_NKEYS, m), BF16)
    sel_spec = pl.BlockSpec((PEER_HEADS, PEER_NKEYS, tm), lambda i: (0, 0, i))
    return pl.pallas_call(
        _peer_sel_kernel,
        out_shape=[jax.ShapeDtypeStruct((m, dm), BF16), sel, sel_bf, sel, sel_bf],
        grid=(m // tm,),
        in_specs=[pl.BlockSpec((tm, dm), lambda i: (i, 0)), const((1, dm)),
                  pl.BlockSpec((None, 1, dm), mod_map), pl.BlockSpec((None, 1, dm), mod_map),
                  const(wq.shape), const(k1.shape), const(k2.shape)],
        out_specs=[pl.BlockSpec((tm, dm), lambda i: (i, 0)), sel_spec, sel_spec, sel_spec, sel_spec],
        scratch_shapes=[pltpu.VMEM((PEER_TOPK, tm), F32), pltpu.VMEM((PEER_TOPK, tm), F32)],
        compiler_params=_cp(("parallel",)),
        name="peer_sel",
    )(x2d, gain, shift, scale, wq, k1, k2)


def _gelu(x):
    return 0.5 * x * (1.0 + lax.erf(x * float(np.sqrt(0.5))))


def _peer_dense_kernel(hx_ref, u_ref, v_ref, cnt_ref, rk_ref, e1_ref, e2_ref, x_ref, gfx_ref, gain_ref, o_ref,
                       wa_s, wb_s, ga_s, gb_s, *, te, nblk):
    e = pl.program_id(1)
    half = te // 2
    nsub = half // PEER_NKEYS
    tm = hx_ref.shape[0]

    def gates(half_blk, dst):
        packed = (PEER_NKEYS // BF16_ROWS, BF16_ROWS, tm)
        for ii in range(nsub):
            key1 = half_blk * nsub + ii
            gate = None
            for hd in range(PEER_HEADS):
                cnt = jnp.broadcast_to(cnt_ref[hd, pl.ds(key1, 1), :], packed[1:]).astype(BF16)
                e1 = jnp.broadcast_to(e1_ref[hd, pl.ds(key1, 1), :], packed[1:]).astype(BF16)
                t = jnp.where(rk_ref[hd].reshape(packed) < cnt[None], e2_ref[hd].reshape(packed) * e1[None],
                              jnp.zeros((), BF16))
                gate = t if gate is None else gate + t
            dst[ii * PEER_NKEYS:(ii + 1) * PEER_NKEYS, :] = gate.reshape(PEER_NKEYS, tm)

    @pl.when(e == 0)
    def _():
        o_ref[...] = jnp.zeros_like(o_ref)
        gates(0, ga_s)

    nt_dims = (((1,), (1,)), ((), ()))
    hx = hx_ref[...]
    gates(2 * e + 1, gb_s)
    act_a = _gelu(lax.dot_general(u_ref[:half, :], hx, nt_dims, preferred_element_type=F32))
    wa_s[...] = (ga_s[...] * act_a.astype(BF16)).T
    act_b = _gelu(lax.dot_general(u_ref[half:, :], hx, nt_dims, preferred_element_type=F32))
    o_ref[...] += jnp.dot(wa_s[...], v_ref[:half, :], preferred_element_type=F32)
    wb_s[...] = (gb_s[...] * act_b.astype(BF16)).T
    gates(2 * ((e + 1) % nblk), ga_s)
    o_ref[...] += jnp.dot(wb_s[...], v_ref[half:, :], preferred_element_type=F32)

    @pl.when(e == nblk - 1)
    def _():
        x = x_ref[...] + gfx_ref[...] * o_ref[...]
        o_ref[...] = x * lax.rsqrt(jnp.mean(x * x, axis=-1, keepdims=True) + NORM_EPS) * gain_ref[...]


def _peer_dense(hx, u, v, cnt, rk, e1, e2, x2d, g_fx, gain, rows_per_mod, tm=512, te=1024):
    m, dm = hx.shape
    ne = u.shape[0]
    nblk = ne // te
    bpm = rows_per_mod // tm
    sel_spec = pl.BlockSpec((PEER_HEADS, PEER_NKEYS, tm), lambda i, e: (0, 0, i))
    tok = pl.BlockSpec((tm, dm), lambda i, e: (i, 0))
    tab = pl.BlockSpec((te, dm), lambda i, e: (e, 0))
    return pl.pallas_call(
        functools.partial(_peer_dense_kernel, te=te, nblk=nblk),
        out_shape=jax.ShapeDtypeStruct((m, dm), F32),
        grid=(m // tm, nblk),
        in_specs=[tok, tab, tab, sel_spec, sel_spec, sel_spec, sel_spec, tok,
                  pl.BlockSpec((None, 1, dm), lambda i, e: (i // bpm, 0, 0)),
                  pl.BlockSpec((1, dm), lambda i, e: (0, 0))],
        out_specs=tok,
        scratch_shapes=[pltpu.VMEM((tm, te // 2), BF16)] * 2 + [pltpu.VMEM((te // 2, tm), BF16)] * 2,
        compiler_params=_cp(("parallel", "arbitrary")),
        name="peer_dense",
    )(hx, u, v, cnt, rk, e1, e2, x2d, g_fx, gain)


def kernel(x, c, ctx, c_ctx, w_ada, b_ada, norm_mix, norm_ffn, norm_final, w_in, rw_conv, rw_w0, rw_w_lora_b, rw_a0, rw_a_lora_b, rw_g_lora_b, rw_k_k, rw_k_a, rw_r_k, rw_ln_w, rw_ln_b, hg_lb, hg_norm, w_up_rw, w_up_hg, w_out, peer_wq, peer_k1, peer_k2, peer_u, peer_v):
    batch, seq, dm = x.shape
    ctx_len = ctx.shape[1]
    depth = w_in.shape[0]
    assert depth == 1 and dm == D_MODEL
    assert seq % (CHUNK * SEQ_CHUNKS) == 0 and ctx_len % (CHUNK * SEQ_CHUNKS) == 0
    n_x = seq // CHUNK
    n_c = ctx_len // CHUNK
    l = 0

    rw_end, hg_end = RW_COLS, RW_COLS + 5 * HG_WIDTH
    w_rw_in = jnp.pad(w_in[l, :, :rw_end], ((0, 0), (0, RW_PAD - RW_COLS))).astype(BF16)
    w_hg_in = w_in[l, :, rw_end:hg_end].astype(BF16)
    w_g_in = w_in[l, :, hg_end:].astype(BF16)
    conv_w = jnp.pad(rw_conv[l].reshape(9, RW_COLS), ((0, 0), (0, RW_PAD - RW_COLS)))
    lower_bounds = jnp.cumsum(jax.nn.softmax(hg_lb.astype(F32), axis=0), axis=0)[l]
    row = lambda t: t.reshape(1, -1)
    hg_norm_row = jnp.tile(hg_norm[l], HG_HEADS).reshape(1, HG_WIDTH)
    u_bf = peer_u[l].astype(BF16)
    v_bf = peer_v[l].astype(BF16)

    cvec = jnp.concatenate([c, c_ctx[None, :], jnp.zeros((8 - batch - 1, dm), F32)], axis=0)
    mod = _ada(cvec, w_ada[l], row(b_ada[l]))
    mods = mod.reshape(8, N_MOD, 1, dm)
    sh_mx, sc_mx, g_mx, sh_fx, sc_fx, g_fx = (mods[:batch, i] for i in range(N_MOD))
    sh_mc, sc_mc = mods[batch:batch + 1, 0], mods[batch:batch + 1, 1]

    x2d = x.reshape(batch * seq, dm)
    c2d = ctx.reshape(batch * ctx_len, dm)
    gain_mix = row(norm_mix[l])

    zc_rw = _inproj(c2d, gain_mix, sh_mc, sc_mc, w_rw_in, batch * ctx_len, tm=batch * ctx_len)
    zc_hg = _inproj(c2d, gain_mix, sh_mc, sc_mc, w_hg_in, batch * ctx_len, tm=batch * ctx_len)
    g_lora = jnp.pad(rw_g_lora_b[l], ((0, RW_PAD - RW_COLS), (0, 0)))
    rw_args = (conv_w, rw_w0[l], rw_w_lora_b[l], rw_a0[l], rw_a_lora_b[l], g_lora, row(rw_k_k[l]),
               row(rw_k_a[l]), row(rw_r_k[l]))
    pre_c = _rwkv_pre(zc_rw, *rw_args, batch=batch, nchunks=n_c, is_ctx=True)
    zero_rw = jnp.zeros((batch, RW_HEADS, RW_HEAD, RW_HEAD), F32)
    zero_hg = jnp.zeros((batch, HG_HEADS, HG_HEAD, HG_HEAD), F32)
    _, _, st_rw_f, st_rw_b = _rwkv_seq(zero_rw, zero_rw, pre_c[0:8], batch, n_c)
    _, _, st_hg_f, st_hg_b = _hgrn(zero_hg, zero_hg, zc_hg, lower_bounds, batch, n_c)

    z_rw = _inproj(x2d, gain_mix, sh_mx, sc_mx, w_rw_in, seq)
    z_hg = _inproj(x2d, gain_mix, sh_mx, sc_mx, w_hg_in, seq)
    z_g = _inproj(x2d, gain_mix, sh_mx, sc_mx, w_g_in, seq)
    pre = _rwkv_pre(z_rw, *rw_args, batch=batch, nchunks=n_x, is_ctx=False)
    o_f, o_b, _, _ = _rwkv_seq(st_rw_f, st_rw_b, pre[0:8], batch, n_x)
    h_f, h_b, _, _ = _hgrn(st_hg_f, st_hg_b, z_hg, lower_bounds, batch, n_x)
    x1 = _readout(x2d, o_f, o_b, pre[9], pre[8], h_f, h_b, z_hg, z_g, g_mx, row(rw_ln_w[l]), row(rw_ln_b[l]),
                  hg_norm_row, w_up_rw[l].astype(BF16), w_up_hg[l].astype(BF16), w_out[l].astype(BF16), seq)

    hx, cnt, rk, e1, e2 = _peer_sel(x1, row(norm_ffn[l]), sh_fx, sc_fx, peer_wq[l].astype(BF16), peer_k1[l],
                                    peer_k2[l], seq)
    out = _peer_dense(hx, u_bf, v_bf, cnt, rk, e1, e2, x1, g_fx, row(norm_final), seq)
    return out.reshape(batch, seq, dm)
```

```python
import functools

import jax
import jax.numpy as jnp
import numpy as np
from jax import lax
from jax.experimental import pallas as pl
from jax.experimental.pallas import tpu as pltpu

F32 = jnp.float32
BF16 = jnp.bfloat16

D_MODEL = 2048
N_MOD = 6
NORM_EPS = 1e-6
CHUNK = 64
SUB = 16
SEQ_CHUNKS = 4
RW_WIDTH = 1024
RW_HEAD = 64
RW_HEADS = 16
RW_LORA = 64
RW_GATE_RANK = 160
RW_COLS = 3488
RW_PAD = 3584
CONV_COLS = 512
NORM_ROWS = 128
RW_GN_EPS = 64e-5
HG_WIDTH = 1024
HG_HEAD = 128
HG_HEADS = 8
PEER_HEADS = 8
PEER_NKEYS = 128
PEER_TOPK = 16
BF16_ROWS = 16
MM1_TOKENS = 256
EXP_M05 = float(np.exp(-0.5))
VMEM_LIMIT = 56 * 1024 * 1024


def _cp(sem):
    return pltpu.CompilerParams(dimension_semantics=sem, vmem_limit_bytes=VMEM_LIMIT)


def _split(x):
    hi = x.astype(BF16)
    return hi, (x - hi.astype(F32)).astype(BF16)


def _dot(a, b, dims, exact):
    dg = lambda p, q: lax.dot_general(p, q, dims, preferred_element_type=F32)
    if not exact:
        return dg(a.astype(BF16), b.astype(BF16))
    a_hi, a_lo = _split(a)
    b_hi, b_lo = _split(b)
    return dg(a_hi, b_hi) + (dg(a_hi, b_lo) + dg(a_lo, b_hi))


def _mm(a, b, exact=False):
    return _dot(a, b, (((1,), (0,)), ((), ())), exact)


def _mm_nt(a, b, exact=False):
    return _dot(a, b, (((1,), (1,)), ((), ())), exact)


def _mm_tn(a, b, exact=False):
    return _dot(a, b, (((0,), (0,)), ((), ())), exact)


def _mm_mask(mask01, x):
    m = mask01.astype(BF16)
    hi = x.astype(BF16)
    mid, lo = _split(x - hi.astype(F32))
    dg = lambda q: jnp.dot(m, q, preferred_element_type=F32)
    return dg(hi) + (dg(mid) + dg(lo))


def _sigmoid(x):
    return 1.0 / (1.0 + jnp.exp(-x))


def _norm_mod(x, gain, shift, scale):
    y = x * lax.rsqrt(jnp.mean(x * x, axis=-1, keepdims=True) + NORM_EPS) * gain
    return y * (1.0 + scale) + shift


def _ada_kernel(c_ref, w_ref, b_ref, o_ref):
    cv = c_ref[...]
    o_ref[...] = _mm(cv * _sigmoid(cv), w_ref[...], exact=True) + b_ref[...]


def _ada(cvec, w, b, tn=512):
    k, n = w.shape
    return pl.pallas_call(
        _ada_kernel,
        out_shape=jax.ShapeDtypeStruct((cvec.shape[0], n), F32),
        grid=(n // tn,),
        in_specs=[pl.BlockSpec((cvec.shape[0], k), lambda j: (0, 0)),
                  pl.BlockSpec((k, tn), lambda j: (0, j)),
                  pl.BlockSpec((1, tn), lambda j: (0, j))],
        out_specs=pl.BlockSpec((cvec.shape[0], tn), lambda j: (0, j)),
        compiler_params=_cp(("parallel",)),
        name="ada",
    )(cvec, w, b)


def _inproj_kernel(x_ref, g_ref, sh_ref, sc_ref, w_ref, *refs, bounds):
    out_refs, h_scr = refs[:-1], refs[-1]
    j = pl.program_id(1)

    @pl.when(j == 0)
    def _():
        for r in range(0, h_scr.shape[0], NORM_ROWS):
            rs = slice(r, r + NORM_ROWS)
            h_scr[rs, :] = _norm_mod(x_ref[rs, :], g_ref[...], sh_ref[...], sc_ref[...]).astype(BF16)

    z = jnp.dot(h_scr[...], w_ref[...], preferred_element_type=F32)
    for (lo, hi), o_ref in zip(bounds, out_refs):
        @pl.when((j >= lo) & (j < hi))
        def _(o_ref=o_ref):
            o_ref[...] = z


def _inproj(x2d, gain, shift, scale, w, widths, rows_per_mod, tm=1024, tn=512):
    m, k = x2d.shape
    tm = min(tm, rows_per_mod)
    assert rows_per_mod % tm == 0 and m % tm == 0 and all(wd % tn == 0 for wd in widths)
    bpm = rows_per_mod // tm
    mod_map = lambda i, j: (i // bpm, 0, 0)
    edges = np.cumsum([0] + [wd // tn for wd in widths])
    bounds = [(int(lo), int(hi)) for lo, hi in zip(edges[:-1], edges[1:])]
    out_spec = lambda lo, hi: pl.BlockSpec((tm, tn), lambda i, j: (i, jnp.clip(j - lo, 0, hi - lo - 1)))
    return pl.pallas_call(
        functools.partial(_inproj_kernel, bounds=bounds),
        out_shape=[jax.ShapeDtypeStruct((m, wd), F32) for wd in widths],
        grid=(m // tm, int(edges[-1])),
        in_specs=[pl.BlockSpec((tm, k), lambda i, j: (i, 0)),
                  pl.BlockSpec((1, k), lambda i, j: (0, 0)),
                  pl.BlockSpec((None, 1, k), mod_map),
                  pl.BlockSpec((None, 1, k), mod_map),
                  pl.BlockSpec((k, tn), lambda i, j: (0, j))],
        out_specs=[out_spec(lo, hi) for lo, hi in bounds],
        scratch_shapes=[pltpu.VMEM((tm, k), BF16)],
        compiler_params=_cp(("parallel", "arbitrary")),
        name="inproj",
    )(x2d, gain, shift, scale, w)


def _tri_masks(reverse):
    row = lax.broadcasted_iota(jnp.int32, (CHUNK, CHUNK), 0)
    col = lax.broadcasted_iota(jnp.int32, (CHUNK, CHUNK), 1)
    if reverse:
        strict, incl = col > row, col >= row
    else:
        strict, incl = col < row, col <= row
    same = (row // SUB) == (col // SUB)
    return strict, incl, same, row == col


def _rw_heads(ar_t, bk_t, a_0, r_0, bk_end, v_b, decay_tot, reverse, out_refs):
    strict, incl, same, eye = _tri_masks(reverse)
    row2 = lax.broadcasted_iota(jnp.int32, (CHUNK, 2 * CHUNK), 0)
    col2 = lax.broadcasted_iota(jnp.int32, (CHUNK, 2 * CHUNK), 1) % CHUNK
    incl2 = (col2 >= row2) if reverse else (col2 <= row2)
    ident = jnp.where(eye, 1.0, 0.0)
    heads = range(RW_HEADS)
    hs = lambda h: slice(h * RW_HEAD, (h + 1) * RW_HEAD)
    bf = lambda xs: [x.astype(BF16) for x in xs]
    mm = lambda xs, ys: [jnp.dot(x, y, preferred_element_type=F32) for x, y in zip(xs, ys)]
    nt_dims = (((1,), (1,)), ((), ()))
    tn_dims = (((0,), (0,)), ((), ()))

    s4 = [lax.dot_general(ar_t[:, hs(h)], bk_t[:, hs(h)], nt_dims, preferred_element_type=F32) for h in heads]
    m_ab = [jnp.where(strict, s[:CHUNK, :CHUNK], 0.0) for s in s4]
    m_ak = bf([jnp.where(strict, s[:CHUNK, CHUNK:], 0.0) for s in s4])
    m_r = bf([jnp.where(incl2, s[CHUNK:], 0.0) for s in s4])

    m_bd = [jnp.where(same, m, 0.0) for m in m_ab]
    m_off = bf([jnp.where(same, 0.0, m) for m in m_ab])
    m1 = bf(m_bd)
    m2 = bf(mm(m1, m1))
    m4 = bf(mm(m2, m2))
    m8 = bf(mm(m4, m4))
    p = [ident + m for m in m_bd]
    for mk in (m2, m4, m8):
        p = [x + y for x, y in zip(p, mm(bf(p), mk))]
    p_b = bf(p)
    n1 = mm(p_b, m_off)
    n1_b = bf(n1)
    n2 = mm(n1_b, n1_b)
    n3 = mm(n1_b, bf(n2))
    t_inv = bf(mm(bf([ident + a + b + c for a, b, c in zip(n1, n2, n3)]), p_b))

    x = bf(mm(m_ak, [v_b[:, hs(h)] for h in heads]))
    wu = bf(mm(t_inv, [jnp.concatenate([a_0[:, hs(h)], xh], axis=1) for h, xh in zip(heads, x)]))
    zero = jnp.zeros((CHUNK, RW_HEAD), BF16)
    rhs = [jnp.concatenate([w, jnp.concatenate([zero, v_b[:, hs(h)]], axis=1)], axis=0) for h, w in zip(heads, wu)]
    qo = mm(m_r, rhs)
    gd = [lax.dot_general(bk_end[:, hs(h)], r, tn_dims, preferred_element_type=F32) for h, r in zip(heads, rhs)]
    q_ref, o_ref, g_ref, d_ref = out_refs
    for h in heads:
        q_ref[:, hs(h)] = r_0[:, hs(h)] + qo[h][:, :RW_HEAD]
        o_ref[:, hs(h)] = qo[h][:, RW_HEAD:]
        g_ref[:, hs(h)] = gd[h][:, :RW_HEAD] + jnp.where(eye, decay_tot[:, hs(h)], 0.0)
        d_ref[:, hs(h)] = gd[h][:, RW_HEAD:]


def _rwkv_pre_kernel(zp_ref, zc_ref, zn_ref, conv_ref, w0_ref, wl_ref, a0_ref, al_ref, gl_ref, kk_ref, ka_ref,
                     rk_ref, qf_ref, of_ref, gf_ref, df_ref, qb_ref, ob_ref, gb_ref, db_ref, g_ref, bonus_ref,
                     conv_s, *, is_ctx, nchunks):
    c = pl.program_id(1)
    row = lax.broadcasted_iota(jnp.int32, (CHUNK, 1), 0)
    has_p = (c > 0).astype(F32)
    has_n = (c < nchunks - 1).astype(F32)

    def left(z, fill):
        return jnp.where(row == 0, fill, pltpu.roll(z, 1, 0))

    def right(z, fill):
        return jnp.where(row == CHUNK - 1, fill, pltpu.roll(z, CHUNK - 1, 0))

    for cb in range(RW_PAD // CONV_COLS):
        cs = slice(cb * CONV_COLS, (cb + 1) * CONV_COLS)
        w = conv_ref[:, cs]
        zc = zc_ref[:, cs]
        if is_ctx:
            conv = (left(zc, zp_ref[CHUNK - 1:CHUNK, cs] * has_p) * w[3:4] + zc * w[4:5]
                    + right(zc, zn_ref[0:1, cs] * has_n) * w[5:6])
        else:
            conv = None
            for dy, z in enumerate((zp_ref[:, cs] * has_p, zc, zn_ref[:, cs] * has_n)):
                t = (left(z, 0.0) * w[3 * dy:3 * dy + 1] + z * w[3 * dy + 1:3 * dy + 2]
                     + right(z, 0.0) * w[3 * dy + 2:3 * dy + 3])
                conv = t if conv is None else conv + t
        conv_s[:, cs] = conv

    r = conv_s[:, 0:RW_WIDTH]
    k = conv_s[:, RW_WIDTH:2 * RW_WIDTH]
    v = conv_s[:, 2 * RW_WIDTH:3 * RW_WIDTH]
    lo = conv_s[:, 3 * RW_WIDTH:RW_PAD]
    g_ref[...] = _mm(_sigmoid(lo[:, 4 * RW_LORA:]), gl_ref[...], exact=True)

    first = lax.broadcasted_iota(jnp.int32, (1, 2 * RW_HEAD), 1) < RW_HEAD

    def head_sum(t):
        tiles = []
        for i in range(RW_WIDTH // (2 * RW_HEAD)):
            ti = t[:, i * 2 * RW_HEAD:(i + 1) * 2 * RW_HEAD]
            s0 = jnp.sum(jnp.where(first, ti, 0.0), axis=-1, keepdims=True)
            s1 = jnp.sum(jnp.where(first, 0.0, ti), axis=-1, keepdims=True)
            tiles.append(jnp.where(first, s0, s1))
        return jnp.concatenate(tiles, axis=1)

    kkr = k * kk_ref[...]
    kk = kkr / jnp.maximum(jnp.sqrt(head_sum(kkr * kkr)), 1e-12)
    v_b = v.astype(BF16)
    outs = ((qf_ref, of_ref, gf_ref, df_ref), (qb_ref, ob_ref, gb_ref, db_ref))
    trow = lax.broadcasted_iota(jnp.int32, (CHUNK, CHUNK), 0)
    tcol = lax.broadcasted_iota(jnp.int32, (CHUNK, CHUNK), 1)
    bonus = None
    for d in range(2):
        wa = lo[:, d * RW_LORA:(d + 1) * RW_LORA]
        aa = lo[:, (2 + d) * RW_LORA:(3 + d) * RW_LORA]
        w_pre = w0_ref[d:d + 1, :] + _mm(jnp.tanh(wa), wl_ref[d], exact=True)
        logw = -EXP_M05 * _sigmoid(w_pre)
        a = _sigmoid(a0_ref[d:d + 1, :] + _mm(aa, al_ref[d], exact=True))
        tri = jnp.where((tcol >= trow) if d == 1 else (tcol <= trow), 1.0, 0.0)
        cum = _mm_mask(tri, logw)
        prev = cum - logw
        mid = CHUNK // 2 - 1 if d == 1 else CHUNK // 2
        end = 0 if d == 1 else CHUNK - 1
        rho = cum[mid:mid + 1]
        tot = cum[end:end + 1]
        kd = k * (1.0 + (a - 1.0) * ka_ref[...])
        kka = kk * a
        hb = head_sum(r * kd * rk_ref[...])
        bonus = hb if bonus is None else bonus + hb
        e_out = jnp.exp(rho - cum)
        e_end = jnp.exp(tot - cum)
        ar_t = jnp.concatenate([-kk * jnp.exp(prev - rho), r * jnp.exp(cum - rho)], axis=0).astype(BF16)
        bk_t = jnp.concatenate([kka * e_out, kd * e_out], axis=0).astype(BF16)
        bk_end = jnp.concatenate([kka * e_end, kd * e_end], axis=0).astype(BF16)
        a_0 = (-kk * jnp.exp(prev)).astype(BF16)
        r_0 = r * jnp.exp(cum)
        _rw_heads(ar_t, bk_t, a_0, r_0, bk_end, v_b, jnp.exp(tot), d == 1, outs[d])
    bonus_ref[...] = bonus * v


def _rwkv_pre(z_rw, conv_w, w0, wl, a0, al, gl, k_k, k_a, r_k, batch, nchunks, is_ctx):
    rows = z_rw.shape[0]
    last = nchunks - 1
    zspec = lambda f: pl.BlockSpec((CHUNK, RW_PAD), lambda b, c: (b * nchunks + f(c), 0))
    full = lambda shape: pl.BlockSpec(shape, lambda b, c: (0,) * len(shape))
    ospec = pl.BlockSpec((CHUNK, RW_WIDTH), lambda b, c: (b * nchunks + c, 0))
    out = jax.ShapeDtypeStruct((rows, RW_WIDTH), F32)
    return pl.pallas_call(
        functools.partial(_rwkv_pre_kernel, is_ctx=is_ctx, nchunks=nchunks),
        out_shape=[out] * 10,
        grid=(batch, nchunks),
        in_specs=[zspec(lambda c: jnp.maximum(c - 1, 0)), zspec(lambda c: c), zspec(lambda c: jnp.minimum(c + 1, last)),
                  full((9, RW_PAD)), full((2, RW_WIDTH)), full((2, RW_LORA, RW_WIDTH)), full((2, RW_WIDTH)),
                  full((2, RW_LORA, RW_WIDTH)), full((RW_PAD - 3 * RW_WIDTH - 4 * RW_LORA, RW_WIDTH)), full((1, RW_WIDTH)),
                  full((1, RW_WIDTH)), full((1, RW_WIDTH))],
        out_specs=[ospec] * 10,
        scratch_shapes=[pltpu.VMEM((CHUNK, RW_PAD), F32)],
        compiler_params=_cp(("parallel", "parallel")),
        name="rwkv_pre",
    )(z_rw, z_rw, z_rw, conv_w, w0, wl, a0, al, gl, k_k, k_a, r_k)


def _rwkv_seq_kernel(x0f_ref, x0b_ref, qf_ref, of_ref, gf_ref, df_ref, qb_ref, ob_ref, gb_ref, db_ref,
                     outf_ref, outb_ref, xff_ref, xfb_ref, x_s, *, nsteps):
    c = pl.program_id(1)

    @pl.when(c == 0)
    def _():
        x_s[0] = x0f_ref[...]
        x_s[1] = x0b_ref[...]

    dirs = ((qf_ref, of_ref, gf_ref, df_ref, outf_ref), (qb_ref, ob_ref, gb_ref, db_ref, outb_ref))
    hs = lambda h: slice(h * RW_HEAD, (h + 1) * RW_HEAD)
    for j in range(SEQ_CHUNKS):
        for d, (q_ref, o_ref, g_ref, d_ref, out_ref) in enumerate(dirs):
            jj = SEQ_CHUNKS - 1 - j if d == 1 else j
            rows = slice(jj * CHUNK, (jj + 1) * CHUNK)
            res = [_mm(jnp.concatenate([q_ref[rows, hs(h)], g_ref[rows, hs(h)]], axis=0), x_s[d, h], exact=True)
                   for h in range(RW_HEADS)]
            for h in range(RW_HEADS):
                out_ref[rows, hs(h)] = res[h][:CHUNK] + o_ref[rows, hs(h)]
                x_s[d, h] = res[h][CHUNK:] + d_ref[rows, hs(h)]

    @pl.when(c == nsteps - 1)
    def _():
        xff_ref[...] = x_s[0]
        xfb_ref[...] = x_s[1]


def _rwkv_seq(x0f, x0b, pre, batch, nchunks):
    nsteps = nchunks // SEQ_CHUNKS
    blk = (SEQ_CHUNKS * CHUNK, RW_WIDTH)
    fspec = pl.BlockSpec(blk, lambda b, c: (b * nsteps + c, 0))
    bspec = pl.BlockSpec(blk, lambda b, c: (b * nsteps + nsteps - 1 - c, 0))
    sspec = pl.BlockSpec((None, RW_HEADS, RW_HEAD, RW_HEAD), lambda b, c: (b, 0, 0, 0))
    seq = jax.ShapeDtypeStruct(pre[0].shape, F32)
    state = jax.ShapeDtypeStruct(x0f.shape, F32)
    return pl.pallas_call(
        functools.partial(_rwkv_seq_kernel, nsteps=nsteps),
        out_shape=[seq, seq, state, state],
        grid=(batch, nsteps),
        in_specs=[sspec, sspec] + [fspec] * 4 + [bspec] * 4,
        out_specs=[fspec, bspec, sspec, sspec],
        scratch_shapes=[pltpu.VMEM((2, RW_HEADS, RW_HEAD, RW_HEAD), F32)],
        compiler_params=_cp(("parallel", "arbitrary")),
        name="rwkv_seq",
    )(x0f, x0b, *pre)


def _hgrn_chunk(st_s, d, q_ref, f_ref, i_ref, lb, out_ref, rows):
    reverse = d == 1
    row = lax.broadcasted_iota(jnp.int32, (CHUNK, CHUNK), 0)
    col = lax.broadcasted_iota(jnp.int32, (CHUNK, CHUNK), 1)
    incl = (col >= row) if reverse else (col <= row)
    mid = CHUNK // 2 - 1 if reverse else CHUNK // 2
    end = 0 if reverse else CHUNK - 1

    forget = lb + (1.0 - lb) * _sigmoid(f_ref[rows, :])
    kf = 1.0 - forget
    qraw = q_ref[rows, :]
    q = qraw * _sigmoid(qraw)
    v = i_ref[rows, :]
    cum = _mm_mask(jnp.where(incl, 1.0, 0.0), jnp.log(forget))
    ref = cum[mid:mid + 1]
    tot = cum[end:end + 1]
    qs = q * jnp.exp(cum - ref)
    ks = kf * jnp.exp(ref - cum)
    qe = q * jnp.exp(cum)
    ke = kf * jnp.exp(tot - cum)
    dec = jnp.exp(tot)
    for h in range(HG_HEADS):
        sl = slice(h * HG_HEAD, (h + 1) * HG_HEAD)
        st = st_s[d, h]
        scores = jnp.where(incl, _mm_nt(qs[:, sl], ks[:, sl]), 0.0)
        out_ref[rows, sl] = _mm(scores, v[:, sl]) + _mm_nt(qe[:, sl], st)
        st_s[d, h] = st * dec[:, sl] + _mm_tn(v[:, sl], ke[:, sl])


def _hgrn_kernel(s0f_ref, s0b_ref, qf_ref, ff_ref, if_ref, qb_ref, fb_ref, ib_ref, lb_ref, outf_ref, outb_ref,
                 sff_ref, sfb_ref, st_s, *, nsteps):
    c = pl.program_id(1)

    @pl.when(c == 0)
    def _():
        st_s[0] = s0f_ref[...]
        st_s[1] = s0b_ref[...]

    for j in range(SEQ_CHUNKS):
        rows = slice(j * CHUNK, (j + 1) * CHUNK)
        _hgrn_chunk(st_s, 0, qf_ref, ff_ref, if_ref, lb_ref[0:1, :], outf_ref, rows)
        jj = SEQ_CHUNKS - 1 - j
        rows = slice(jj * CHUNK, (jj + 1) * CHUNK)
        _hgrn_chunk(st_s, 1, qb_ref, fb_ref, ib_ref, lb_ref[1:2, :], outb_ref, rows)

    @pl.when(c == nsteps - 1)
    def _():
        sff_ref[...] = st_s[0]
        sfb_ref[...] = st_s[1]


def _hgrn(s0f, s0b, z_hg, lb, batch, nchunks):
    nsteps = nchunks // SEQ_CHUNKS
    blk = (SEQ_CHUNKS * CHUNK, HG_WIDTH)
    fspec = lambda col: pl.BlockSpec(blk, lambda b, c: (b * nsteps + c, col))
    bspec = lambda col: pl.BlockSpec(blk, lambda b, c: (b * nsteps + nsteps - 1 - c, col))
    sspec = pl.BlockSpec((None, HG_HEADS, HG_HEAD, HG_HEAD), lambda b, c: (b, 0, 0, 0))
    seq = jax.ShapeDtypeStruct((z_hg.shape[0], HG_WIDTH), F32)
    state = jax.ShapeDtypeStruct(s0f.shape, F32)
    return pl.pallas_call(
        functools.partial(_hgrn_kernel, nsteps=nsteps),
        out_shape=[seq, seq, state, state],
        grid=(batch, nsteps),
        in_specs=[sspec, sspec, fspec(0), fspec(1), fspec(3), bspec(0), bspec(2), bspec(3),
                  pl.BlockSpec((2, HG_WIDTH), lambda b, c: (0, 0))],
        out_specs=[fspec(0), bspec(0), sspec, sspec],
        scratch_shapes=[pltpu.VMEM((2, HG_HEADS, HG_HEAD, HG_HEAD), F32)],
        compiler_params=_cp(("parallel", "arbitrary")),
        name="hgrn",
    )(s0f, s0b, z_hg, z_hg, z_hg, z_hg, z_hg, z_hg, lb)


def _readout_kernel(x_ref, of_ref, ob_ref, bonus_ref, g_ref, hf_ref, hb_ref, og_ref, grw_ref, ghg_ref, gmx_ref,
                    lnw_ref, lnb_ref, hgn_ref, wrw_ref, whg_ref, wout_ref, o_ref, y_s):
    first = lax.broadcasted_iota(jnp.int32, (1, 2 * RW_HEAD), 1) < RW_HEAD
    tiles = [slice(i * 2 * RW_HEAD, (i + 1) * 2 * RW_HEAD) for i in range(RW_WIDTH // (2 * RW_HEAD))]

    def head_mean(ts):
        s0 = [jnp.sum(jnp.where(first, t, 0.0), axis=-1, keepdims=True) for t in ts]
        s1 = [jnp.sum(jnp.where(first, 0.0, t), axis=-1, keepdims=True) for t in ts]
        return [jnp.where(first, a, b) * (1.0 / RW_HEAD) for a, b in zip(s0, s1)]

    o_rw = [of_ref[:, sl] + ob_ref[:, sl] for sl in tiles]
    dv = [o - mu for o, mu in zip(o_rw, head_mean(o_rw))]
    var = head_mean([d * d for d in dv])
    for sl, d, vr in zip(tiles, dv, var):
        y_s[:, sl] = d * lax.rsqrt(vr + RW_GN_EPS)
    y_rw = (y_s[...] * lnw_ref[...] + lnb_ref[...] + bonus_ref[...]) * g_ref[...]
    branch_rw = _mm(y_rw, wrw_ref[...])

    hg_tiles = [slice(h * HG_HEAD, (h + 1) * HG_HEAD) for h in range(HG_HEADS)]
    o_hg = [hf_ref[:, sl] + hb_ref[:, sl] for sl in hg_tiles]
    ms = [jnp.mean(o * o, axis=-1, keepdims=True) for o in o_hg]
    for sl, o, m2 in zip(hg_tiles, o_hg, ms):
        y_s[:, sl] = o * lax.rsqrt(m2 + NORM_EPS)
    og = og_ref[...]
    y_hg = y_s[...] * hgn_ref[...] * (og * _sigmoid(og))
    branch_hg = _mm(y_hg, whg_ref[...])

    merged = _sigmoid(grw_ref[...]) * branch_rw + _sigmoid(ghg_ref[...]) * branch_hg
    o_ref[...] = x_ref[...] + gmx_ref[...] * _mm(merged, wout_ref[...])


def _readout(x2d, o_f, o_b, bonus, g, h_f, h_b, z_hg, z_g, g_mx, ln_w, ln_b, hg_norm, w_rw, w_hg, w_out,
             rows_per_mod, tm=256):
    m, dm = x2d.shape
    bpm = rows_per_mod // tm
    tok = lambda width, blk=0: pl.BlockSpec((tm, width), lambda i: (i, blk))
    const = lambda shape: pl.BlockSpec(shape, lambda i: (0,) * len(shape), pipeline_mode=pl.Buffered(1))
    return pl.pallas_call(
        _readout_kernel,
        out_shape=jax.ShapeDtypeStruct((m, dm), F32),
        grid=(m // tm,),
        in_specs=[tok(dm), tok(RW_WIDTH), tok(RW_WIDTH), tok(RW_WIDTH), tok(RW_WIDTH), tok(HG_WIDTH), tok(HG_WIDTH),
                  tok(HG_WIDTH, 4), tok(dm, 0), tok(dm, 1),
                  pl.BlockSpec((None, 1, dm), lambda i: (i // bpm, 0, 0)),
                  const((1, RW_WIDTH)), const((1, RW_WIDTH)), const((1, HG_WIDTH)),
                  const((RW_WIDTH, dm)), const((HG_WIDTH, dm)), const((dm, dm))],
        out_specs=tok(dm),
        scratch_shapes=[pltpu.VMEM((tm, RW_WIDTH), F32)],
        compiler_params=_cp(("parallel",)),
        name="readout",
    )(x2d, o_f, o_b, bonus, g, h_f, h_b, z_hg, z_g, z_g, g_mx, ln_w, ln_b, hg_norm, w_rw, w_hg, w_out)


def _cand_groups():
    return [(i, PEER_TOPK // (i + 1)) for i in range(PEER_TOPK)]


def _pick_max(work, io, first_only):
    m = jnp.max(work, axis=0, keepdims=True)
    sel = work == m
    if first_only:
        sel = io == jnp.min(jnp.where(sel, io, work.shape[0]), axis=0, keepdims=True)
    return m, sel


def _all_tokens(pred):
    return jnp.min(jnp.where(pred, 1.0, 0.0)) > 0.5


def _topk_ranks(s, vals_ref, rank_ref):
    io = lax.broadcasted_iota(jnp.int32, s.shape, 0)
    init = (s, jnp.full(s.shape, float(PEER_TOPK), F32))

    def run(first_only):
        def body(r, carry):
            work, rank = carry
            m, sel = _pick_max(work, io, first_only)
            vals_ref[pl.ds(r, 1), :] = m
            return jnp.where(sel, -jnp.inf, work), jnp.where(sel, lax.convert_element_type(r, F32), rank)

        rank_ref[...] = lax.fori_loop(0, PEER_TOPK, body, init)[1]

    run(False)
    taken = jnp.sum(jnp.where(rank_ref[...] < float(PEER_TOPK), 1.0, 0.0), axis=0, keepdims=True)

    @pl.when(jnp.logical_not(_all_tokens(taken == float(PEER_TOPK))))
    def _():
        run(True)

    return rank_ref[...]


def _peer_sel_kernel(x_ref, gain_ref, sh_ref, sc_ref, wq_ref, k1_ref, k2_ref, hx_ref, cnt_ref, rk_ref, e1_ref,
                     e2_ref, v1_s, v2_s, r1_s, r2_s, taken_s, z_s):
    hb = _norm_mod(x_ref[...], gain_ref[...], sh_ref[...], sc_ref[...]).astype(BF16)
    hx_ref[...] = hb
    q = jnp.dot(hb, wq_ref[...], preferred_element_type=F32)
    tm = q.shape[0]
    groups = _cand_groups()
    ncand = sum(n for _, n in groups)
    npad = -ncand % 8
    for hd in range(PEER_HEADS):
        q1 = q[:, (2 * hd) * PEER_NKEYS:(2 * hd + 1) * PEER_NKEYS]
        q2 = q[:, (2 * hd + 1) * PEER_NKEYS:(2 * hd + 2) * PEER_NKEYS]
        s1 = _mm_nt(k1_ref[...], q1, exact=True)
        s2 = _mm_nt(k2_ref[...], q2, exact=True)
        rank1 = _topk_ranks(s1, v1_s, r1_s)
        rank2 = _topk_ranks(s2, v2_s, r2_s)
        v1 = v1_s[...]
        v2 = v2_s[...]
        rows = [v1[i:i + 1] + v2[0:n] for i, n in groups]
        if npad:
            rows.append(jnp.full((npad, tm), -jnp.inf, F32))
        cand = jnp.concatenate(rows, axis=0)
        io = lax.broadcasted_iota(jnp.int32, cand.shape, 0)
        best0 = v1[0:1] + v2[0:1]

        def run(first_only):
            def body(r, carry):
                work, z = carry
                m, sel = _pick_max(work, io, first_only)
                return jnp.where(sel, -jnp.inf, work), z + jnp.exp(m - best0)

            work, z = lax.fori_loop(0, PEER_TOPK, body, (cand, jnp.zeros((1, tm), F32)))
            taken_s[...] = jnp.where((work == -jnp.inf) & (io < ncand), 1.0, 0.0)
            z_s[...] = z

        run(False)

        @pl.when(jnp.logical_not(_all_tokens(jnp.sum(taken_s[...], axis=0, keepdims=True) == float(PEER_TOPK))))
        def _():
            run(True)

        taken = taken_s[...]
        z = z_s[...]
        cnt1 = jnp.zeros_like(rank1)
        start = 0
        for i, n in groups:
            c_i = jnp.sum(taken[start:start + n], axis=0, keepdims=True)
            cnt1 = cnt1 + jnp.where(rank1 == float(i), c_i, 0.0)
            start += n
        cnt_ref[hd] = cnt1
        rk_ref[hd] = rank2.astype(BF16)
        e1_ref[hd] = jnp.exp(s1 - v1[0:1])
        e2_ref[hd] = (jnp.exp(s2 - v2[0:1]) / z).astype(BF16)


def _peer_sel(x2d, gain, shift, scale, wq, k1, k2, rows_per_mod, tm=256):
    m, dm = x2d.shape
    bpm = rows_per_mod // tm
    mod_map = lambda i: (i // bpm, 0, 0)
    const = lambda shape: pl.BlockSpec(shape, lambda i: (0,) * len(shape))
    sel = jax.ShapeDtypeStruct((PEER_HEADS, PEER_NKEYS, m), F32)
    sel_bf = jax.ShapeDtypeStruct((PEER_HEADS, PEER_NKEYS, m), BF16)
    sel_spec = pl.BlockSpec((PEER_HEADS, PEER_NKEYS, tm), lambda i: (0, 0, i))
    ncand = sum(n for _, n in _cand_groups())
    ncand_pad = ncand + (-ncand % 8)
    return pl.pallas_call(
        _peer_sel_kernel,
        out_shape=[jax.ShapeDtypeStruct((m, dm), BF16), sel, sel_bf, sel, sel_bf],
        grid=(m // tm,),
        in_specs=[pl.BlockSpec((tm, dm), lambda i: (i, 0)), const((1, dm)),
                  pl.BlockSpec((None, 1, dm), mod_map), pl.BlockSpec((None, 1, dm), mod_map),
                  const(wq.shape), const(k1.shape), const(k2.shape)],
        out_specs=[pl.BlockSpec((tm, dm), lambda i: (i, 0)), sel_spec, sel_spec, sel_spec, sel_spec],
        scratch_shapes=[pltpu.VMEM((PEER_TOPK, tm), F32), pltpu.VMEM((PEER_TOPK, tm), F32),
                        pltpu.VMEM((PEER_NKEYS, tm), F32), pltpu.VMEM((PEER_NKEYS, tm), F32),
                        pltpu.VMEM((ncand_pad, tm), F32), pltpu.VMEM((1, tm), F32)],
        compiler_params=_cp(("parallel",)),
        name="peer_sel",
    )(x2d, gain, shift, scale, wq, k1, k2)


def _gelu(x):
    return 0.5 * x * (1.0 + lax.erf(x * float(np.sqrt(0.5))))


def _peer_dense_kernel(hx_ref, u_ref, v_ref, cnt_ref, rk_ref, e1_ref, e2_ref, x_ref, gfx_ref, gain_ref, o_ref,
                       wa_s, wb_s, ga_s, gb_s, *, te, nblk):
    e = pl.program_id(1)
    half = te // 2
    nsub = half // PEER_NKEYS
    tm = hx_ref.shape[0]

    def gates(half_blk, dst):
        packed = (PEER_NKEYS // BF16_ROWS, BF16_ROWS, tm)
        for ii in range(nsub):
            key1 = half_blk * nsub + ii
            gate = None
            for hd in range(PEER_HEADS):
                cnt = jnp.broadcast_to(cnt_ref[hd, pl.ds(key1, 1), :], packed[1:]).astype(BF16)
                e1 = jnp.broadcast_to(e1_ref[hd, pl.ds(key1, 1), :], packed[1:]).astype(BF16)
                t = jnp.where(rk_ref[hd].reshape(packed) < cnt[None], e2_ref[hd].reshape(packed) * e1[None],
                              jnp.zeros((), BF16))
                gate = t if gate is None else gate + t
            dst[ii * PEER_NKEYS:(ii + 1) * PEER_NKEYS, :] = gate.reshape(PEER_NKEYS, tm)

    @pl.when(e == 0)
    def _():
        o_ref[...] = jnp.zeros_like(o_ref)
        gates(0, ga_s)

    nt_dims = (((1,), (1,)), ((), ()))
    pieces = tm // MM1_TOKENS

    def weighted(u_rows, g_s, w_s):
        for t in range(pieces):
            ts = slice(t * MM1_TOKENS, (t + 1) * MM1_TOKENS)
            act = _gelu(lax.dot_general(u_ref[u_rows, :], hx_ref[ts, :], nt_dims, preferred_element_type=F32))
            w_s[ts, :] = (g_s[:, ts] * act.astype(BF16)).T

    gates(2 * e + 1, gb_s)
    weighted(slice(0, half), ga_s, wa_s)
    weighted(slice(half, te), gb_s, wb_s)
    o_ref[...] += jnp.dot(wa_s[...], v_ref[:half, :], preferred_element_type=F32)
    gates(2 * ((e + 1) % nblk), ga_s)
    o_ref[...] += jnp.dot(wb_s[...], v_ref[half:, :], preferred_element_type=F32)

    @pl.when(e == nblk - 1)
    def _():
        x = x_ref[...] + gfx_ref[...] * o_ref[...]
        o_ref[...] = x * lax.rsqrt(jnp.mean(x * x, axis=-1, keepdims=True) + NORM_EPS) * gain_ref[...]


def _peer_dense(hx, u, v, cnt, rk, e1, e2, x2d, g_fx, gain, rows_per_mod, tm=512, te=1024):
    m, dm = hx.shape
    ne = u.shape[0]
    nblk = ne // te
    bpm = rows_per_mod // tm
    sel_spec = pl.BlockSpec((PEER_HEADS, PEER_NKEYS, tm), lambda i, e: (0, 0, i))
    tok = pl.BlockSpec((tm, dm), lambda i, e: (i, 0))
    tab = pl.BlockSpec((te, dm), lambda i, e: (e, 0))
    return pl.pallas_call(
        functools.partial(_peer_dense_kernel, te=te, nblk=nblk),
        out_shape=jax.ShapeDtypeStruct((m, dm), F32),
        grid=(m // tm, nblk),
        in_specs=[tok, tab, tab, sel_spec, sel_spec, sel_spec, sel_spec, tok,
                  pl.BlockSpec((None, 1, dm), lambda i, e: (i // bpm, 0, 0)),
                  pl.BlockSpec((1, dm), lambda i, e: (0, 0))],
        out_specs=tok,
        scratch_shapes=[pltpu.VMEM((tm, te // 2), BF16)] * 2 + [pltpu.VMEM((te // 2, tm), BF16)] * 2,
        compiler_params=_cp(("parallel", "arbitrary")),
        name="peer_dense",
    )(hx, u, v, cnt, rk, e1, e2, x2d, g_fx, gain)


def kernel(x, c, ctx, c_ctx, w_ada, b_ada, norm_mix, norm_ffn, norm_final, w_in, rw_conv, rw_w0, rw_w_lora_b, rw_a0, rw_a_lora_b, rw_g_lora_b, rw_k_k, rw_k_a, rw_r_k, rw_ln_w, rw_ln_b, hg_lb, hg_norm, w_up_rw, w_up_hg, w_out, peer_wq, peer_k1, peer_k2, peer_u, peer_v):
    batch, seq, dm = x.shape
    ctx_len = ctx.shape[1]
    depth = w_in.shape[0]
    assert depth == 1 and dm == D_MODEL
    assert seq % (CHUNK * SEQ_CHUNKS) == 0 and ctx_len % (CHUNK * SEQ_CHUNKS) == 0
    n_x = seq // CHUNK
    n_c = ctx_len // CHUNK
    l = 0

    w_in_p = jnp.concatenate([w_in[l, :, :RW_COLS].astype(BF16), jnp.zeros((dm, RW_PAD - RW_COLS), BF16),
                              w_in[l, :, RW_COLS:].astype(BF16)], axis=1)
    conv_w = jnp.pad(rw_conv[l].reshape(9, RW_COLS), ((0, 0), (0, RW_PAD - RW_COLS)))
    lower_bounds = jnp.cumsum(jax.nn.softmax(hg_lb.astype(F32), axis=0), axis=0)[l]
    row = lambda t: t.reshape(1, -1)
    hg_norm_row = jnp.tile(hg_norm[l], HG_HEADS).reshape(1, HG_WIDTH)
    u_bf = peer_u[l].astype(BF16)
    v_bf = peer_v[l].astype(BF16)

    cvec = jnp.concatenate([c, c_ctx[None, :], jnp.zeros((8 - batch - 1, dm), F32)], axis=0)
    mod = _ada(cvec, w_ada[l], row(b_ada[l]))
    mods = mod.reshape(8, N_MOD, 1, dm)
    sh_mx, sc_mx, g_mx, sh_fx, sc_fx, g_fx = (mods[:batch, i] for i in range(N_MOD))
    sh_mc, sc_mc = mods[batch:batch + 1, 0], mods[batch:batch + 1, 1]

    x2d = x.reshape(batch * seq, dm)
    c2d = ctx.reshape(batch * ctx_len, dm)
    gain_mix = row(norm_mix[l])

    zc_rw, zc_hg = _inproj(c2d, gain_mix, sh_mc, sc_mc, w_in_p, (RW_PAD, 5 * HG_WIDTH), batch * ctx_len)
    g_lora = jnp.pad(rw_g_lora_b[l], ((0, RW_PAD - RW_COLS), (0, 0)))
    rw_args = (conv_w, rw_w0[l], rw_w_lora_b[l], rw_a0[l], rw_a_lora_b[l], g_lora, row(rw_k_k[l]),
               row(rw_k_a[l]), row(rw_r_k[l]))
    pre_c = _rwkv_pre(zc_rw, *rw_args, batch=batch, nchunks=n_c, is_ctx=True)
    zero_rw = jnp.zeros((batch, RW_HEADS, RW_HEAD, RW_HEAD), F32)
    zero_hg = jnp.zeros((batch, HG_HEADS, HG_HEAD, HG_HEAD), F32)
    _, _, st_rw_f, st_rw_b = _rwkv_seq(zero_rw, zero_rw, pre_c[0:8], batch, n_c)
    _, _, st_hg_f, st_hg_b = _hgrn(zero_hg, zero_hg, zc_hg, lower_bounds, batch, n_c)

    z_rw, z_hg, z_g = _inproj(x2d, gain_mix, sh_mx, sc_mx, w_in_p, (RW_PAD, 5 * HG_WIDTH, 2 * dm), seq)
    pre = _rwkv_pre(z_rw, *rw_args, batch=batch, nchunks=n_x, is_ctx=False)
    o_f, o_b, _, _ = _rwkv_seq(st_rw_f, st_rw_b, pre[0:8], batch, n_x)
    h_f, h_b, _, _ = _hgrn(st_hg_f, st_hg_b, z_hg, lower_bounds, batch, n_x)
    x1 = _readout(x2d, o_f, o_b, pre[9], pre[8], h_f, h_b, z_hg, z_g, g_mx, row(rw_ln_w[l]), row(rw_ln_b[l]),
                  hg_norm_row, w_up_rw[l].astype(BF16), w_up_hg[l].astype(BF16), w_out[l].astype(BF16), seq)

    hx, cnt, rk, e1, e2 = _peer_sel(x1, row(norm_ffn[l]), sh_fx, sc_fx, peer_wq[l].astype(BF16), peer_k1[l],
                                    peer_k2[l], seq)
    out = _peer_dense(hx, u_bf, v_bf, cnt, rk, e1, e2, x1, g_fx, row(norm_final), seq)
    return out.reshape(batch, seq, dm)
```

```python
import functools

import jax
import jax.numpy as jnp
import numpy as np
from jax import lax
from jax.experimental import pallas as pl
from jax.experimental.pallas import tpu as pltpu

F32 = jnp.float32
BF16 = jnp.bfloat16

D_MODEL = 2048
N_MOD = 6
NORM_EPS = 1e-6
CHUNK = 64
SUB = 16
SEQ_CHUNKS = 4
RW_WIDTH = 1024
RW_HEAD = 64
RW_HEADS = 16
RW_LORA = 64
RW_GATE_RANK = 160
RW_COLS = 3488
RW_PAD = 3584
CONV_COLS = 512
NORM_ROWS = 128
RW_GN_EPS = 64e-5
HG_WIDTH = 1024
HG_HEAD = 128
HG_HEADS = 8
PEER_HEADS = 8
PEER_NKEYS = 128
PEER_TOPK = 16
BF16_ROWS = 16
MM1_TOKENS = 256
EXP_M05 = float(np.exp(-0.5))
VMEM_LIMIT = 56 * 1024 * 1024


def _cp(sem):
    return pltpu.CompilerParams(dimension_semantics=sem, vmem_limit_bytes=VMEM_LIMIT)


def _split(x):
    hi = x.astype(BF16)
    return hi, (x - hi.astype(F32)).astype(BF16)


def _dot(a, b, dims, exact):
    dg = lambda p, q: lax.dot_general(p, q, dims, preferred_element_type=F32)
    if not exact:
        return dg(a.astype(BF16), b.astype(BF16))
    a_hi, a_lo = _split(a)
    b_hi, b_lo = _split(b)
    return dg(a_hi, b_hi) + (dg(a_hi, b_lo) + dg(a_lo, b_hi))


def _mm(a, b, exact=False):
    return _dot(a, b, (((1,), (0,)), ((), ())), exact)


def _mm_nt(a, b, exact=False):
    return _dot(a, b, (((1,), (1,)), ((), ())), exact)


def _mm_tn(a, b, exact=False):
    return _dot(a, b, (((0,), (0,)), ((), ())), exact)


def _mm_mask(mask01, x):
    m = mask01.astype(BF16)
    hi = x.astype(BF16)
    mid, lo = _split(x - hi.astype(F32))
    dg = lambda q: jnp.dot(m, q, preferred_element_type=F32)
    return dg(hi) + (dg(mid) + dg(lo))


def _sigmoid(x):
    return 1.0 / (1.0 + jnp.exp(-x))


def _norm_mod(x, gain, shift, scale):
    y = x * lax.rsqrt(jnp.mean(x * x, axis=-1, keepdims=True) + NORM_EPS) * gain
    return y * (1.0 + scale) + shift


def _ada_kernel(c_ref, w_ref, b_ref, o_ref):
    cv = c_ref[...]
    o_ref[...] = _mm(cv * _sigmoid(cv), w_ref[...], exact=True) + b_ref[...]


def _ada(cvec, w, b, tn=512):
    k, n = w.shape
    return pl.pallas_call(
        _ada_kernel,
        out_shape=jax.ShapeDtypeStruct((cvec.shape[0], n), F32),
        grid=(n // tn,),
        in_specs=[pl.BlockSpec((cvec.shape[0], k), lambda j: (0, 0)),
                  pl.BlockSpec((k, tn), lambda j: (0, j)),
                  pl.BlockSpec((1, tn), lambda j: (0, j))],
        out_specs=pl.BlockSpec((cvec.shape[0], tn), lambda j: (0, j)),
        compiler_params=_cp(("parallel",)),
        name="ada",
    )(cvec, w, b)


def _regroup_kernel(w_ref, o_ref):
    w = w_ref[...]
    pad = jnp.zeros((w.shape[0], RW_PAD - RW_COLS), F32)
    o_ref[...] = jnp.concatenate([w[:, :RW_COLS], pad, w[:, RW_COLS:]], axis=1).astype(BF16)


def _regroup(w, tk=128):
    k, n = w.shape
    n_out = n + RW_PAD - RW_COLS
    return pl.pallas_call(
        _regroup_kernel,
        out_shape=jax.ShapeDtypeStruct((k, n_out), BF16),
        grid=(k // tk,),
        in_specs=[pl.BlockSpec((tk, n), lambda i: (i, 0))],
        out_specs=pl.BlockSpec((tk, n_out), lambda i: (i, 0)),
        compiler_params=_cp(("parallel",)),
        name="regroup",
    )(w)


def _inproj_kernel(x_ref, g_ref, sh_ref, sc_ref, w_ref, *refs, bounds):
    out_refs, h_scr = refs[:-1], refs[-1]
    j = pl.program_id(1)

    @pl.when(j == 0)
    def _():
        for r in range(0, h_scr.shape[0], NORM_ROWS):
            rs = slice(r, r + NORM_ROWS)
            h_scr[rs, :] = _norm_mod(x_ref[rs, :], g_ref[...], sh_ref[...], sc_ref[...]).astype(BF16)

    z = jnp.dot(h_scr[...], w_ref[...], preferred_element_type=F32)
    for (lo, hi), o_ref in zip(bounds, out_refs):
        @pl.when((j >= lo) & (j < hi))
        def _(o_ref=o_ref):
            o_ref[...] = z


def _inproj(x2d, gain, shift, scale, w, widths, rows_per_mod, tm=1024, tn=512):
    m, k = x2d.shape
    tm = min(tm, rows_per_mod)
    assert rows_per_mod % tm == 0 and m % tm == 0 and all(wd % tn == 0 for wd in widths)
    bpm = rows_per_mod // tm
    mod_map = lambda i, j: (i // bpm, 0, 0)
    edges = np.cumsum([0] + [wd // tn for wd in widths])
    bounds = [(int(lo), int(hi)) for lo, hi in zip(edges[:-1], edges[1:])]
    out_spec = lambda lo, hi: pl.BlockSpec((tm, tn), lambda i, j: (i, jnp.clip(j - lo, 0, hi - lo - 1)))
    return pl.pallas_call(
        functools.partial(_inproj_kernel, bounds=bounds),
        out_shape=[jax.ShapeDtypeStruct((m, wd), F32) for wd in widths],
        grid=(m // tm, int(edges[-1])),
        in_specs=[pl.BlockSpec((tm, k), lambda i, j: (i, 0)),
                  pl.BlockSpec((1, k), lambda i, j: (0, 0)),
                  pl.BlockSpec((None, 1, k), mod_map),
                  pl.BlockSpec((None, 1, k), mod_map),
                  pl.BlockSpec((k, tn), lambda i, j: (0, j))],
        out_specs=[out_spec(lo, hi) for lo, hi in bounds],
        scratch_shapes=[pltpu.VMEM((tm, k), BF16)],
        compiler_params=_cp(("parallel", "arbitrary")),
        name="inproj",
    )(x2d, gain, shift, scale, w)


def _tri_masks(reverse):
    row = lax.broadcasted_iota(jnp.int32, (CHUNK, CHUNK), 0)
    col = lax.broadcasted_iota(jnp.int32, (CHUNK, CHUNK), 1)
    if reverse:
        strict, incl = col > row, col >= row
    else:
        strict, incl = col < row, col <= row
    same = (row // SUB) == (col // SUB)
    return strict, incl, same, row == col


def _rw_heads(ar_t, bk_t, a_0, r_0, bk_end, v_b, decay_tot, reverse, out_refs):
    strict, incl, same, eye = _tri_masks(reverse)
    row2 = lax.broadcasted_iota(jnp.int32, (CHUNK, 2 * CHUNK), 0)
    col2 = lax.broadcasted_iota(jnp.int32, (CHUNK, 2 * CHUNK), 1) % CHUNK
    incl2 = (col2 >= row2) if reverse else (col2 <= row2)
    ident = jnp.where(eye, 1.0, 0.0)
    heads = range(RW_HEADS)
    hs = lambda h: slice(h * RW_HEAD, (h + 1) * RW_HEAD)
    bf = lambda xs: [x.astype(BF16) for x in xs]
    mm = lambda xs, ys: [jnp.dot(x, y, preferred_element_type=F32) for x, y in zip(xs, ys)]
    nt_dims = (((1,), (1,)), ((), ()))
    tn_dims = (((0,), (0,)), ((), ()))

    s4 = [lax.dot_general(ar_t[:, hs(h)], bk_t[:, hs(h)], nt_dims, preferred_element_type=F32) for h in heads]
    m_ab = [jnp.where(strict, s[:CHUNK, :CHUNK], 0.0) for s in s4]
    m_ak = bf([jnp.where(strict, s[:CHUNK, CHUNK:], 0.0) for s in s4])
    m_r = bf([jnp.where(incl2, s[CHUNK:], 0.0) for s in s4])

    m_bd = [jnp.where(same, m, 0.0) for m in m_ab]
    m_off = bf([jnp.where(same, 0.0, m) for m in m_ab])
    m1 = bf(m_bd)
    m2 = bf(mm(m1, m1))
    m4 = bf(mm(m2, m2))
    m8 = bf(mm(m4, m4))
    p = [ident + m for m in m_bd]
    for mk in (m2, m4, m8):
        p = [x + y for x, y in zip(p, mm(bf(p), mk))]
    p_b = bf(p)
    n1 = mm(p_b, m_off)
    n1_b = bf(n1)
    n2 = mm(n1_b, n1_b)
    n3 = mm(n1_b, bf(n2))
    t_inv = bf(mm(bf([ident + a + b + c for a, b, c in zip(n1, n2, n3)]), p_b))

    x = bf(mm(m_ak, [v_b[:, hs(h)] for h in heads]))
    wu = bf(mm(t_inv, [jnp.concatenate([a_0[:, hs(h)], xh], axis=1) for h, xh in zip(heads, x)]))
    zero = jnp.zeros((CHUNK, RW_HEAD), BF16)
    rhs = [jnp.concatenate([w, jnp.concatenate([zero, v_b[:, hs(h)]], axis=1)], axis=0) for h, w in zip(heads, wu)]
    qo = mm(m_r, rhs)
    gd = [lax.dot_general(bk_end[:, hs(h)], r, tn_dims, preferred_element_type=F32) for h, r in zip(heads, rhs)]
    q_ref, o_ref, g_ref, d_ref = out_refs
    for h in heads:
        q_ref[:, hs(h)] = r_0[:, hs(h)] + qo[h][:, :RW_HEAD]
        o_ref[:, hs(h)] = qo[h][:, RW_HEAD:]
        g_ref[:, hs(h)] = gd[h][:, :RW_HEAD] + jnp.where(eye, decay_tot[:, hs(h)], 0.0)
        d_ref[:, hs(h)] = gd[h][:, RW_HEAD:]


def _rwkv_pre_kernel(zp_ref, zc_ref, zn_ref, conv_ref, w0_ref, wl_ref, a0_ref, al_ref, gl_ref, kk_ref, ka_ref,
                     rk_ref, qf_ref, of_ref, gf_ref, df_ref, qb_ref, ob_ref, gb_ref, db_ref, g_ref, bonus_ref,
                     conv_s, *, is_ctx, nchunks):
    c = pl.program_id(1)
    row = lax.broadcasted_iota(jnp.int32, (CHUNK, 1), 0)
    has_p = (c > 0).astype(F32)
    has_n = (c < nchunks - 1).astype(F32)

    def left(z, fill):
        return jnp.where(row == 0, fill, pltpu.roll(z, 1, 0))

    def right(z, fill):
        return jnp.where(row == CHUNK - 1, fill, pltpu.roll(z, CHUNK - 1, 0))

    for cb in range(RW_PAD // CONV_COLS):
        cs = slice(cb * CONV_COLS, (cb + 1) * CONV_COLS)
        w = conv_ref[:, cs]
        zc = zc_ref[:, cs]
        if is_ctx:
            conv = (left(zc, zp_ref[CHUNK - 1:CHUNK, cs] * has_p) * w[3:4] + zc * w[4:5]
                    + right(zc, zn_ref[0:1, cs] * has_n) * w[5:6])
        else:
            conv = None
            for dy, z in enumerate((zp_ref[:, cs] * has_p, zc, zn_ref[:, cs] * has_n)):
                t = (left(z, 0.0) * w[3 * dy:3 * dy + 1] + z * w[3 * dy + 1:3 * dy + 2]
                     + right(z, 0.0) * w[3 * dy + 2:3 * dy + 3])
                conv = t if conv is None else conv + t
        conv_s[:, cs] = conv

    r = conv_s[:, 0:RW_WIDTH]
    k = conv_s[:, RW_WIDTH:2 * RW_WIDTH]
    v = conv_s[:, 2 * RW_WIDTH:3 * RW_WIDTH]
    lo = conv_s[:, 3 * RW_WIDTH:RW_PAD]
    g_ref[...] = _mm(_sigmoid(lo[:, 4 * RW_LORA:]), gl_ref[...], exact=True)

    first = lax.broadcasted_iota(jnp.int32, (1, 2 * RW_HEAD), 1) < RW_HEAD

    def head_sum(t):
        tiles = []
        for i in range(RW_WIDTH // (2 * RW_HEAD)):
            ti = t[:, i * 2 * RW_HEAD:(i + 1) * 2 * RW_HEAD]
            s0 = jnp.sum(jnp.where(first, ti, 0.0), axis=-1, keepdims=True)
            s1 = jnp.sum(jnp.where(first, 0.0, ti), axis=-1, keepdims=True)
            tiles.append(jnp.where(first, s0, s1))
        return jnp.concatenate(tiles, axis=1)

    kkr = k * kk_ref[...]
    kk = kkr / jnp.maximum(jnp.sqrt(head_sum(kkr * kkr)), 1e-12)
    v_b = v.astype(BF16)
    outs = ((qf_ref, of_ref, gf_ref, df_ref), (qb_ref, ob_ref, gb_ref, db_ref))
    trow = lax.broadcasted_iota(jnp.int32, (CHUNK, CHUNK), 0)
    tcol = lax.broadcasted_iota(jnp.int32, (CHUNK, CHUNK), 1)
    bonus = None
    for d in range(2):
        wa = lo[:, d * RW_LORA:(d + 1) * RW_LORA]
        aa = lo[:, (2 + d) * RW_LORA:(3 + d) * RW_LORA]
        w_pre = w0_ref[d:d + 1, :] + _mm(jnp.tanh(wa), wl_ref[d], exact=True)
        logw = -EXP_M05 * _sigmoid(w_pre)
        a = _sigmoid(a0_ref[d:d + 1, :] + _mm(aa, al_ref[d], exact=True))
        tri = jnp.where((tcol >= trow) if d == 1 else (tcol <= trow), 1.0, 0.0)
        cum = _mm_mask(tri, logw)
        prev = cum - logw
        mid = CHUNK // 2 - 1 if d == 1 else CHUNK // 2
        end = 0 if d == 1 else CHUNK - 1
        rho = cum[mid:mid + 1]
        tot = cum[end:end + 1]
        kd = k * (1.0 + (a - 1.0) * ka_ref[...])
        kka = kk * a
        hb = head_sum(r * kd * rk_ref[...])
        bonus = hb if bonus is None else bonus + hb
        e_out = jnp.exp(rho - cum)
        e_end = jnp.exp(tot - cum)
        ar_t = jnp.concatenate([-kk * jnp.exp(prev - rho), r * jnp.exp(cum - rho)], axis=0).astype(BF16)
        bk_t = jnp.concatenate([kka * e_out, kd * e_out], axis=0).astype(BF16)
        bk_end = jnp.concatenate([kka * e_end, kd * e_end], axis=0).astype(BF16)
        a_0 = (-kk * jnp.exp(prev)).astype(BF16)
        r_0 = r * jnp.exp(cum)
        _rw_heads(ar_t, bk_t, a_0, r_0, bk_end, v_b, jnp.exp(tot), d == 1, outs[d])
    bonus_ref[...] = bonus * v


def _rwkv_pre(z_rw, conv_w, w0, wl, a0, al, gl, k_k, k_a, r_k, batch, nchunks, is_ctx):
    rows = z_rw.shape[0]
    last = nchunks - 1
    zspec = lambda f: pl.BlockSpec((CHUNK, RW_PAD), lambda b, c: (b * nchunks + f(c), 0))
    full = lambda shape: pl.BlockSpec(shape, lambda b, c: (0,) * len(shape))
    ospec = pl.BlockSpec((CHUNK, RW_WIDTH), lambda b, c: (b * nchunks + c, 0))
    out = jax.ShapeDtypeStruct((rows, RW_WIDTH), F32)
    return pl.pallas_call(
        functools.partial(_rwkv_pre_kernel, is_ctx=is_ctx, nchunks=nchunks),
        out_shape=[out] * 10,
        grid=(batch, nchunks),
        in_specs=[zspec(lambda c: jnp.maximum(c - 1, 0)), zspec(lambda c: c), zspec(lambda c: jnp.minimum(c + 1, last)),
                  full((9, RW_PAD)), full((2, RW_WIDTH)), full((2, RW_LORA, RW_WIDTH)), full((2, RW_WIDTH)),
                  full((2, RW_LORA, RW_WIDTH)), full((RW_PAD - 3 * RW_WIDTH - 4 * RW_LORA, RW_WIDTH)), full((1, RW_WIDTH)),
                  full((1, RW_WIDTH)), full((1, RW_WIDTH))],
        out_specs=[ospec] * 10,
        scratch_shapes=[pltpu.VMEM((CHUNK, RW_PAD), F32)],
        compiler_params=_cp(("parallel", "parallel")),
        name="rwkv_pre",
    )(z_rw, z_rw, z_rw, conv_w, w0, wl, a0, al, gl, k_k, k_a, r_k)


def _rwkv_seq_kernel(x0f_ref, x0b_ref, qf_ref, of_ref, gf_ref, df_ref, qb_ref, ob_ref, gb_ref, db_ref,
                     outf_ref, outb_ref, xff_ref, xfb_ref, x_s, *, nsteps):
    c = pl.program_id(1)

    @pl.when(c == 0)
    def _():
        x_s[0] = x0f_ref[...]
        x_s[1] = x0b_ref[...]

    dirs = ((qf_ref, of_ref, gf_ref, df_ref, outf_ref), (qb_ref, ob_ref, gb_ref, db_ref, outb_ref))
    hs = lambda h: slice(h * RW_HEAD, (h + 1) * RW_HEAD)
    for j in range(SEQ_CHUNKS):
        for d, (q_ref, o_ref, g_ref, d_ref, out_ref) in enumerate(dirs):
            jj = SEQ_CHUNKS - 1 - j if d == 1 else j
            rows = slice(jj * CHUNK, (jj + 1) * CHUNK)
            res = [_mm(jnp.concatenate([q_ref[rows, hs(h)], g_ref[rows, hs(h)]], axis=0), x_s[d, h], exact=True)
                   for h in range(RW_HEADS)]
            for h in range(RW_HEADS):
                out_ref[rows, hs(h)] = res[h][:CHUNK] + o_ref[rows, hs(h)]
                x_s[d, h] = res[h][CHUNK:] + d_ref[rows, hs(h)]

    @pl.when(c == nsteps - 1)
    def _():
        xff_ref[...] = x_s[0]
        xfb_ref[...] = x_s[1]


def _rwkv_seq(x0f, x0b, pre, batch, nchunks):
    nsteps = nchunks // SEQ_CHUNKS
    blk = (SEQ_CHUNKS * CHUNK, RW_WIDTH)
    fspec = pl.BlockSpec(blk, lambda b, c: (b * nsteps + c, 0))
    bspec = pl.BlockSpec(blk, lambda b, c: (b * nsteps + nsteps - 1 - c, 0))
    sspec = pl.BlockSpec((None, RW_HEADS, RW_HEAD, RW_HEAD), lambda b, c: (b, 0, 0, 0))
    seq = jax.ShapeDtypeStruct(pre[0].shape, F32)
    state = jax.ShapeDtypeStruct(x0f.shape, F32)
    return pl.pallas_call(
        functools.partial(_rwkv_seq_kernel, nsteps=nsteps),
        out_shape=[seq, seq, state, state],
        grid=(batch, nsteps),
        in_specs=[sspec, sspec] + [fspec] * 4 + [bspec] * 4,
        out_specs=[fspec, bspec, sspec, sspec],
        scratch_shapes=[pltpu.VMEM((2, RW_HEADS, RW_HEAD, RW_HEAD), F32)],
        compiler_params=_cp(("parallel", "arbitrary")),
        name="rwkv_seq",
    )(x0f, x0b, *pre)


def _hgrn_prep(d, q_ref, f_ref, i_ref, lb, rows):
    reverse = d == 1
    row = lax.broadcasted_iota(jnp.int32, (CHUNK, CHUNK), 0)
    col = lax.broadcasted_iota(jnp.int32, (CHUNK, CHUNK), 1)
    incl = (col >= row) if reverse else (col <= row)
    mid = CHUNK // 2 - 1 if reverse else CHUNK // 2
    end = 0 if reverse else CHUNK - 1

    forget = lb + (1.0 - lb) * _sigmoid(f_ref[rows, :])
    kf = 1.0 - forget
    qraw = q_ref[rows, :]
    q = qraw * _sigmoid(qraw)
    v = i_ref[rows, :]
    cum = _mm_mask(jnp.where(incl, 1.0, 0.0), jnp.log(forget))
    ref = cum[mid:mid + 1]
    tot = cum[end:end + 1]
    bf = lambda t: t.astype(BF16)
    return dict(incl=incl, qs=bf(q * jnp.exp(cum - ref)), ks=bf(kf * jnp.exp(ref - cum)), qe=bf(q * jnp.exp(cum)),
                ke=bf(kf * jnp.exp(tot - cum)), v=bf(v), dec=jnp.exp(tot))


def _hgrn_heads(st_s, preps, out_refs, rows):
    for h in range(HG_HEADS):
        sl = slice(h * HG_HEAD, (h + 1) * HG_HEAD)
        for d, (p, out_ref, rw) in enumerate(zip(preps, out_refs, rows)):
            st = st_s[d, h]
            scores = jnp.where(p["incl"], _mm_nt(p["qs"][:, sl], p["ks"][:, sl]), 0.0)
            out_ref[rw, sl] = _mm(scores, p["v"][:, sl]) + _mm_nt(p["qe"][:, sl], st)
            st_s[d, h] = st * p["dec"][:, sl] + _mm_tn(p["v"][:, sl], p["ke"][:, sl])


def _hgrn_kernel(s0f_ref, s0b_ref, qf_ref, ff_ref, if_ref, qb_ref, fb_ref, ib_ref, lb_ref, outf_ref, outb_ref,
                 sff_ref, sfb_ref, st_s, *, nsteps):
    c = pl.program_id(1)

    @pl.when(c == 0)
    def _():
        st_s[0] = s0f_ref[...]
        st_s[1] = s0b_ref[...]

    for j in range(SEQ_CHUNKS):
        jj = SEQ_CHUNKS - 1 - j
        rows = (slice(j * CHUNK, (j + 1) * CHUNK), slice(jj * CHUNK, (jj + 1) * CHUNK))
        preps = (_hgrn_prep(0, qf_ref, ff_ref, if_ref, lb_ref[0:1, :], rows[0]),
                 _hgrn_prep(1, qb_ref, fb_ref, ib_ref, lb_ref[1:2, :], rows[1]))
        _hgrn_heads(st_s, preps, (outf_ref, outb_ref), rows)

    @pl.when(c == nsteps - 1)
    def _():
        sff_ref[...] = st_s[0]
        sfb_ref[...] = st_s[1]


def _hgrn(s0f, s0b, z_hg, lb, batch, nchunks):
    nsteps = nchunks // SEQ_CHUNKS
    blk = (SEQ_CHUNKS * CHUNK, HG_WIDTH)
    fspec = lambda col: pl.BlockSpec(blk, lambda b, c: (b * nsteps + c, col))
    bspec = lambda col: pl.BlockSpec(blk, lambda b, c: (b * nsteps + nsteps - 1 - c, col))
    sspec = pl.BlockSpec((None, HG_HEADS, HG_HEAD, HG_HEAD), lambda b, c: (b, 0, 0, 0))
    seq = jax.ShapeDtypeStruct((z_hg.shape[0], HG_WIDTH), F32)
    state = jax.ShapeDtypeStruct(s0f.shape, F32)
    return pl.pallas_call(
        functools.partial(_hgrn_kernel, nsteps=nsteps),
        out_shape=[seq, seq, state, state],
        grid=(batch, nsteps),
        in_specs=[sspec, sspec, fspec(0), fspec(1), fspec(3), bspec(0), bspec(2), bspec(3),
                  pl.BlockSpec((2, HG_WIDTH), lambda b, c: (0, 0))],
        out_specs=[fspec(0), bspec(0), sspec, sspec],
        scratch_shapes=[pltpu.VMEM((2, HG_HEADS, HG_HEAD, HG_HEAD), F32)],
        compiler_params=_cp(("parallel", "arbitrary")),
        name="hgrn",
    )(s0f, s0b, z_hg, z_hg, z_hg, z_hg, z_hg, z_hg, lb)


def _readout_kernel(x_ref, of_ref, ob_ref, bonus_ref, g_ref, hf_ref, hb_ref, og_ref, grw_ref, ghg_ref, gmx_ref,
                    lnw_ref, lnb_ref, hgn_ref, wrw_ref, whg_ref, wout_ref, o_ref, y_s):
    first = lax.broadcasted_iota(jnp.int32, (1, 2 * RW_HEAD), 1) < RW_HEAD
    tiles = [slice(i * 2 * RW_HEAD, (i + 1) * 2 * RW_HEAD) for i in range(RW_WIDTH // (2 * RW_HEAD))]

    def head_mean(ts):
        s0 = [jnp.sum(jnp.where(first, t, 0.0), axis=-1, keepdims=True) for t in ts]
        s1 = [jnp.sum(jnp.where(first, 0.0, t), axis=-1, keepdims=True) for t in ts]
        return [jnp.where(first, a, b) * (1.0 / RW_HEAD) for a, b in zip(s0, s1)]

    o_rw = [of_ref[:, sl] + ob_ref[:, sl] for sl in tiles]
    dv = [o - mu for o, mu in zip(o_rw, head_mean(o_rw))]
    var = head_mean([d * d for d in dv])
    for sl, d, vr in zip(tiles, dv, var):
        y_s[:, sl] = d * lax.rsqrt(vr + RW_GN_EPS)
    y_rw = (y_s[...] * lnw_ref[...] + lnb_ref[...] + bonus_ref[...]) * g_ref[...]
    branch_rw = _mm(y_rw, wrw_ref[...])

    hg_tiles = [slice(h * HG_HEAD, (h + 1) * HG_HEAD) for h in range(HG_HEADS)]
    o_hg = [hf_ref[:, sl] + hb_ref[:, sl] for sl in hg_tiles]
    ms = [jnp.mean(o * o, axis=-1, keepdims=True) for o in o_hg]
    for sl, o, m2 in zip(hg_tiles, o_hg, ms):
        y_s[:, sl] = o * lax.rsqrt(m2 + NORM_EPS)
    og = og_ref[...]
    y_hg = y_s[...] * hgn_ref[...] * (og * _sigmoid(og))
    branch_hg = _mm(y_hg, whg_ref[...])

    merged = _sigmoid(grw_ref[...]) * branch_rw + _sigmoid(ghg_ref[...]) * branch_hg
    o_ref[...] = x_ref[...] + gmx_ref[...] * _mm(merged, wout_ref[...])


def _readout(x2d, o_f, o_b, bonus, g, h_f, h_b, z_hg, z_g, g_mx, ln_w, ln_b, hg_norm, w_rw, w_hg, w_out,
             rows_per_mod, tm=256):
    m, dm = x2d.shape
    bpm = rows_per_mod // tm
    tok = lambda width, blk=0: pl.BlockSpec((tm, width), lambda i: (i, blk))
    const = lambda shape: pl.BlockSpec(shape, lambda i: (0,) * len(shape), pipeline_mode=pl.Buffered(1))
    return pl.pallas_call(
        _readout_kernel,
        out_shape=jax.ShapeDtypeStruct((m, dm), F32),
        grid=(m // tm,),
        in_specs=[tok(dm), tok(RW_WIDTH), tok(RW_WIDTH), tok(RW_WIDTH), tok(RW_WIDTH), tok(HG_WIDTH), tok(HG_WIDTH),
                  tok(HG_WIDTH, 4), tok(dm, 0), tok(dm, 1),
                  pl.BlockSpec((None, 1, dm), lambda i: (i // bpm, 0, 0)),
                  const((1, RW_WIDTH)), const((1, RW_WIDTH)), const((1, HG_WIDTH)),
                  const((RW_WIDTH, dm)), const((HG_WIDTH, dm)), const((dm, dm))],
        out_specs=tok(dm),
        scratch_shapes=[pltpu.VMEM((tm, RW_WIDTH), F32)],
        compiler_params=_cp(("parallel",)),
        name="readout",
    )(x2d, o_f, o_b, bonus, g, h_f, h_b, z_hg, z_g, z_g, g_mx, ln_w, ln_b, hg_norm, w_rw, w_hg, w_out)


def _cand_groups():
    return [(i, PEER_TOPK // (i + 1)) for i in range(PEER_TOPK)]


def _pick_max(work, io, first_only):
    m = jnp.max(work, axis=0, keepdims=True)
    sel = work == m
    if first_only:
        sel = io == jnp.min(jnp.where(sel, io, work.shape[0]), axis=0, keepdims=True)
    return m, sel


def _topk_ranks(s, vals_ref, first_only):
    io = lax.broadcasted_iota(jnp.int32, s.shape, 0)

    def body(r, carry):
        work, rank = carry
        m, sel = _pick_max(work, io, first_only)
        vals_ref[pl.ds(r, 1), :] = m
        return jnp.where(sel, -jnp.inf, work), jnp.where(sel, lax.convert_element_type(r, F32), rank)

    _, rank = lax.fori_loop(0, PEER_TOPK, body, (s, jnp.full(s.shape, float(PEER_TOPK), F32)))
    taken = jnp.sum(jnp.where(rank < float(PEER_TOPK), 1.0, 0.0), axis=0, keepdims=True)
    return rank, jnp.where(taken == float(PEER_TOPK), 0.0, 1.0)


def _peer_sel_kernel(x_ref, gain_ref, sh_ref, sc_ref, wq_ref, k1_ref, k2_ref, hx_ref, cnt_ref, rk_ref, e1_ref,
                     e2_ref, q_s, v1_s, v2_s):
    hb = _norm_mod(x_ref[...], gain_ref[...], sh_ref[...], sc_ref[...]).astype(BF16)
    hx_ref[...] = hb
    q_s[...] = jnp.dot(hb, wq_ref[...], preferred_element_type=F32)
    tm = q_s.shape[0]
    groups = _cand_groups()
    ncand = sum(n for _, n in groups)
    npad = -ncand % 8

    def select(first_only):
        bad = jnp.zeros((1, tm), F32)
        for hd in range(PEER_HEADS):
            q1 = q_s[:, (2 * hd) * PEER_NKEYS:(2 * hd + 1) * PEER_NKEYS]
            q2 = q_s[:, (2 * hd + 1) * PEER_NKEYS:(2 * hd + 2) * PEER_NKEYS]
            s1 = _mm_nt(k1_ref[...], q1, exact=True)
            s2 = _mm_nt(k2_ref[...], q2, exact=True)
            rank1, bad1 = _topk_ranks(s1, v1_s, first_only)
            rank2, bad2 = _topk_ranks(s2, v2_s, first_only)
            v1 = v1_s[...]
            v2 = v2_s[...]
            rows = [v1[i:i + 1] + v2[0:n] for i, n in groups]
            if npad:
                rows.append(jnp.full((npad, tm), -jnp.inf, F32))
            cand = jnp.concatenate(rows, axis=0)
            io = lax.broadcasted_iota(jnp.int32, cand.shape, 0)
            best0 = v1[0:1] + v2[0:1]

            def body(r, carry):
                work, z = carry
                m, sel = _pick_max(work, io, first_only)
                return jnp.where(sel, -jnp.inf, work), z + jnp.exp(m - best0)

            work, z = lax.fori_loop(0, PEER_TOPK, body, (cand, jnp.zeros((1, tm), F32)))
            taken = jnp.where((work == -jnp.inf) & (io < ncand), 1.0, 0.0)
            bad3 = jnp.where(jnp.sum(taken, axis=0, keepdims=True) == float(PEER_TOPK), 0.0, 1.0)
            bad = jnp.maximum(bad, jnp.maximum(bad1, jnp.maximum(bad2, bad3)))
            cnt1 = jnp.zeros_like(rank1)
            start = 0
            for i, n in groups:
                c_i = jnp.sum(taken[start:start + n], axis=0, keepdims=True)
                cnt1 = cnt1 + jnp.where(rank1 == float(i), c_i, 0.0)
                start += n
            cnt_ref[hd] = cnt1
            rk_ref[hd] = rank2.astype(BF16)
            e1_ref[hd] = jnp.exp(s1 - v1[0:1])
            e2_ref[hd] = (jnp.exp(s2 - v2[0:1]) / z).astype(BF16)
        return bad

    bad = select(False)

    @pl.when(jnp.max(bad) > 0.5)
    def _():
        select(True)


def _peer_sel(x2d, gain, shift, scale, wq, k1, k2, rows_per_mod, tm=256):
    m, dm = x2d.shape
    bpm = rows_per_mod // tm
    mod_map = lambda i: (i // bpm, 0, 0)
    const = lambda shape: pl.BlockSpec(shape, lambda i: (0,) * len(shape))
    sel = jax.ShapeDtypeStruct((PEER_HEADS, PEER_NKEYS, m), F32)
    sel_bf = jax.ShapeDtypeStruct((PEER_HEADS, PEER_NKEYS, m), BF16)
    sel_spec = pl.BlockSpec((PEER_HEADS, PEER_NKEYS, tm), lambda i: (0, 0, i))
    return pl.pallas_call(
        _peer_sel_kernel,
        out_shape=[jax.ShapeDtypeStruct((m, dm), BF16), sel, sel_bf, sel, sel_bf],
        grid=(m // tm,),
        in_specs=[pl.BlockSpec((tm, dm), lambda i: (i, 0)), const((1, dm)),
                  pl.BlockSpec((None, 1, dm), mod_map), pl.BlockSpec((None, 1, dm), mod_map),
                  const(wq.shape), const(k1.shape), const(k2.shape)],
        out_specs=[pl.BlockSpec((tm, dm), lambda i: (i, 0)), sel_spec, sel_spec, sel_spec, sel_spec],
        scratch_shapes=[pltpu.VMEM((tm, dm), F32), pltpu.VMEM((PEER_TOPK, tm), F32),
                        pltpu.VMEM((PEER_TOPK, tm), F32)],
        compiler_params=_cp(("parallel",)),
        name="peer_sel",
    )(x2d, gain, shift, scale, wq, k1, k2)


def _gelu(x):
    return 0.5 * x * (1.0 + lax.erf(x * float(np.sqrt(0.5))))


def _peer_dense_kernel(hx_ref, u_ref, v_ref, cnt_ref, e1_ref, cnt_nxt_ref, e1_nxt_ref, rk_ref, e2_ref, x_ref, gfx_ref,
                       gain_ref, o_ref, wa_s, wb_s, ga_s, gb_s, *, te, nblk):
    e = pl.program_id(1)
    half = te // 2
    nsub = half // PEER_NKEYS
    tm = hx_ref.shape[0]

    def gates(cnt_blk, e1_blk, first, dst):
        packed = (PEER_NKEYS // BF16_ROWS, BF16_ROWS, tm)
        for ii in range(nsub):
            gate = None
            for hd in range(PEER_HEADS):
                cnt = jnp.broadcast_to(cnt_blk[hd, first + ii:first + ii + 1, :], packed[1:]).astype(BF16)
                e1 = jnp.broadcast_to(e1_blk[hd, first + ii:first + ii + 1, :], packed[1:]).astype(BF16)
                t = jnp.where(rk_ref[hd].reshape(packed) < cnt[None], e2_ref[hd].reshape(packed) * e1[None],
                              jnp.zeros((), BF16))
                gate = t if gate is None else gate + t
            dst[ii * PEER_NKEYS:(ii + 1) * PEER_NKEYS, :] = gate.reshape(PEER_NKEYS, tm)

    @pl.when(e == 0)
    def _():
        o_ref[...] = jnp.zeros_like(o_ref)
        gates(cnt_ref, e1_ref, 0, ga_s)

    nt_dims = (((1,), (1,)), ((), ()))
    pieces = tm // MM1_TOKENS

    def weighted(u_rows, g_s, w_s):
        for t in range(pieces):
            ts = slice(t * MM1_TOKENS, (t + 1) * MM1_TOKENS)
            act = _gelu(lax.dot_general(u_ref[u_rows, :], hx_ref[ts, :], nt_dims, preferred_element_type=F32))
            w_s[ts, :] = (g_s[:, ts] * act.astype(BF16)).T

    gates(cnt_ref, e1_ref, nsub, gb_s)
    weighted(slice(0, half), ga_s, wa_s)
    weighted(slice(half, te), gb_s, wb_s)
    o_ref[...] += jnp.dot(wa_s[...], v_ref[:half, :], preferred_element_type=F32)
    gates(cnt_nxt_ref, e1_nxt_ref, 0, ga_s)
    o_ref[...] += jnp.dot(wb_s[...], v_ref[half:, :], preferred_element_type=F32)

    @pl.when(e == nblk - 1)
    def _():
        x = x_ref[...] + gfx_ref[...] * o_ref[...]
        o_ref[...] = x * lax.rsqrt(jnp.mean(x * x, axis=-1, keepdims=True) + NORM_EPS) * gain_ref[...]


def _peer_dense(hx, u, v, cnt, rk, e1, e2, x2d, g_fx, gain, rows_per_mod, tm=512, te=1024):
    m, dm = hx.shape
    ne = u.shape[0]
    nblk = ne // te
    bpm = rows_per_mod // tm
    sel_spec = pl.BlockSpec((PEER_HEADS, PEER_NKEYS, tm), lambda i, e: (0, 0, i))
    tok = pl.BlockSpec((tm, dm), lambda i, e: (i, 0))
    tab = pl.BlockSpec((te, dm), lambda i, e: (e, 0))
    key1_spec = lambda blk: pl.BlockSpec((PEER_HEADS, te // PEER_NKEYS, tm), lambda i, e: (0, blk(e), i))
    return pl.pallas_call(
        functools.partial(_peer_dense_kernel, te=te, nblk=nblk),
        out_shape=jax.ShapeDtypeStruct((m, dm), F32),
        grid=(m // tm, nblk),
        in_specs=[tok, tab, tab, key1_spec(lambda e: e), key1_spec(lambda e: e),
                  key1_spec(lambda e: (e + 1) % nblk), key1_spec(lambda e: (e + 1) % nblk), sel_spec, sel_spec, tok,
                  pl.BlockSpec((None, 1, dm), lambda i, e: (i // bpm, 0, 0)),
                  pl.BlockSpec((1, dm), lambda i, e: (0, 0))],
        out_specs=tok,
        scratch_shapes=[pltpu.VMEM((tm, te // 2), BF16)] * 2 + [pltpu.VMEM((te // 2, tm), BF16)] * 2,
        compiler_params=_cp(("parallel", "arbitrary")),
        name="peer_dense",
    )(hx, u, v, cnt, e1, cnt, e1, rk, e2, x2d, g_fx, gain)


def kernel(x, c, ctx, c_ctx, w_ada, b_ada, norm_mix, norm_ffn, norm_final, w_in, rw_conv, rw_w0, rw_w_lora_b, rw_a0, rw_a_lora_b, rw_g_lora_b, rw_k_k, rw_k_a, rw_r_k, rw_ln_w, rw_ln_b, hg_lb, hg_norm, w_up_rw, w_up_hg, w_out, peer_wq, peer_k1, peer_k2, peer_u, peer_v):
    batch, seq, dm = x.shape
    ctx_len = ctx.shape[1]
    depth = w_in.shape[0]
    assert depth == 1 and dm == D_MODEL
    assert seq % (CHUNK * SEQ_CHUNKS) == 0 and ctx_len % (CHUNK * SEQ_CHUNKS) == 0
    n_x = seq // CHUNK
    n_c = ctx_len // CHUNK
    l = 0

    w_in_p = _regroup(w_in[l])
    conv_w = jnp.pad(rw_conv[l].reshape(9, RW_COLS), ((0, 0), (0, RW_PAD - RW_COLS)))
    lower_bounds = jnp.cumsum(jax.nn.softmax(hg_lb.astype(F32), axis=0), axis=0)[l]
    row = lambda t: t.reshape(1, -1)
    hg_norm_row = jnp.tile(hg_norm[l], HG_HEADS).reshape(1, HG_WIDTH)
    u_bf = peer_u[l].astype(BF16)
    v_bf = peer_v[l].astype(BF16)

    cvec = jnp.concatenate([c, c_ctx[None, :], jnp.zeros((8 - batch - 1, dm), F32)], axis=0)
    mod = _ada(cvec, w_ada[l], row(b_ada[l]))
    mods = mod.reshape(8, N_MOD, 1, dm)
    sh_mx, sc_mx, g_mx, sh_fx, sc_fx, g_fx = (mods[:batch, i] for i in range(N_MOD))
    sh_mc, sc_mc = mods[batch:batch + 1, 0], mods[batch:batch + 1, 1]

    x2d = x.reshape(batch * seq, dm)
    c2d = ctx.reshape(batch * ctx_len, dm)
    gain_mix = row(norm_mix[l])

    zc_rw, zc_hg = _inproj(c2d, gain_mix, sh_mc, sc_mc, w_in_p, (RW_PAD, 5 * HG_WIDTH), batch * ctx_len)
    g_lora = jnp.pad(rw_g_lora_b[l], ((0, RW_PAD - RW_COLS), (0, 0)))
    rw_args = (conv_w, rw_w0[l], rw_w_lora_b[l], rw_a0[l], rw_a_lora_b[l], g_lora, row(rw_k_k[l]),
               row(rw_k_a[l]), row(rw_r_k[l]))
    pre_c = _rwkv_pre(zc_rw, *rw_args, batch=batch, nchunks=n_c, is_ctx=True)
    zero_rw = jnp.zeros((batch, RW_HEADS, RW_HEAD, RW_HEAD), F32)
    zero_hg = jnp.zeros((batch, HG_HEADS, HG_HEAD, HG_HEAD), F32)
    _, _, st_rw_f, st_rw_b = _rwkv_seq(zero_rw, zero_rw, pre_c[0:8], batch, n_c)
    _, _, st_hg_f, st_hg_b = _hgrn(zero_hg, zero_hg, zc_hg, lower_bounds, batch, n_c)

    z_rw, z_hg, z_g = _inproj(x2d, gain_mix, sh_mx, sc_mx, w_in_p, (RW_PAD, 5 * HG_WIDTH, 2 * dm), seq)
    pre = _rwkv_pre(z_rw, *rw_args, batch=batch, nchunks=n_x, is_ctx=False)
    o_f, o_b, _, _ = _rwkv_seq(st_rw_f, st_rw_b, pre[0:8], batch, n_x)
    h_f, h_b, _, _ = _hgrn(st_hg_f, st_hg_b, z_hg, lower_bounds, batch, n_x)
    x1 = _readout(x2d, o_f, o_b, pre[9], pre[8], h_f, h_b, z_hg, z_g, g_mx, row(rw_ln_w[l]), row(rw_ln_b[l]),
                  hg_norm_row, w_up_rw[l].astype(BF16), w_up_hg[l].astype(BF16), w_out[l].astype(BF16), seq)

    hx, cnt, rk, e1, e2 = _peer_sel(x1, row(norm_ffn[l]), sh_fx, sc_fx, peer_wq[l].astype(BF16), peer_k1[l],
                                    peer_k2[l], seq)
    out = _peer_dense(hx, u_bf, v_bf, cnt, rk, e1, e2, x1, g_fx, row(norm_final), seq)
    return out.reshape(batch, seq, dm)
```

```python
import functools

import jax
import jax.numpy as jnp
import numpy as np
from jax import lax
from jax.experimental import pallas as pl
from jax.experimental.pallas import tpu as pltpu

F32 = jnp.float32
BF16 = jnp.bfloat16

D_MODEL = 2048
N_MOD = 6
NORM_EPS = 1e-6
CHUNK = 64
SUB = 16
SEQ_CHUNKS = 4
RW_WIDTH = 1024
RW_HEAD = 64
RW_HEADS = 16
RW_LORA = 64
RW_GATE_RANK = 160
RW_COLS = 3488
RW_PAD = 3584
CONV_COLS = 512
NORM_ROWS = 128
RW_GN_EPS = 64e-5
HG_WIDTH = 1024
HG_HEAD = 128
HG_HEADS = 8
PEER_HEADS = 8
PEER_NKEYS = 128
PEER_TOPK = 16
BF16_ROWS = 16
MM1_TOKENS = 256
EXP_M05 = float(np.exp(-0.5))
VMEM_LIMIT = 56 * 1024 * 1024


def _cp(sem):
    return pltpu.CompilerParams(dimension_semantics=sem, vmem_limit_bytes=VMEM_LIMIT)


def _split(x):
    hi = x.astype(BF16)
    return hi, (x - hi.astype(F32)).astype(BF16)


def _dot(a, b, dims, exact):
    dg = lambda p, q: lax.dot_general(p, q, dims, preferred_element_type=F32)
    if not exact:
        return dg(a.astype(BF16), b.astype(BF16))
    a_hi, a_lo = _split(a)
    b_hi, b_lo = _split(b)
    return dg(a_hi, b_hi) + (dg(a_hi, b_lo) + dg(a_lo, b_hi))


def _mm(a, b, exact=False):
    return _dot(a, b, (((1,), (0,)), ((), ())), exact)


def _mm_nt(a, b, exact=False):
    return _dot(a, b, (((1,), (1,)), ((), ())), exact)


def _mm_tn(a, b, exact=False):
    return _dot(a, b, (((0,), (0,)), ((), ())), exact)


def _mm_mask(mask01, x):
    m = mask01.astype(BF16)
    hi = x.astype(BF16)
    mid, lo = _split(x - hi.astype(F32))
    dg = lambda q: jnp.dot(m, q, preferred_element_type=F32)
    return dg(hi) + (dg(mid) + dg(lo))


def _sigmoid(x):
    return 1.0 / (1.0 + jnp.exp(-x))


def _norm_mod(x, gain, shift, scale):
    y = x * lax.rsqrt(jnp.mean(x * x, axis=-1, keepdims=True) + NORM_EPS) * gain
    return y * (1.0 + scale) + shift


def _ada_kernel(c_ref, w_ref, b_ref, o_ref):
    cv = c_ref[...]
    o_ref[...] = _mm(cv * _sigmoid(cv), w_ref[...], exact=True) + b_ref[...]


def _ada(cvec, w, b, tn=512):
    k, n = w.shape
    return pl.pallas_call(
        _ada_kernel,
        out_shape=jax.ShapeDtypeStruct((cvec.shape[0], n), F32),
        grid=(n // tn,),
        in_specs=[pl.BlockSpec((cvec.shape[0], k), lambda j: (0, 0)),
                  pl.BlockSpec((k, tn), lambda j: (0, j)),
                  pl.BlockSpec((1, tn), lambda j: (0, j))],
        out_specs=pl.BlockSpec((cvec.shape[0], tn), lambda j: (0, j)),
        compiler_params=_cp(("parallel",)),
        name="ada",
    )(cvec, w, b)


def _regroup_kernel(w_ref, o_ref):
    w = w_ref[...]
    pad = jnp.zeros((w.shape[0], RW_PAD - RW_COLS), F32)
    o_ref[...] = jnp.concatenate([w[:, :RW_COLS], pad, w[:, RW_COLS:]], axis=1).astype(BF16)


def _regroup(w, layer, tk=128):
    _, k, n = w.shape
    n_out = n + RW_PAD - RW_COLS
    return pl.pallas_call(
        _regroup_kernel,
        out_shape=jax.ShapeDtypeStruct((k, n_out), BF16),
        grid=(k // tk,),
        in_specs=[pl.BlockSpec((None, tk, n), lambda i: (layer, i, 0))],
        out_specs=pl.BlockSpec((tk, n_out), lambda i: (i, 0)),
        compiler_params=_cp(("parallel",)),
        name="regroup",
    )(w)


def _inproj_kernel(x_ref, g_ref, sh_ref, sc_ref, w_ref, *refs, bounds):
    out_refs, h_scr = refs[:-1], refs[-1]
    j = pl.program_id(1)

    @pl.when(j == 0)
    def _():
        for r in range(0, h_scr.shape[0], NORM_ROWS):
            rs = slice(r, r + NORM_ROWS)
            h_scr[rs, :] = _norm_mod(x_ref[rs, :], g_ref[...], sh_ref[...], sc_ref[...]).astype(BF16)

    z = jnp.dot(h_scr[...], w_ref[...], preferred_element_type=F32).astype(BF16)
    for (lo, hi), o_ref in zip(bounds, out_refs):
        @pl.when((j >= lo) & (j < hi))
        def _(o_ref=o_ref):
            o_ref[...] = z


def _inproj(x2d, gain, shift, scale, w, widths, rows_per_mod, tm=2048, tn=512):
    m, k = x2d.shape
    tm = min(tm, rows_per_mod)
    assert rows_per_mod % tm == 0 and m % tm == 0 and all(wd % tn == 0 for wd in widths)
    bpm = rows_per_mod // tm
    mod_map = lambda i, j: (i // bpm, 0, 0)
    edges = np.cumsum([0] + [wd // tn for wd in widths])
    bounds = [(int(lo), int(hi)) for lo, hi in zip(edges[:-1], edges[1:])]
    out_spec = lambda lo, hi: pl.BlockSpec((tm, tn), lambda i, j: (i, jnp.clip(j - lo, 0, hi - lo - 1)))
    return pl.pallas_call(
        functools.partial(_inproj_kernel, bounds=bounds),
        out_shape=[jax.ShapeDtypeStruct((m, wd), BF16) for wd in widths],
        grid=(m // tm, int(edges[-1])),
        in_specs=[pl.BlockSpec((tm, k), lambda i, j: (i, 0), pipeline_mode=pl.Buffered(1)),
                  pl.BlockSpec((1, k), lambda i, j: (0, 0)),
                  pl.BlockSpec((None, 1, k), mod_map),
                  pl.BlockSpec((None, 1, k), mod_map),
                  pl.BlockSpec((k, tn), lambda i, j: (0, j))],
        out_specs=[out_spec(lo, hi) for lo, hi in bounds],
        scratch_shapes=[pltpu.VMEM((tm, k), BF16)],
        compiler_params=_cp(("parallel", "arbitrary")),
        name="inproj",
    )(x2d, gain, shift, scale, w)


def _tri_masks(reverse):
    row = lax.broadcasted_iota(jnp.int32, (CHUNK, CHUNK), 0)
    col = lax.broadcasted_iota(jnp.int32, (CHUNK, CHUNK), 1)
    if reverse:
        strict, incl = col > row, col >= row
    else:
        strict, incl = col < row, col <= row
    same = (row // SUB) == (col // SUB)
    return strict, incl, same, row == col


def _rw_heads(ar_t, bk_t, a_0, r_0, bk_end, v_b, decay_tot, reverse, out_refs):
    strict, incl, same, eye = _tri_masks(reverse)
    row2 = lax.broadcasted_iota(jnp.int32, (CHUNK, 2 * CHUNK), 0)
    col2 = lax.broadcasted_iota(jnp.int32, (CHUNK, 2 * CHUNK), 1) % CHUNK
    incl2 = (col2 >= row2) if reverse else (col2 <= row2)
    ident = jnp.where(eye, 1.0, 0.0)
    heads = range(RW_HEADS)
    hs = lambda h: slice(h * RW_HEAD, (h + 1) * RW_HEAD)
    bf = lambda xs: [x.astype(BF16) for x in xs]
    mm = lambda xs, ys: [jnp.dot(x, y, preferred_element_type=F32) for x, y in zip(xs, ys)]
    nt_dims = (((1,), (1,)), ((), ()))
    tn_dims = (((0,), (0,)), ((), ()))

    s4 = [lax.dot_general(ar_t[:, hs(h)], bk_t[:, hs(h)], nt_dims, preferred_element_type=F32) for h in heads]
    m_ab = [jnp.where(strict, s[:CHUNK, :CHUNK], 0.0) for s in s4]
    m_ak = bf([jnp.where(strict, s[:CHUNK, CHUNK:], 0.0) for s in s4])
    m_r = bf([jnp.where(incl2, s[CHUNK:], 0.0) for s in s4])

    m_bd = [jnp.where(same, m, 0.0) for m in m_ab]
    m_off = bf([jnp.where(same, 0.0, m) for m in m_ab])
    m1 = bf(m_bd)
    m2 = bf(mm(m1, m1))
    m4 = bf(mm(m2, m2))
    m8 = bf(mm(m4, m4))
    p = [ident + m for m in m_bd]
    for mk in (m2, m4, m8):
        p = [x + y for x, y in zip(p, mm(bf(p), mk))]
    p_b = bf(p)
    n1 = mm(p_b, m_off)
    n1_b = bf(n1)
    n2 = mm(n1_b, n1_b)
    n3 = mm(n1_b, bf(n2))
    t_inv = bf(mm(bf([ident + a + b + c for a, b, c in zip(n1, n2, n3)]), p_b))

    x = bf(mm(m_ak, [v_b[:, hs(h)] for h in heads]))
    wu = bf(mm(t_inv, [jnp.concatenate([a_0[:, hs(h)], xh], axis=1) for h, xh in zip(heads, x)]))
    zero = jnp.zeros((CHUNK, RW_HEAD), BF16)
    rhs = [jnp.concatenate([w, jnp.concatenate([zero, v_b[:, hs(h)]], axis=1)], axis=0) for h, w in zip(heads, wu)]
    qo = mm(m_r, rhs)
    gd = [lax.dot_general(bk_end[:, hs(h)], r, tn_dims, preferred_element_type=F32) for h, r in zip(heads, rhs)]
    q_ref, o_ref, g_ref, d_ref = out_refs
    for h in heads:
        q_ref[:, hs(h)] = r_0[:, hs(h)] + qo[h][:, :RW_HEAD]
        o_ref[:, hs(h)] = qo[h][:, RW_HEAD:]
        g_ref[:, hs(h)] = gd[h][:, :RW_HEAD] + jnp.where(eye, decay_tot[:, hs(h)], 0.0)
        d_ref[:, hs(h)] = gd[h][:, RW_HEAD:]


def _rwkv_pre_kernel(zp_ref, zc_ref, zn_ref, conv_ref, w0_ref, wl_ref, a0_ref, al_ref, gl_ref, kk_ref, ka_ref,
                     rk_ref, qf_ref, of_ref, gf_ref, df_ref, qb_ref, ob_ref, gb_ref, db_ref, g_ref, bonus_ref,
                     conv_s, *, is_ctx, nchunks):
    c = pl.program_id(1)
    row = lax.broadcasted_iota(jnp.int32, (CHUNK, 1), 0)
    has_p = (c > 0).astype(F32)
    has_n = (c < nchunks - 1).astype(F32)

    def left(z, fill):
        return jnp.where(row == 0, fill, pltpu.roll(z, 1, 0))

    def right(z, fill):
        return jnp.where(row == CHUNK - 1, fill, pltpu.roll(z, CHUNK - 1, 0))

    for cb in range(RW_PAD // CONV_COLS):
        cs = slice(cb * CONV_COLS, (cb + 1) * CONV_COLS)
        w = conv_ref[:, cs]
        zc = zc_ref[:, cs].astype(F32)
        if is_ctx:
            prev_row = zp_ref[CHUNK - BF16_ROWS:CHUNK, cs].astype(F32)[BF16_ROWS - 1:BF16_ROWS] * has_p
            next_row = zn_ref[0:BF16_ROWS, cs].astype(F32)[0:1] * has_n
            conv = left(zc, prev_row) * w[3:4] + zc * w[4:5] + right(zc, next_row) * w[5:6]
        else:
            zp = zp_ref[:, cs].astype(F32) * has_p
            zn = zn_ref[:, cs].astype(F32) * has_n
            col = lambda c: zp * w[c:c + 1] + zc * w[3 + c:4 + c] + zn * w[6 + c:7 + c]
            conv = left(col(0), 0.0) + col(1) + right(col(2), 0.0)
        conv_s[:, cs] = conv

    r = conv_s[:, 0:RW_WIDTH]
    k = conv_s[:, RW_WIDTH:2 * RW_WIDTH]
    v = conv_s[:, 2 * RW_WIDTH:3 * RW_WIDTH]
    lo = conv_s[:, 3 * RW_WIDTH:RW_PAD]
    g_ref[...] = _mm(_sigmoid(lo[:, 4 * RW_LORA:]), gl_ref[...], exact=True)

    first = lax.broadcasted_iota(jnp.int32, (1, 2 * RW_HEAD), 1) < RW_HEAD

    def head_sum(t):
        tiles = []
        for i in range(RW_WIDTH // (2 * RW_HEAD)):
            ti = t[:, i * 2 * RW_HEAD:(i + 1) * 2 * RW_HEAD]
            s0 = jnp.sum(jnp.where(first, ti, 0.0), axis=-1, keepdims=True)
            s1 = jnp.sum(jnp.where(first, 0.0, ti), axis=-1, keepdims=True)
            tiles.append(jnp.where(first, s0, s1))
        return jnp.concatenate(tiles, axis=1)

    kkr = k * kk_ref[...]
    kk = kkr / jnp.maximum(jnp.sqrt(head_sum(kkr * kkr)), 1e-12)
    v_b = v.astype(BF16)
    outs = ((qf_ref, of_ref, gf_ref, df_ref), (qb_ref, ob_ref, gb_ref, db_ref))
    trow = lax.broadcasted_iota(jnp.int32, (CHUNK, CHUNK), 0)
    tcol = lax.broadcasted_iota(jnp.int32, (CHUNK, CHUNK), 1)
    bonus = None
    for d in range(2):
        wa = lo[:, d * RW_LORA:(d + 1) * RW_LORA]
        aa = lo[:, (2 + d) * RW_LORA:(3 + d) * RW_LORA]
        w_pre = w0_ref[d:d + 1, :] + _mm(jnp.tanh(wa), wl_ref[d], exact=True)
        logw = -EXP_M05 * _sigmoid(w_pre)
        a = _sigmoid(a0_ref[d:d + 1, :] + _mm(aa, al_ref[d], exact=True))
        tri = jnp.where((tcol >= trow) if d == 1 else (tcol <= trow), 1.0, 0.0)
        cum = _mm_mask(tri, logw)
        prev = cum - logw
        mid = CHUNK // 2 - 1 if d == 1 else CHUNK // 2
        end = 0 if d == 1 else CHUNK - 1
        rho = cum[mid:mid + 1]
        tot = cum[end:end + 1]
        kd = k * (1.0 + (a - 1.0) * ka_ref[...])
        kka = kk * a
        hb = head_sum(r * kd * rk_ref[...])
        bonus = hb if bonus is None else bonus + hb
        e_out = jnp.exp(rho - cum)
        e_end = jnp.exp(tot - cum)
        ar_t = jnp.concatenate([-kk * jnp.exp(prev - rho), r * jnp.exp(cum - rho)], axis=0).astype(BF16)
        bk_t = jnp.concatenate([kka * e_out, kd * e_out], axis=0).astype(BF16)
        bk_end = jnp.concatenate([kka * e_end, kd * e_end], axis=0).astype(BF16)
        a_0 = (-kk * jnp.exp(prev)).astype(BF16)
        r_0 = r * jnp.exp(cum)
        _rw_heads(ar_t, bk_t, a_0, r_0, bk_end, v_b, jnp.exp(tot), d == 1, outs[d])
    bonus_ref[...] = bonus * v


def _rwkv_pre(z_rw, conv_w, w0, wl, a0, al, gl, k_k, k_a, r_k, batch, nchunks, is_ctx):
    rows = z_rw.shape[0]
    last = nchunks - 1
    zspec = lambda f: pl.BlockSpec((CHUNK, RW_PAD), lambda b, c: (b * nchunks + f(c), 0))
    full = lambda shape: pl.BlockSpec(shape, lambda b, c: (0,) * len(shape))
    ospec = pl.BlockSpec((CHUNK, RW_WIDTH), lambda b, c: (b * nchunks + c, 0))
    out = jax.ShapeDtypeStruct((rows, RW_WIDTH), F32)
    return pl.pallas_call(
        functools.partial(_rwkv_pre_kernel, is_ctx=is_ctx, nchunks=nchunks),
        out_shape=[out] * 10,
        grid=(batch, nchunks),
        in_specs=[zspec(lambda c: jnp.maximum(c - 1, 0)), zspec(lambda c: c), zspec(lambda c: jnp.minimum(c + 1, last)),
                  full((9, RW_PAD)), full((2, RW_WIDTH)), full((2, RW_LORA, RW_WIDTH)), full((2, RW_WIDTH)),
                  full((2, RW_LORA, RW_WIDTH)), full((RW_PAD - 3 * RW_WIDTH - 4 * RW_LORA, RW_WIDTH)), full((1, RW_WIDTH)),
                  full((1, RW_WIDTH)), full((1, RW_WIDTH))],
        out_specs=[ospec] * 10,
        scratch_shapes=[pltpu.VMEM((CHUNK, RW_PAD), F32)],
        compiler_params=_cp(("parallel", "parallel")),
        name="rwkv_pre",
    )(z_rw, z_rw, z_rw, conv_w, w0, wl, a0, al, gl, k_k, k_a, r_k)


def _rwkv_seq_kernel(x0f_ref, x0b_ref, qf_ref, of_ref, gf_ref, df_ref, qb_ref, ob_ref, gb_ref, db_ref,
                     outf_ref, outb_ref, xff_ref, xfb_ref, x_s, *, nsteps):
    c = pl.program_id(1)

    @pl.when(c == 0)
    def _():
        x_s[0] = x0f_ref[...]
        x_s[1] = x0b_ref[...]

    dirs = ((qf_ref, of_ref, gf_ref, df_ref, outf_ref), (qb_ref, ob_ref, gb_ref, db_ref, outb_ref))
    hs = lambda h: slice(h * RW_HEAD, (h + 1) * RW_HEAD)
    for j in range(SEQ_CHUNKS):
        for d, (q_ref, o_ref, g_ref, d_ref, out_ref) in enumerate(dirs):
            jj = SEQ_CHUNKS - 1 - j if d == 1 else j
            rows = slice(jj * CHUNK, (jj + 1) * CHUNK)
            res = [_mm(jnp.concatenate([q_ref[rows, hs(h)], g_ref[rows, hs(h)]], axis=0), x_s[d, h], exact=True)
                   for h in range(RW_HEADS)]
            for h in range(RW_HEADS):
                out_ref[rows, hs(h)] = res[h][:CHUNK] + o_ref[rows, hs(h)]
                x_s[d, h] = res[h][CHUNK:] + d_ref[rows, hs(h)]

    @pl.when(c == nsteps - 1)
    def _():
        xff_ref[...] = x_s[0]
        xfb_ref[...] = x_s[1]


def _rwkv_seq(x0f, x0b, pre, batch, nchunks):
    nsteps = nchunks // SEQ_CHUNKS
    blk = (SEQ_CHUNKS * CHUNK, RW_WIDTH)
    fspec = pl.BlockSpec(blk, lambda b, c: (b * nsteps + c, 0))
    bspec = pl.BlockSpec(blk, lambda b, c: (b * nsteps + nsteps - 1 - c, 0))
    sspec = pl.BlockSpec((None, RW_HEADS, RW_HEAD, RW_HEAD), lambda b, c: (b, 0, 0, 0))
    seq = jax.ShapeDtypeStruct(pre[0].shape, F32)
    state = jax.ShapeDtypeStruct(x0f.shape, F32)
    return pl.pallas_call(
        functools.partial(_rwkv_seq_kernel, nsteps=nsteps),
        out_shape=[seq, seq, state, state],
        grid=(batch, nsteps),
        in_specs=[sspec, sspec] + [fspec] * 4 + [bspec] * 4,
        out_specs=[fspec, bspec, sspec, sspec],
        scratch_shapes=[pltpu.VMEM((2, RW_HEADS, RW_HEAD, RW_HEAD), F32)],
        compiler_params=_cp(("parallel", "arbitrary")),
        name="rwkv_seq",
    )(x0f, x0b, *pre)


def _hgrn_prep(d, q_ref, f_ref, i_ref, lb, rows):
    reverse = d == 1
    row = lax.broadcasted_iota(jnp.int32, (CHUNK, CHUNK), 0)
    col = lax.broadcasted_iota(jnp.int32, (CHUNK, CHUNK), 1)
    incl = (col >= row) if reverse else (col <= row)
    mid = CHUNK // 2 - 1 if reverse else CHUNK // 2
    end = 0 if reverse else CHUNK - 1

    forget = lb + (1.0 - lb) * _sigmoid(f_ref[rows, :].astype(F32))
    kf = 1.0 - forget
    qraw = q_ref[rows, :].astype(F32)
    q = qraw * _sigmoid(qraw)
    v = i_ref[rows, :].astype(F32)
    cum = _mm_mask(jnp.where(incl, 1.0, 0.0), jnp.log(forget))
    ref = cum[mid:mid + 1]
    tot = cum[end:end + 1]
    bf = lambda t: t.astype(BF16)
    return dict(incl=incl, qs=bf(q * jnp.exp(cum - ref)), ks=bf(kf * jnp.exp(ref - cum)), qe=bf(q * jnp.exp(cum)),
                ke=bf(kf * jnp.exp(tot - cum)), v=bf(v), dec=jnp.exp(tot))


def _hgrn_heads(st_s, preps, out_refs, rows):
    for h in range(HG_HEADS):
        sl = slice(h * HG_HEAD, (h + 1) * HG_HEAD)
        for d, (p, out_ref, rw) in enumerate(zip(preps, out_refs, rows)):
            st = st_s[d, h]
            scores = jnp.where(p["incl"], _mm_nt(p["qs"][:, sl], p["ks"][:, sl]), 0.0)
            out_ref[rw, sl] = _mm(scores, p["v"][:, sl]) + _mm_nt(p["qe"][:, sl], st)
            st_s[d, h] = st * p["dec"][:, sl] + _mm_tn(p["v"][:, sl], p["ke"][:, sl])


def _hgrn_kernel(s0f_ref, s0b_ref, qf_ref, ff_ref, if_ref, qb_ref, fb_ref, ib_ref, lb_ref, outf_ref, outb_ref,
                 sff_ref, sfb_ref, st_s, *, nsteps):
    c = pl.program_id(1)

    @pl.when(c == 0)
    def _():
        st_s[0] = s0f_ref[...]
        st_s[1] = s0b_ref[...]

    for j in range(SEQ_CHUNKS):
        jj = SEQ_CHUNKS - 1 - j
        rows = (slice(j * CHUNK, (j + 1) * CHUNK), slice(jj * CHUNK, (jj + 1) * CHUNK))
        preps = (_hgrn_prep(0, qf_ref, ff_ref, if_ref, lb_ref[0:1, :], rows[0]),
                 _hgrn_prep(1, qb_ref, fb_ref, ib_ref, lb_ref[1:2, :], rows[1]))
        _hgrn_heads(st_s, preps, (outf_ref, outb_ref), rows)

    @pl.when(c == nsteps - 1)
    def _():
        sff_ref[...] = st_s[0]
        sfb_ref[...] = st_s[1]


def _hgrn(s0f, s0b, z_hg, lb, batch, nchunks):
    nsteps = nchunks // SEQ_CHUNKS
    blk = (SEQ_CHUNKS * CHUNK, HG_WIDTH)
    fspec = lambda col: pl.BlockSpec(blk, lambda b, c: (b * nsteps + c, col))
    bspec = lambda col: pl.BlockSpec(blk, lambda b, c: (b * nsteps + nsteps - 1 - c, col))
    sspec = pl.BlockSpec((None, HG_HEADS, HG_HEAD, HG_HEAD), lambda b, c: (b, 0, 0, 0))
    seq = jax.ShapeDtypeStruct((z_hg.shape[0], HG_WIDTH), F32)
    state = jax.ShapeDtypeStruct(s0f.shape, F32)
    return pl.pallas_call(
        functools.partial(_hgrn_kernel, nsteps=nsteps),
        out_shape=[seq, seq, state, state],
        grid=(batch, nsteps),
        in_specs=[sspec, sspec, fspec(0), fspec(1), fspec(3), bspec(0), bspec(2), bspec(3),
                  pl.BlockSpec((2, HG_WIDTH), lambda b, c: (0, 0))],
        out_specs=[fspec(0), bspec(0), sspec, sspec],
        scratch_shapes=[pltpu.VMEM((2, HG_HEADS, HG_HEAD, HG_HEAD), F32)],
        compiler_params=_cp(("parallel", "arbitrary")),
        name="hgrn",
    )(s0f, s0b, z_hg, z_hg, z_hg, z_hg, z_hg, z_hg, lb)


def _readout_kernel(x_ref, of_ref, ob_ref, bonus_ref, g_ref, hf_ref, hb_ref, og_ref, grw_ref, ghg_ref, gmx_ref,
                    lnw_ref, lnb_ref, hgn_ref, wrw_ref, whg_ref, wout_ref, o_ref, y_s):
    first = lax.broadcasted_iota(jnp.int32, (1, 2 * RW_HEAD), 1) < RW_HEAD
    tiles = [slice(i * 2 * RW_HEAD, (i + 1) * 2 * RW_HEAD) for i in range(RW_WIDTH // (2 * RW_HEAD))]

    def head_mean(ts):
        s0 = [jnp.sum(jnp.where(first, t, 0.0), axis=-1, keepdims=True) for t in ts]
        s1 = [jnp.sum(jnp.where(first, 0.0, t), axis=-1, keepdims=True) for t in ts]
        return [jnp.where(first, a, b) * (1.0 / RW_HEAD) for a, b in zip(s0, s1)]

    o_rw = [of_ref[:, sl] + ob_ref[:, sl] for sl in tiles]
    dv = [o - mu for o, mu in zip(o_rw, head_mean(o_rw))]
    var = head_mean([d * d for d in dv])
    for sl, d, vr in zip(tiles, dv, var):
        y_s[:, sl] = d * lax.rsqrt(vr + RW_GN_EPS)
    y_rw = (y_s[...] * lnw_ref[...] + lnb_ref[...] + bonus_ref[...]) * g_ref[...]
    branch_rw = _mm(y_rw, wrw_ref[...])

    hg_tiles = [slice(h * HG_HEAD, (h + 1) * HG_HEAD) for h in range(HG_HEADS)]
    o_hg = [hf_ref[:, sl] + hb_ref[:, sl] for sl in hg_tiles]
    ms = [jnp.mean(o * o, axis=-1, keepdims=True) for o in o_hg]
    for sl, o, m2 in zip(hg_tiles, o_hg, ms):
        y_s[:, sl] = o * lax.rsqrt(m2 + NORM_EPS)
    og = og_ref[...].astype(F32)
    y_hg = y_s[...] * hgn_ref[...] * (og * _sigmoid(og))
    branch_hg = _mm(y_hg, whg_ref[...])

    merged = _sigmoid(grw_ref[...].astype(F32)) * branch_rw + _sigmoid(ghg_ref[...].astype(F32)) * branch_hg
    o_ref[...] = x_ref[...] + gmx_ref[...] * _mm(merged, wout_ref[...])


def _readout(x2d, o_f, o_b, bonus, g, h_f, h_b, z_hg, z_g, g_mx, ln_w, ln_b, hg_norm, w_rw, w_hg, w_out,
             rows_per_mod, tm=256):
    m, dm = x2d.shape
    bpm = rows_per_mod // tm
    tok = lambda width, blk=0: pl.BlockSpec((tm, width), lambda i: (i, blk))
    const = lambda shape: pl.BlockSpec(shape, lambda i: (0,) * len(shape), pipeline_mode=pl.Buffered(1))
    return pl.pallas_call(
        _readout_kernel,
        out_shape=jax.ShapeDtypeStruct((m, dm), F32),
        grid=(m // tm,),
        in_specs=[tok(dm), tok(RW_WIDTH), tok(RW_WIDTH), tok(RW_WIDTH), tok(RW_WIDTH), tok(HG_WIDTH), tok(HG_WIDTH),
                  tok(HG_WIDTH, 4), tok(dm, 0), tok(dm, 1),
                  pl.BlockSpec((None, 1, dm), lambda i: (i // bpm, 0, 0)),
                  const((1, RW_WIDTH)), const((1, RW_WIDTH)), const((1, HG_WIDTH)),
                  const((RW_WIDTH, dm)), const((HG_WIDTH, dm)), const((dm, dm))],
        out_specs=tok(dm),
        scratch_shapes=[pltpu.VMEM((tm, RW_WIDTH), F32)],
        compiler_params=_cp(("parallel",)),
        name="readout",
    )(x2d, o_f, o_b, bonus, g, h_f, h_b, z_hg, z_g, z_g, g_mx, ln_w, ln_b, hg_norm, w_rw, w_hg, w_out)


def _cand_groups():
    return [(i, PEER_TOPK // (i + 1)) for i in range(PEER_TOPK)]


def _pick_max(work, io, first_only):
    m = jnp.max(work, axis=0, keepdims=True)
    sel = work == m
    if first_only:
        sel = io == jnp.min(jnp.where(sel, io, work.shape[0]), axis=0, keepdims=True)
    return m, sel


def _topk_ranks(s, vals_ref, first_only):
    io = lax.broadcasted_iota(jnp.int32, s.shape, 0)

    def body(r, carry):
        work, rank = carry
        m, sel = _pick_max(work, io, first_only)
        vals_ref[pl.ds(r, 1), :] = m
        return jnp.where(sel, -jnp.inf, work), jnp.where(sel, lax.convert_element_type(r, F32), rank)

    _, rank = lax.fori_loop(0, PEER_TOPK, body, (s, jnp.full(s.shape, float(PEER_TOPK), F32)))
    taken = jnp.sum(jnp.where(rank < float(PEER_TOPK), 1.0, 0.0), axis=0, keepdims=True)
    return rank, jnp.where(taken == float(PEER_TOPK), 0.0, 1.0)


def _peer_sel_kernel(x_ref, gain_ref, sh_ref, sc_ref, wq_ref, k1_ref, k2_ref, hx_ref, cnt_ref, rk_ref, e1_ref,
                     e2_ref, q_s, v1_s, v2_s):
    hb = _norm_mod(x_ref[...], gain_ref[...], sh_ref[...], sc_ref[...]).astype(BF16)
    hx_ref[...] = hb
    q_s[...] = jnp.dot(hb, wq_ref[...], preferred_element_type=F32)
    tm = q_s.shape[0]
    groups = _cand_groups()
    ncand = sum(n for _, n in groups)
    npad = -ncand % 8

    def select(first_only):
        bad = jnp.zeros((1, tm), F32)
        for hd in range(PEER_HEADS):
            q1 = q_s[:, (2 * hd) * PEER_NKEYS:(2 * hd + 1) * PEER_NKEYS]
            q2 = q_s[:, (2 * hd + 1) * PEER_NKEYS:(2 * hd + 2) * PEER_NKEYS]
            s1 = _mm_nt(k1_ref[...], q1, exact=True)
            s2 = _mm_nt(k2_ref[...], q2, exact=True)
            rank1, bad1 = _topk_ranks(s1, v1_s, first_only)
            rank2, bad2 = _topk_ranks(s2, v2_s, first_only)
            v1 = v1_s[...]
            v2 = v2_s[...]
            rows = [v1[i:i + 1] + v2[0:n] for i, n in groups]
            if npad:
                rows.append(jnp.full((npad, tm), -jnp.inf, F32))
            cand = jnp.concatenate(rows, axis=0)
            io = lax.broadcasted_iota(jnp.int32, cand.shape, 0)
            best0 = v1[0:1] + v2[0:1]

            def body(r, carry):
                work, z = carry
                m, sel = _pick_max(work, io, first_only)
                return jnp.where(sel, -jnp.inf, work), z + jnp.exp(m - best0)

            work, z = lax.fori_loop(0, PEER_TOPK, body, (cand, jnp.zeros((1, tm), F32)))
            taken = jnp.where((work == -jnp.inf) & (io < ncand), 1.0, 0.0)
            bad3 = jnp.where(jnp.sum(taken, axis=0, keepdims=True) == float(PEER_TOPK), 0.0, 1.0)
            bad = jnp.maximum(bad, jnp.maximum(bad1, jnp.maximum(bad2, bad3)))
            cnt1 = jnp.zeros_like(rank1)
            start = 0
            for i, n in groups:
                c_i = jnp.sum(taken[start:start + n], axis=0, keepdims=True)
                cnt1 = cnt1 + jnp.where(rank1 == float(i), c_i, 0.0)
                start += n
            cnt_ref[hd] = cnt1
            rk_ref[hd] = rank2.astype(BF16)
            e1_ref[hd] = jnp.exp(s1 - v1[0:1])
            e2_ref[hd] = (jnp.exp(s2 - v2[0:1]) / z).astype(BF16)
        return bad

    bad = select(False)

    @pl.when(jnp.max(bad) > 0.5)
    def _():
        select(True)


def _peer_sel(x2d, gain, shift, scale, wq, k1, k2, rows_per_mod, tm=256):
    m, dm = x2d.shape
    bpm = rows_per_mod // tm
    mod_map = lambda i: (i // bpm, 0, 0)
    const = lambda shape: pl.BlockSpec(shape, lambda i: (0,) * len(shape))
    sel = jax.ShapeDtypeStruct((PEER_HEADS, PEER_NKEYS, m), F32)
    sel_bf = jax.ShapeDtypeStruct((PEER_HEADS, PEER_NKEYS, m), BF16)
    sel_spec = pl.BlockSpec((PEER_HEADS, PEER_NKEYS, tm), lambda i: (0, 0, i))
    return pl.pallas_call(
        _peer_sel_kernel,
        out_shape=[jax.ShapeDtypeStruct((m, dm), BF16), sel, sel_bf, sel, sel_bf],
        grid=(m // tm,),
        in_specs=[pl.BlockSpec((tm, dm), lambda i: (i, 0)), const((1, dm)),
                  pl.BlockSpec((None, 1, dm), mod_map), pl.BlockSpec((None, 1, dm), mod_map),
                  const(wq.shape), const(k1.shape), const(k2.shape)],
        out_specs=[pl.BlockSpec((tm, dm), lambda i: (i, 0)), sel_spec, sel_spec, sel_spec, sel_spec],
        scratch_shapes=[pltpu.VMEM((tm, dm), F32), pltpu.VMEM((PEER_TOPK, tm), F32),
                        pltpu.VMEM((PEER_TOPK, tm), F32)],
        compiler_params=_cp(("parallel",)),
        name="peer_sel",
    )(x2d, gain, shift, scale, wq, k1, k2)


def _gelu(x):
    return 0.5 * x * (1.0 + lax.erf(x * float(np.sqrt(0.5))))


def _peer_dense_kernel(hx_ref, u_ref, v_ref, cnt_ref, e1_ref, cnt_nxt_ref, e1_nxt_ref, rk_ref, e2_ref, x_ref, gfx_ref,
                       gain_ref, o_ref, w_s, ga_s, gb_s, *, te, nblk):
    e = pl.program_id(1)
    half = te // 2
    nsub = half // PEER_NKEYS
    tm = hx_ref.shape[0]

    def gates(cnt_blk, e1_blk, first, dst):
        packed = (PEER_NKEYS // BF16_ROWS, BF16_ROWS, tm)
        for ii in range(nsub):
            gate = None
            for hd in range(PEER_HEADS):
                cnt = jnp.broadcast_to(cnt_blk[hd, first + ii:first + ii + 1, :], packed[1:]).astype(BF16)
                e1 = jnp.broadcast_to(e1_blk[hd, first + ii:first + ii + 1, :], packed[1:]).astype(BF16)
                t = jnp.where(rk_ref[hd].reshape(packed) < cnt[None], e2_ref[hd].reshape(packed) * e1[None],
                              jnp.zeros((), BF16))
                gate = t if gate is None else gate + t
            dst[ii * PEER_NKEYS:(ii + 1) * PEER_NKEYS, :] = gate.reshape(PEER_NKEYS, tm)

    @pl.when(e == 0)
    def _():
        o_ref[...] = jnp.zeros_like(o_ref)
        gates(cnt_ref, e1_ref, 0, ga_s)

    nt_dims = (((1,), (1,)), ((), ()))
    pieces = tm // MM1_TOKENS

    def weighted(u_rows, g_s):
        for t in range(pieces):
            ts = slice(t * MM1_TOKENS, (t + 1) * MM1_TOKENS)
            act = _gelu(lax.dot_general(u_ref[u_rows, :], hx_ref[ts, :], nt_dims, preferred_element_type=F32))
            w_s[ts, u_rows] = (g_s[:, ts] * act.astype(BF16)).T

    gates(cnt_ref, e1_ref, nsub, gb_s)
    weighted(slice(0, half), ga_s)
    weighted(slice(half, te), gb_s)
    gates(cnt_nxt_ref, e1_nxt_ref, 0, ga_s)
    o_ref[...] += jnp.dot(w_s[...], v_ref[...], preferred_element_type=F32)

    @pl.when(e == nblk - 1)
    def _():
        x = x_ref[...] + gfx_ref[...] * o_ref[...]
        o_ref[...] = x * lax.rsqrt(jnp.mean(x * x, axis=-1, keepdims=True) + NORM_EPS) * gain_ref[...]


def _peer_dense(hx, u, v, cnt, rk, e1, e2, x2d, g_fx, gain, rows_per_mod, tm=512, te=1024):
    m, dm = hx.shape
    ne = u.shape[0]
    nblk = ne // te
    bpm = rows_per_mod // tm
    sel_spec = pl.BlockSpec((PEER_HEADS, PEER_NKEYS, tm), lambda i, e: (0, 0, i))
    tok = pl.BlockSpec((tm, dm), lambda i, e: (i, 0))
    tab = pl.BlockSpec((te, dm), lambda i, e: (e, 0))
    key1_spec = lambda blk: pl.BlockSpec((PEER_HEADS, te // PEER_NKEYS, tm), lambda i, e: (0, blk(e), i))
    return pl.pallas_call(
        functools.partial(_peer_dense_kernel, te=te, nblk=nblk),
        out_shape=jax.ShapeDtypeStruct((m, dm), F32),
        grid=(m // tm, nblk),
        in_specs=[tok, tab, tab, key1_spec(lambda e: e), key1_spec(lambda e: e),
                  key1_spec(lambda e: (e + 1) % nblk), key1_spec(lambda e: (e + 1) % nblk), sel_spec, sel_spec, tok,
                  pl.BlockSpec((None, 1, dm), lambda i, e: (i // bpm, 0, 0)),
                  pl.BlockSpec((1, dm), lambda i, e: (0, 0))],
        out_specs=tok,
        scratch_shapes=[pltpu.VMEM((tm, te), BF16)] + [pltpu.VMEM((te // 2, tm), BF16)] * 2,
        compiler_params=_cp(("parallel", "arbitrary")),
        name="peer_dense",
    )(hx, u, v, cnt, e1, cnt, e1, rk, e2, x2d, g_fx, gain)


def kernel(x, c, ctx, c_ctx, w_ada, b_ada, norm_mix, norm_ffn, norm_final, w_in, rw_conv, rw_w0, rw_w_lora_b, rw_a0, rw_a_lora_b, rw_g_lora_b, rw_k_k, rw_k_a, rw_r_k, rw_ln_w, rw_ln_b, hg_lb, hg_norm, w_up_rw, w_up_hg, w_out, peer_wq, peer_k1, peer_k2, peer_u, peer_v):
    batch, seq, dm = x.shape
    ctx_len = ctx.shape[1]
    depth = w_in.shape[0]
    assert depth == 1 and dm == D_MODEL
    assert seq % (CHUNK * SEQ_CHUNKS) == 0 and ctx_len % (CHUNK * SEQ_CHUNKS) == 0
    n_x = seq // CHUNK
    n_c = ctx_len // CHUNK
    l = 0

    w_in_p = _regroup(w_in, l)
    conv_w = jnp.pad(rw_conv[l].reshape(9, RW_COLS), ((0, 0), (0, RW_PAD - RW_COLS)))
    lower_bounds = jnp.cumsum(jax.nn.softmax(hg_lb.astype(F32), axis=0), axis=0)[l]
    row = lambda t: t.reshape(1, -1)
    hg_norm_row = jnp.tile(hg_norm[l], HG_HEADS).reshape(1, HG_WIDTH)
    u_bf = peer_u[l].astype(BF16)
    v_bf = peer_v[l].astype(BF16)

    cvec = jnp.concatenate([c, c_ctx[None, :], jnp.zeros((8 - batch - 1, dm), F32)], axis=0)
    mod = _ada(cvec, w_ada[l], row(b_ada[l]))
    mods = mod.reshape(8, N_MOD, 1, dm)
    sh_mx, sc_mx, g_mx, sh_fx, sc_fx, g_fx = (mods[:batch, i] for i in range(N_MOD))
    sh_mc, sc_mc = mods[batch:batch + 1, 0], mods[batch:batch + 1, 1]

    x2d = x.reshape(batch * seq, dm)
    c2d = ctx.reshape(batch * ctx_len, dm)
    gain_mix = row(norm_mix[l])

    zc_rw, zc_hg = _inproj(c2d, gain_mix, sh_mc, sc_mc, w_in_p, (RW_PAD, 5 * HG_WIDTH), batch * ctx_len)
    g_lora = jnp.pad(rw_g_lora_b[l], ((0, RW_PAD - RW_COLS), (0, 0)))
    rw_args = (conv_w, rw_w0[l], rw_w_lora_b[l], rw_a0[l], rw_a_lora_b[l], g_lora, row(rw_k_k[l]),
               row(rw_k_a[l]), row(rw_r_k[l]))
    pre_c = _rwkv_pre(zc_rw, *rw_args, batch=batch, nchunks=n_c, is_ctx=True)
    zero_rw = jnp.zeros((batch, RW_HEADS, RW_HEAD, RW_HEAD), F32)
    zero_hg = jnp.zeros((batch, HG_HEADS, HG_HEAD, HG_HEAD), F32)
    _, _, st_rw_f, st_rw_b = _rwkv_seq(zero_rw, zero_rw, pre_c[0:8], batch, n_c)
    _, _, st_hg_f, st_hg_b = _hgrn(zero_hg, zero_hg, zc_hg, lower_bounds, batch, n_c)

    z_rw, z_hg, z_g = _inproj(x2d, gain_mix, sh_mx, sc_mx, w_in_p, (RW_PAD, 5 * HG_WIDTH, 2 * dm), seq)
    pre = _rwkv_pre(z_rw, *rw_args, batch=batch, nchunks=n_x, is_ctx=False)
    o_f, o_b, _, _ = _rwkv_seq(st_rw_f, st_rw_b, pre[0:8], batch, n_x)
    h_f, h_b, _, _ = _hgrn(st_hg_f, st_hg_b, z_hg, lower_bounds, batch, n_x)
    x1 = _readout(x2d, o_f, o_b, pre[9], pre[8], h_f, h_b, z_hg, z_g, g_mx, row(rw_ln_w[l]), row(rw_ln_b[l]),
                  hg_norm_row, w_up_rw[l].astype(BF16), w_up_hg[l].astype(BF16), w_out[l].astype(BF16), seq)

    hx, cnt, rk, e1, e2 = _peer_sel(x1, row(norm_ffn[l]), sh_fx, sc_fx, peer_wq[l].astype(BF16), peer_k1[l],
                                    peer_k2[l], seq)
    out = _peer_dense(hx, u_bf, v_bf, cnt, rk, e1, e2, x1, g_fx, row(norm_final), seq)
    return out.reshape(batch, seq, dm)
```

```python
import functools

import jax
import jax.numpy as jnp
import numpy as np
from jax import lax
from jax.experimental import pallas as pl
from jax.experimental.pallas import tpu as pltpu

F32 = jnp.float32
BF16 = jnp.bfloat16

D_MODEL = 2048
N_MOD = 6
NORM_EPS = 1e-6
CHUNK = 64
SUB = 16
SEQ_CHUNKS = 4
RW_WIDTH = 1024
RW_HEAD = 64
RW_HEADS = 16
RW_LORA = 64
RW_GATE_RANK = 160
RW_COLS = 3488
RW_PAD = 3584
CONV_COLS = 512
NORM_ROWS = 128
RW_GN_EPS = 64e-5
HG_WIDTH = 1024
HG_HEAD = 128
HG_HEADS = 8
PEER_HEADS = 8
PEER_NKEYS = 128
PEER_TOPK = 16
BF16_ROWS = 16
MM1_TOKENS = 256
EXP_M05 = float(np.exp(-0.5))
VMEM_LIMIT = 56 * 1024 * 1024


def _cp(sem):
    return pltpu.CompilerParams(dimension_semantics=sem, vmem_limit_bytes=VMEM_LIMIT)


def _split(x):
    hi = x.astype(BF16)
    return hi, (x - hi.astype(F32)).astype(BF16)


def _dot(a, b, dims, exact):
    dg = lambda p, q: lax.dot_general(p, q, dims, preferred_element_type=F32)
    if not exact:
        return dg(a.astype(BF16), b.astype(BF16))
    a_hi, a_lo = _split(a)
    b_hi, b_lo = _split(b)
    return dg(a_hi, b_hi) + (dg(a_hi, b_lo) + dg(a_lo, b_hi))


def _mm(a, b, exact=False):
    return _dot(a, b, (((1,), (0,)), ((), ())), exact)


def _mm_nt(a, b, exact=False):
    return _dot(a, b, (((1,), (1,)), ((), ())), exact)


def _mm_tn(a, b, exact=False):
    return _dot(a, b, (((0,), (0,)), ((), ())), exact)


def _mm_mask(mask01, x):
    m = mask01.astype(BF16)
    hi = x.astype(BF16)
    mid, lo = _split(x - hi.astype(F32))
    dg = lambda q: jnp.dot(m, q, preferred_element_type=F32)
    return dg(hi) + (dg(mid) + dg(lo))


def _sigmoid(x):
    return 1.0 / (1.0 + jnp.exp(-x))


def _norm_mod(x, gain, shift, scale):
    y = x * lax.rsqrt(jnp.mean(x * x, axis=-1, keepdims=True) + NORM_EPS) * gain
    return y * (1.0 + scale) + shift


def _ada_kernel(c_ref, w_ref, b_ref, o_ref):
    cv = c_ref[...]
    o_ref[...] = _mm(cv * _sigmoid(cv), w_ref[...], exact=True) + b_ref[...]


def _ada(cvec, w, b, tn=512):
    k, n = w.shape
    return pl.pallas_call(
        _ada_kernel,
        out_shape=jax.ShapeDtypeStruct((cvec.shape[0], n), F32),
        grid=(n // tn,),
        in_specs=[pl.BlockSpec((cvec.shape[0], k), lambda j: (0, 0)),
                  pl.BlockSpec((k, tn), lambda j: (0, j)),
                  pl.BlockSpec((1, tn), lambda j: (0, j))],
        out_specs=pl.BlockSpec((cvec.shape[0], tn), lambda j: (0, j)),
        compiler_params=_cp(("parallel",)),
        name="ada",
    )(cvec, w, b)


def _regroup_kernel(w_ref, o_ref):
    w = w_ref[...]
    pad = jnp.zeros((w.shape[0], RW_PAD - RW_COLS), F32)
    o_ref[...] = jnp.concatenate([w[:, :RW_COLS], pad, w[:, RW_COLS:]], axis=1).astype(BF16)


def _regroup(w, layer, tk=128):
    _, k, n = w.shape
    n_out = n + RW_PAD - RW_COLS
    return pl.pallas_call(
        _regroup_kernel,
        out_shape=jax.ShapeDtypeStruct((k, n_out), BF16),
        grid=(k // tk,),
        in_specs=[pl.BlockSpec((None, tk, n), lambda i: (layer, i, 0))],
        out_specs=pl.BlockSpec((tk, n_out), lambda i: (i, 0)),
        compiler_params=_cp(("parallel",)),
        name="regroup",
    )(w)


def _inproj_kernel(x_ref, g_ref, sh_ref, sc_ref, w_ref, *refs, bounds):
    out_refs, h_scr = refs[:-1], refs[-1]
    j = pl.program_id(1)

    @pl.when(j == 0)
    def _():
        for r in range(0, h_scr.shape[0], NORM_ROWS):
            rs = slice(r, r + NORM_ROWS)
            h_scr[rs, :] = _norm_mod(x_ref[rs, :], g_ref[...], sh_ref[...], sc_ref[...]).astype(BF16)

    for (lo, hi), o_ref in zip(bounds, out_refs):
        @pl.when((j >= lo) & (j < hi))
        def _(o_ref=o_ref):
            o_ref[...] = jnp.dot(h_scr[...], w_ref[...], preferred_element_type=F32).astype(BF16)


def _inproj(x2d, gain, shift, scale, w, widths, rows_per_mod, tm=2048, tn=512):
    m, k = x2d.shape
    tm = min(tm, rows_per_mod)
    assert rows_per_mod % tm == 0 and m % tm == 0 and all(wd % tn == 0 for wd in widths)
    bpm = rows_per_mod // tm
    mod_map = lambda i, j: (i // bpm, 0, 0)
    edges = np.cumsum([0] + [wd // tn for wd in widths])
    bounds = [(int(lo), int(hi)) for lo, hi in zip(edges[:-1], edges[1:])]
    out_spec = lambda lo, hi: pl.BlockSpec((tm, tn), lambda i, j: (i, jnp.clip(j - lo, 0, hi - lo - 1)))
    return pl.pallas_call(
        functools.partial(_inproj_kernel, bounds=bounds),
        out_shape=[jax.ShapeDtypeStruct((m, wd), BF16) for wd in widths],
        grid=(m // tm, int(edges[-1])),
        in_specs=[pl.BlockSpec((tm, k), lambda i, j: (i, 0), pipeline_mode=pl.Buffered(1)),
                  pl.BlockSpec((1, k), lambda i, j: (0, 0)),
                  pl.BlockSpec((None, 1, k), mod_map),
                  pl.BlockSpec((None, 1, k), mod_map),
                  pl.BlockSpec((k, tn), lambda i, j: (0, j))],
        out_specs=[out_spec(lo, hi) for lo, hi in bounds],
        scratch_shapes=[pltpu.VMEM((tm, k), BF16)],
        compiler_params=_cp(("parallel", "arbitrary")),
        name="inproj",
    )(x2d, gain, shift, scale, w)


def _tri_masks(reverse):
    row = lax.broadcasted_iota(jnp.int32, (CHUNK, CHUNK), 0)
    col = lax.broadcasted_iota(jnp.int32, (CHUNK, CHUNK), 1)
    if reverse:
        strict, incl = col > row, col >= row
    else:
        strict, incl = col < row, col <= row
    same = (row // SUB) == (col // SUB)
    return strict, incl, same, row == col


def _rw_heads(ar_t, bk_t, a_0, r_0, bk_end, v_b, decay_tot, reverse, out_refs):
    strict, incl, same, eye = _tri_masks(reverse)
    row2 = lax.broadcasted_iota(jnp.int32, (CHUNK, 2 * CHUNK), 0)
    col2 = lax.broadcasted_iota(jnp.int32, (CHUNK, 2 * CHUNK), 1) % CHUNK
    incl2 = (col2 >= row2) if reverse else (col2 <= row2)
    ident = jnp.where(eye, 1.0, 0.0)
    heads = range(RW_HEADS)
    hs = lambda h: slice(h * RW_HEAD, (h + 1) * RW_HEAD)
    bf = lambda xs: [x.astype(BF16) for x in xs]
    mm = lambda xs, ys: [jnp.dot(x, y, preferred_element_type=F32) for x, y in zip(xs, ys)]
    nt_dims = (((1,), (1,)), ((), ()))
    tn_dims = (((0,), (0,)), ((), ()))

    s4 = [lax.dot_general(ar_t[:, hs(h)], bk_t[:, hs(h)], nt_dims, preferred_element_type=F32) for h in heads]
    m_ab = [jnp.where(strict, s[:CHUNK, :CHUNK], 0.0) for s in s4]
    m_ak = bf([jnp.where(strict, s[:CHUNK, CHUNK:], 0.0) for s in s4])
    m_r = bf([jnp.where(incl2, s[CHUNK:], 0.0) for s in s4])

    m_bd = [jnp.where(same, m, 0.0) for m in m_ab]
    m_off = bf([jnp.where(same, 0.0, m) for m in m_ab])
    m1 = bf(m_bd)
    m2 = bf(mm(m1, m1))
    m4 = bf(mm(m2, m2))
    m8 = bf(mm(m4, m4))
    p = [ident + m for m in m_bd]
    for mk in (m2, m4, m8):
        p = [x + y for x, y in zip(p, mm(bf(p), mk))]
    p_b = bf(p)
    n1 = mm(p_b, m_off)
    n1_b = bf(n1)
    n2 = mm(n1_b, n1_b)
    n3 = mm(n1_b, bf(n2))
    t_inv = bf(mm(bf([ident + a + b + c for a, b, c in zip(n1, n2, n3)]), p_b))

    x = bf(mm(m_ak, [v_b[:, hs(h)] for h in heads]))
    wu = bf(mm(t_inv, [jnp.concatenate([a_0[:, hs(h)], xh], axis=1) for h, xh in zip(heads, x)]))
    zero = jnp.zeros((CHUNK, RW_HEAD), BF16)
    rhs = [jnp.concatenate([w, jnp.concatenate([zero, v_b[:, hs(h)]], axis=1)], axis=0) for h, w in zip(heads, wu)]
    qo = mm(m_r, rhs)
    gd = [lax.dot_general(bk_end[:, hs(h)], r, tn_dims, preferred_element_type=F32) for h, r in zip(heads, rhs)]
    q_ref, o_ref, g_ref, d_ref = out_refs
    for h in heads:
        q_ref[:, hs(h)] = r_0[:, hs(h)] + qo[h][:, :RW_HEAD]
        o_ref[:, hs(h)] = qo[h][:, RW_HEAD:]
        g_ref[:, hs(h)] = gd[h][:, :RW_HEAD] + jnp.where(eye, decay_tot[:, hs(h)], 0.0)
        d_ref[:, hs(h)] = gd[h][:, RW_HEAD:]


def _rwkv_pre_kernel(zp_ref, zc_ref, zn_ref, conv_ref, w0_ref, wl_ref, a0_ref, al_ref, gl_ref, kk_ref, ka_ref,
                     rk_ref, qf_ref, of_ref, gf_ref, df_ref, qb_ref, ob_ref, gb_ref, db_ref, g_ref, bonus_ref,
                     conv_s, *, is_ctx, nchunks):
    c = pl.program_id(1)
    row = lax.broadcasted_iota(jnp.int32, (CHUNK, 1), 0)
    has_p = (c > 0).astype(F32)
    has_n = (c < nchunks - 1).astype(F32)

    def left(z, fill):
        return jnp.where(row == 0, fill, pltpu.roll(z, 1, 0))

    def right(z, fill):
        return jnp.where(row == CHUNK - 1, fill, pltpu.roll(z, CHUNK - 1, 0))

    for cb in range(RW_PAD // CONV_COLS):
        cs = slice(cb * CONV_COLS, (cb + 1) * CONV_COLS)
        w = conv_ref[:, cs]
        zc = zc_ref[:, cs].astype(F32)
        if is_ctx:
            prev_row = zp_ref[CHUNK - BF16_ROWS:CHUNK, cs].astype(F32)[BF16_ROWS - 1:BF16_ROWS] * has_p
            next_row = zn_ref[0:BF16_ROWS, cs].astype(F32)[0:1] * has_n
            conv = left(zc, prev_row) * w[3:4] + zc * w[4:5] + right(zc, next_row) * w[5:6]
        else:
            zp = zp_ref[:, cs].astype(F32) * has_p
            zn = zn_ref[:, cs].astype(F32) * has_n
            col = lambda c: zp * w[c:c + 1] + zc * w[3 + c:4 + c] + zn * w[6 + c:7 + c]
            conv = left(col(0), 0.0) + col(1) + right(col(2), 0.0)
        conv_s[:, cs] = conv

    r = conv_s[:, 0:RW_WIDTH]
    k = conv_s[:, RW_WIDTH:2 * RW_WIDTH]
    v = conv_s[:, 2 * RW_WIDTH:3 * RW_WIDTH]
    lo = conv_s[:, 3 * RW_WIDTH:RW_PAD]
    g_ref[...] = _mm(_sigmoid(lo[:, 4 * RW_LORA:]), gl_ref[...], exact=True)

    first = lax.broadcasted_iota(jnp.int32, (1, 2 * RW_HEAD), 1) < RW_HEAD

    def head_sum(t):
        tiles = []
        for i in range(RW_WIDTH // (2 * RW_HEAD)):
            ti = t[:, i * 2 * RW_HEAD:(i + 1) * 2 * RW_HEAD]
            s0 = jnp.sum(jnp.where(first, ti, 0.0), axis=-1, keepdims=True)
            s1 = jnp.sum(jnp.where(first, 0.0, ti), axis=-1, keepdims=True)
            tiles.append(jnp.where(first, s0, s1))
        return jnp.concatenate(tiles, axis=1)

    kkr = k * kk_ref[...]
    kk = kkr / jnp.maximum(jnp.sqrt(head_sum(kkr * kkr)), 1e-12)
    v_b = v.astype(BF16)
    outs = ((qf_ref, of_ref, gf_ref, df_ref), (qb_ref, ob_ref, gb_ref, db_ref))
    trow = lax.broadcasted_iota(jnp.int32, (CHUNK, CHUNK), 0)
    tcol = lax.broadcasted_iota(jnp.int32, (CHUNK, CHUNK), 1)
    bonus = None
    for d in range(2):
        wa = lo[:, d * RW_LORA:(d + 1) * RW_LORA]
        aa = lo[:, (2 + d) * RW_LORA:(3 + d) * RW_LORA]
        w_pre = w0_ref[d:d + 1, :] + _mm(jnp.tanh(wa), wl_ref[d], exact=True)
        logw = -EXP_M05 * _sigmoid(w_pre)
        a = _sigmoid(a0_ref[d:d + 1, :] + _mm(aa, al_ref[d], exact=True))
        tri = jnp.where((tcol >= trow) if d == 1 else (tcol <= trow), 1.0, 0.0)
        cum = _mm_mask(tri, logw)
        prev = cum - logw
        mid = CHUNK // 2 - 1 if d == 1 else CHUNK // 2
        end = 0 if d == 1 else CHUNK - 1
        rho = cum[mid:mid + 1]
        tot = cum[end:end + 1]
        kd = k * (1.0 + (a - 1.0) * ka_ref[...])
        kka = kk * a
        hb = head_sum(r * kd * rk_ref[...])
        bonus = hb if bonus is None else bonus + hb
        e_out = jnp.exp(rho - cum)
        e_end = jnp.exp(tot - cum)
        ar_t = jnp.concatenate([-kk * jnp.exp(prev - rho), r * jnp.exp(cum - rho)], axis=0).astype(BF16)
        bk_t = jnp.concatenate([kka * e_out, kd * e_out], axis=0).astype(BF16)
        bk_end = jnp.concatenate([kka * e_end, kd * e_end], axis=0).astype(BF16)
        a_0 = (-kk * jnp.exp(prev)).astype(BF16)
        r_0 = r * jnp.exp(cum)
        _rw_heads(ar_t, bk_t, a_0, r_0, bk_end, v_b, jnp.exp(tot), d == 1, outs[d])
    bonus_ref[...] = bonus * v


def _rwkv_pre(z_rw, conv_w, w0, wl, a0, al, gl, k_k, k_a, r_k, batch, nchunks, is_ctx):
    rows = z_rw.shape[0]
    last = nchunks - 1
    zspec = lambda f: pl.BlockSpec((CHUNK, RW_PAD), lambda b, c: (b * nchunks + f(c), 0))
    full = lambda shape: pl.BlockSpec(shape, lambda b, c: (0,) * len(shape))
    ospec = pl.BlockSpec((CHUNK, RW_WIDTH), lambda b, c: (b * nchunks + c, 0))
    out = jax.ShapeDtypeStruct((rows, RW_WIDTH), F32)
    return pl.pallas_call(
        functools.partial(_rwkv_pre_kernel, is_ctx=is_ctx, nchunks=nchunks),
        out_shape=[out] * 10,
        grid=(batch, nchunks),
        in_specs=[zspec(lambda c: jnp.maximum(c - 1, 0)), zspec(lambda c: c), zspec(lambda c: jnp.minimum(c + 1, last)),
                  full((9, RW_PAD)), full((2, RW_WIDTH)), full((2, RW_LORA, RW_WIDTH)), full((2, RW_WIDTH)),
                  full((2, RW_LORA, RW_WIDTH)), full((RW_PAD - 3 * RW_WIDTH - 4 * RW_LORA, RW_WIDTH)), full((1, RW_WIDTH)),
                  full((1, RW_WIDTH)), full((1, RW_WIDTH))],
        out_specs=[ospec] * 10,
        scratch_shapes=[pltpu.VMEM((CHUNK, RW_PAD), F32)],
        compiler_params=_cp(("parallel", "parallel")),
        name="rwkv_pre",
    )(z_rw, z_rw, z_rw, conv_w, w0, wl, a0, al, gl, k_k, k_a, r_k)


def _rwkv_seq_kernel(x0f_ref, x0b_ref, qf_ref, of_ref, gf_ref, df_ref, qb_ref, ob_ref, gb_ref, db_ref,
                     outf_ref, outb_ref, xff_ref, xfb_ref, x_s, *, nsteps):
    c = pl.program_id(1)

    @pl.when(c == 0)
    def _():
        x_s[0] = x0f_ref[...]
        x_s[1] = x0b_ref[...]

    dirs = ((qf_ref, of_ref, gf_ref, df_ref, outf_ref), (qb_ref, ob_ref, gb_ref, db_ref, outb_ref))
    hs = lambda h: slice(h * RW_HEAD, (h + 1) * RW_HEAD)
    for j in range(SEQ_CHUNKS):
        for d, (q_ref, o_ref, g_ref, d_ref, out_ref) in enumerate(dirs):
            jj = SEQ_CHUNKS - 1 - j if d == 1 else j
            rows = slice(jj * CHUNK, (jj + 1) * CHUNK)
            res = [_mm(jnp.concatenate([q_ref[rows, hs(h)], g_ref[rows, hs(h)]], axis=0), x_s[d, h], exact=True)
                   for h in range(RW_HEADS)]
            for h in range(RW_HEADS):
                out_ref[rows, hs(h)] = res[h][:CHUNK] + o_ref[rows, hs(h)]
                x_s[d, h] = res[h][CHUNK:] + d_ref[rows, hs(h)]

    @pl.when(c == nsteps - 1)
    def _():
        xff_ref[...] = x_s[0]
        xfb_ref[...] = x_s[1]


def _rwkv_seq(x0f, x0b, pre, batch, nchunks):
    nsteps = nchunks // SEQ_CHUNKS
    blk = (SEQ_CHUNKS * CHUNK, RW_WIDTH)
    fspec = pl.BlockSpec(blk, lambda b, c: (b * nsteps + c, 0))
    bspec = pl.BlockSpec(blk, lambda b, c: (b * nsteps + nsteps - 1 - c, 0))
    sspec = pl.BlockSpec((None, RW_HEADS, RW_HEAD, RW_HEAD), lambda b, c: (b, 0, 0, 0))
    seq = jax.ShapeDtypeStruct(pre[0].shape, F32)
    state = jax.ShapeDtypeStruct(x0f.shape, F32)
    return pl.pallas_call(
        functools.partial(_rwkv_seq_kernel, nsteps=nsteps),
        out_shape=[seq, seq, state, state],
        grid=(batch, nsteps),
        in_specs=[sspec, sspec] + [fspec] * 4 + [bspec] * 4,
        out_specs=[fspec, bspec, sspec, sspec],
        scratch_shapes=[pltpu.VMEM((2, RW_HEADS, RW_HEAD, RW_HEAD), F32)],
        compiler_params=_cp(("parallel", "arbitrary")),
        name="rwkv_seq",
    )(x0f, x0b, *pre)


def _hgrn_prep(d, q_ref, f_ref, i_ref, lb, rows):
    reverse = d == 1
    row = lax.broadcasted_iota(jnp.int32, (CHUNK, CHUNK), 0)
    col = lax.broadcasted_iota(jnp.int32, (CHUNK, CHUNK), 1)
    incl = (col >= row) if reverse else (col <= row)
    mid = CHUNK // 2 - 1 if reverse else CHUNK // 2
    end = 0 if reverse else CHUNK - 1

    forget = lb + (1.0 - lb) * _sigmoid(f_ref[rows, :].astype(F32))
    kf = 1.0 - forget
    qraw = q_ref[rows, :].astype(F32)
    q = qraw * _sigmoid(qraw)
    v = i_ref[rows, :].astype(F32)
    cum = _mm_mask(jnp.where(incl, 1.0, 0.0), jnp.log(forget))
    ref = cum[mid:mid + 1]
    tot = cum[end:end + 1]
    bf = lambda t: t.astype(BF16)
    return dict(incl=incl, qs=bf(q * jnp.exp(cum - ref)), ks=bf(kf * jnp.exp(ref - cum)), qe=bf(q * jnp.exp(cum)),
                ke=bf(kf * jnp.exp(tot - cum)), v=bf(v), dec=jnp.exp(tot))


def _hgrn_heads(st_s, preps, out_refs, rows):
    for h in range(HG_HEADS):
        sl = slice(h * HG_HEAD, (h + 1) * HG_HEAD)
        for d, (p, out_ref, rw) in enumerate(zip(preps, out_refs, rows)):
            st = st_s[d, h]
            scores = jnp.where(p["incl"], _mm_nt(p["qs"][:, sl], p["ks"][:, sl]), 0.0)
            out_ref[rw, sl] = _mm(scores, p["v"][:, sl]) + _mm_nt(p["qe"][:, sl], st)
            st_s[d, h] = st * p["dec"][:, sl] + _mm_tn(p["v"][:, sl], p["ke"][:, sl])


def _hgrn_kernel(s0f_ref, s0b_ref, qf_ref, ff_ref, if_ref, qb_ref, fb_ref, ib_ref, lb_ref, outf_ref, outb_ref,
                 sff_ref, sfb_ref, st_s, *, nsteps):
    c = pl.program_id(1)

    @pl.when(c == 0)
    def _():
        st_s[0] = s0f_ref[...]
        st_s[1] = s0b_ref[...]

    for j in range(SEQ_CHUNKS):
        jj = SEQ_CHUNKS - 1 - j
        rows = (slice(j * CHUNK, (j + 1) * CHUNK), slice(jj * CHUNK, (jj + 1) * CHUNK))
        preps = (_hgrn_prep(0, qf_ref, ff_ref, if_ref, lb_ref[0:1, :], rows[0]),
                 _hgrn_prep(1, qb_ref, fb_ref, ib_ref, lb_ref[1:2, :], rows[1]))
        _hgrn_heads(st_s, preps, (outf_ref, outb_ref), rows)

    @pl.when(c == nsteps - 1)
    def _():
        sff_ref[...] = st_s[0]
        sfb_ref[...] = st_s[1]


def _hgrn(s0f, s0b, z_hg, lb, batch, nchunks):
    nsteps = nchunks // SEQ_CHUNKS
    blk = (SEQ_CHUNKS * CHUNK, HG_WIDTH)
    fspec = lambda col: pl.BlockSpec(blk, lambda b, c: (b * nsteps + c, col))
    bspec = lambda col: pl.BlockSpec(blk, lambda b, c: (b * nsteps + nsteps - 1 - c, col))
    sspec = pl.BlockSpec((None, HG_HEADS, HG_HEAD, HG_HEAD), lambda b, c: (b, 0, 0, 0))
    seq = jax.ShapeDtypeStruct((z_hg.shape[0], HG_WIDTH), F32)
    state = jax.ShapeDtypeStruct(s0f.shape, F32)
    return pl.pallas_call(
        functools.partial(_hgrn_kernel, nsteps=nsteps),
        out_shape=[seq, seq, state, state],
        grid=(batch, nsteps),
        in_specs=[sspec, sspec, fspec(0), fspec(1), fspec(3), bspec(0), bspec(2), bspec(3),
                  pl.BlockSpec((2, HG_WIDTH), lambda b, c: (0, 0))],
        out_specs=[fspec(0), bspec(0), sspec, sspec],
        scratch_shapes=[pltpu.VMEM((2, HG_HEADS, HG_HEAD, HG_HEAD), F32)],
        compiler_params=_cp(("parallel", "arbitrary")),
        name="hgrn",
    )(s0f, s0b, z_hg, z_hg, z_hg, z_hg, z_hg, z_hg, lb)


def _readout_kernel(x_ref, of_ref, ob_ref, bonus_ref, g_ref, hf_ref, hb_ref, og_ref, grw_ref, ghg_ref, gmx_ref,
                    lnw_ref, lnb_ref, hgn_ref, wrw_ref, whg_ref, wout_ref, o_ref, y_s):
    first = lax.broadcasted_iota(jnp.int32, (1, 2 * RW_HEAD), 1) < RW_HEAD
    tiles = [slice(i * 2 * RW_HEAD, (i + 1) * 2 * RW_HEAD) for i in range(RW_WIDTH // (2 * RW_HEAD))]

    def head_mean(ts):
        s0 = [jnp.sum(jnp.where(first, t, 0.0), axis=-1, keepdims=True) for t in ts]
        s1 = [jnp.sum(jnp.where(first, 0.0, t), axis=-1, keepdims=True) for t in ts]
        return [jnp.where(first, a, b) * (1.0 / RW_HEAD) for a, b in zip(s0, s1)]

    o_rw = [of_ref[:, sl] + ob_ref[:, sl] for sl in tiles]
    dv = [o - mu for o, mu in zip(o_rw, head_mean(o_rw))]
    var = head_mean([d * d for d in dv])
    for sl, d, vr in zip(tiles, dv, var):
        y_s[:, sl] = d * lax.rsqrt(vr + RW_GN_EPS)
    y_rw = (y_s[...] * lnw_ref[...] + lnb_ref[...] + bonus_ref[...]) * g_ref[...]
    branch_rw = _mm(y_rw, wrw_ref[...])

    hg_tiles = [slice(h * HG_HEAD, (h + 1) * HG_HEAD) for h in range(HG_HEADS)]
    o_hg = [hf_ref[:, sl] + hb_ref[:, sl] for sl in hg_tiles]
    ms = [jnp.mean(o * o, axis=-1, keepdims=True) for o in o_hg]
    for sl, o, m2 in zip(hg_tiles, o_hg, ms):
        y_s[:, sl] = o * lax.rsqrt(m2 + NORM_EPS)
    og = og_ref[...].astype(F32)
    y_hg = y_s[...] * hgn_ref[...] * (og * _sigmoid(og))
    branch_hg = _mm(y_hg, whg_ref[...])

    merged = _sigmoid(grw_ref[...].astype(F32)) * branch_rw + _sigmoid(ghg_ref[...].astype(F32)) * branch_hg
    o_ref[...] = x_ref[...] + gmx_ref[...] * _mm(merged, wout_ref[...])


def _readout(x2d, o_f, o_b, bonus, g, h_f, h_b, z_hg, z_g, g_mx, ln_w, ln_b, hg_norm, w_rw, w_hg, w_out,
             rows_per_mod, tm=256):
    m, dm = x2d.shape
    bpm = rows_per_mod // tm
    tok = lambda width, blk=0: pl.BlockSpec((tm, width), lambda i: (i, blk))
    const = lambda shape: pl.BlockSpec(shape, lambda i: (0,) * len(shape), pipeline_mode=pl.Buffered(1))
    return pl.pallas_call(
        _readout_kernel,
        out_shape=jax.ShapeDtypeStruct((m, dm), F32),
        grid=(m // tm,),
        in_specs=[tok(dm), tok(RW_WIDTH), tok(RW_WIDTH), tok(RW_WIDTH), tok(RW_WIDTH), tok(HG_WIDTH), tok(HG_WIDTH),
                  tok(HG_WIDTH, 4), tok(dm, 0), tok(dm, 1),
                  pl.BlockSpec((None, 1, dm), lambda i: (i // bpm, 0, 0)),
                  const((1, RW_WIDTH)), const((1, RW_WIDTH)), const((1, HG_WIDTH)),
                  const((RW_WIDTH, dm)), const((HG_WIDTH, dm)), const((dm, dm))],
        out_specs=tok(dm),
        scratch_shapes=[pltpu.VMEM((tm, RW_WIDTH), F32)],
        compiler_params=_cp(("parallel",)),
        name="readout",
    )(x2d, o_f, o_b, bonus, g, h_f, h_b, z_hg, z_g, z_g, g_mx, ln_w, ln_b, hg_norm, w_rw, w_hg, w_out)


def _cand_groups():
    return [(i, PEER_TOPK // (i + 1)) for i in range(PEER_TOPK)]


def _pick_max(work, io, first_only):
    m = jnp.max(work, axis=0, keepdims=True)
    sel = work == m
    if first_only:
        sel = io == jnp.min(jnp.where(sel, io, work.shape[0]), axis=0, keepdims=True)
    return m, sel


def _topk_ranks(s, vals_ref, first_only):
    io = lax.broadcasted_iota(jnp.int32, s.shape, 0)

    if first_only:
        def body(r, carry):
            work, rank = carry
            m, sel = _pick_max(work, io, True)
            vals_ref[pl.ds(r, 1), :] = m
            return jnp.where(sel, -jnp.inf, work), jnp.where(sel, lax.convert_element_type(r, F32), rank)

        _, rank = lax.fori_loop(0, PEER_TOPK, body, (s, jnp.full(s.shape, float(PEER_TOPK), F32)))
    else:
        def body(r, work):
            m, sel = _pick_max(work, io, False)
            vals_ref[pl.ds(r, 1), :] = m
            return jnp.where(sel, -jnp.inf, work)

        lax.fori_loop(0, PEER_TOPK, body, s)
        rank = jnp.zeros(s.shape, F32)
        for r in range(PEER_TOPK):
            rank = rank + jnp.where(vals_ref[r:r + 1, :] > s, 1.0, 0.0)
    taken = jnp.sum(jnp.where(rank < float(PEER_TOPK), 1.0, 0.0), axis=0, keepdims=True)
    return rank, jnp.where(taken == float(PEER_TOPK), 0.0, 1.0)


def _peer_sel_kernel(x_ref, gain_ref, sh_ref, sc_ref, wq_ref, k1_ref, k2_ref, hx_ref, cnt_ref, rk_ref, e1_ref,
                     e2_ref, q_s, v1_s, v2_s):
    hb = _norm_mod(x_ref[...], gain_ref[...], sh_ref[...], sc_ref[...]).astype(BF16)
    hx_ref[...] = hb
    q_s[...] = jnp.dot(hb, wq_ref[...], preferred_element_type=F32)
    tm = q_s.shape[0]
    groups = _cand_groups()
    ncand = sum(n for _, n in groups)
    npad = -ncand % 8

    def select(hd, first_only):
        q1 = q_s[:, (2 * hd) * PEER_NKEYS:(2 * hd + 1) * PEER_NKEYS]
        q2 = q_s[:, (2 * hd + 1) * PEER_NKEYS:(2 * hd + 2) * PEER_NKEYS]
        s1 = _mm_nt(k1_ref[...], q1, exact=True)
        s2 = _mm_nt(k2_ref[...], q2, exact=True)
        rank1, bad1 = _topk_ranks(s1, v1_s, first_only)
        rank2, bad2 = _topk_ranks(s2, v2_s, first_only)
        v1 = v1_s[...]
        v2 = v2_s[...]
        rows = [v1[i:i + 1] + v2[0:n] for i, n in groups]
        if npad:
            rows.append(jnp.full((npad, tm), -jnp.inf, F32))
        cand = jnp.concatenate(rows, axis=0)
        io = lax.broadcasted_iota(jnp.int32, cand.shape, 0)
        best0 = v1[0:1] + v2[0:1]

        def body(r, carry):
            work, z = carry
            m, sel = _pick_max(work, io, first_only)
            return jnp.where(sel, -jnp.inf, work), z + jnp.exp(m - best0)

        work, z = lax.fori_loop(0, PEER_TOPK, body, (cand, jnp.zeros((1, tm), F32)))
        taken = jnp.where((work == -jnp.inf) & (io < ncand), 1.0, 0.0)
        bad3 = jnp.where(jnp.sum(taken, axis=0, keepdims=True) == float(PEER_TOPK), 0.0, 1.0)
        cnt1 = jnp.zeros_like(rank1)
        start = 0
        for i, n in groups:
            c_i = jnp.sum(taken[start:start + n], axis=0, keepdims=True)
            cnt1 = cnt1 + jnp.where(rank1 == float(i), c_i, 0.0)
            start += n
        cnt_ref[hd] = cnt1
        rk_ref[hd] = rank2.astype(BF16)
        e1_ref[hd] = jnp.exp(s1 - v1[0:1])
        e2_ref[hd] = (jnp.exp(s2 - v2[0:1]) / z).astype(BF16)
        return jnp.maximum(bad1, jnp.maximum(bad2, bad3))

    for hd in range(PEER_HEADS):
        bad = select(hd, False)

        @pl.when(jnp.max(bad) > 0.5)
        def _(hd=hd):
            select(hd, True)


def _peer_sel(x2d, gain, shift, scale, wq, k1, k2, rows_per_mod, tm=256):
    m, dm = x2d.shape
    bpm = rows_per_mod // tm
    mod_map = lambda i: (i // bpm, 0, 0)
    const = lambda shape: pl.BlockSpec(shape, lambda i: (0,) * len(shape))
    sel = jax.ShapeDtypeStruct((PEER_HEADS, PEER_NKEYS, m), F32)
    sel_bf = jax.ShapeDtypeStruct((PEER_HEADS, PEER_NKEYS, m), BF16)
    sel_spec = pl.BlockSpec((PEER_HEADS, PEER_NKEYS, tm), lambda i: (0, 0, i))
    return pl.pallas_call(
        _peer_sel_kernel,
        out_shape=[jax.ShapeDtypeStruct((m, dm), BF16), sel, sel_bf, sel, sel_bf],
        grid=(m // tm,),
        in_specs=[pl.BlockSpec((tm, dm), lambda i: (i, 0)), const((1, dm)),
                  pl.BlockSpec((None, 1, dm), mod_map), pl.BlockSpec((None, 1, dm), mod_map),
                  const(wq.shape), const(k1.shape), const(k2.shape)],
        out_specs=[pl.BlockSpec((tm, dm), lambda i: (i, 0)), sel_spec, sel_spec, sel_spec, sel_spec],
        scratch_shapes=[pltpu.VMEM((tm, dm), F32), pltpu.VMEM((PEER_TOPK, tm), F32),
                        pltpu.VMEM((PEER_TOPK, tm), F32)],
        compiler_params=_cp(("parallel",)),
        name="peer_sel",
    )(x2d, gain, shift, scale, wq, k1, k2)


def _gelu(x):
    return 0.5 * x * (1.0 + lax.erf(x * float(np.sqrt(0.5))))


def _peer_dense_kernel(hx_ref, u_ref, v_ref, cnt_ref, e1_ref, cnt_nxt_ref, e1_nxt_ref, rk_ref, e2_ref, x_ref, gfx_ref,
                       gain_ref, o_ref, w_s, ga_s, gb_s, *, te, nblk):
    e = pl.program_id(1)
    half = te // 2
    nsub = half // PEER_NKEYS
    tm = hx_ref.shape[0]

    def gates(cnt_blk, e1_blk, first, dst):
        packed = (PEER_NKEYS // BF16_ROWS, BF16_ROWS, tm)
        for ii in range(nsub):
            gate = None
            for hd in range(PEER_HEADS):
                cnt = jnp.broadcast_to(cnt_blk[hd, first + ii:first + ii + 1, :], packed[1:]).astype(BF16)
                e1 = jnp.broadcast_to(e1_blk[hd, first + ii:first + ii + 1, :], packed[1:]).astype(BF16)
                t = jnp.where(rk_ref[hd].reshape(packed) < cnt[None], e2_ref[hd].reshape(packed) * e1[None],
                              jnp.zeros((), BF16))
                gate = t if gate is None else gate + t
            dst[ii * PEER_NKEYS:(ii + 1) * PEER_NKEYS, :] = gate.reshape(PEER_NKEYS, tm)

    @pl.when(e == 0)
    def _():
        o_ref[...] = jnp.zeros_like(o_ref)
        gates(cnt_ref, e1_ref, 0, ga_s)

    nt_dims = (((1,), (1,)), ((), ()))
    pieces = tm // MM1_TOKENS

    def weighted(u_rows, g_s):
        for t in range(pieces):
            ts = slice(t * MM1_TOKENS, (t + 1) * MM1_TOKENS)
            act = _gelu(lax.dot_general(u_ref[u_rows, :], hx_ref[ts, :], nt_dims, preferred_element_type=F32))
            w_s[ts, u_rows] = (g_s[:, ts] * act.astype(BF16)).T

    gates(cnt_ref, e1_ref, nsub, gb_s)
    weighted(slice(0, half), ga_s)
    weighted(slice(half, te), gb_s)
    gates(cnt_nxt_ref, e1_nxt_ref, 0, ga_s)
    o_ref[...] += jnp.dot(w_s[...], v_ref[...], preferred_element_type=F32)

    @pl.when(e == nblk - 1)
    def _():
        x = x_ref[...] + gfx_ref[...] * o_ref[...]
        o_ref[...] = x * lax.rsqrt(jnp.mean(x * x, axis=-1, keepdims=True) + NORM_EPS) * gain_ref[...]


def _peer_dense(hx, u, v, cnt, rk, e1, e2, x2d, g_fx, gain, rows_per_mod, tm=512, te=1024):
    m, dm = hx.shape
    ne = u.shape[0]
    nblk = ne // te
    bpm = rows_per_mod // tm
    sel_spec = pl.BlockSpec((PEER_HEADS, PEER_NKEYS, tm), lambda i, e: (0, 0, i))
    tok = pl.BlockSpec((tm, dm), lambda i, e: (i, 0))
    tab = pl.BlockSpec((te, dm), lambda i, e: (e, 0))
    key1_spec = lambda blk: pl.BlockSpec((PEER_HEADS, te // PEER_NKEYS, tm), lambda i, e: (0, blk(e), i))
    return pl.pallas_call(
        functools.partial(_peer_dense_kernel, te=te, nblk=nblk),
        out_shape=jax.ShapeDtypeStruct((m, dm), F32),
        grid=(m // tm, nblk),
        in_specs=[tok, tab, tab, key1_spec(lambda e: e), key1_spec(lambda e: e),
                  key1_spec(lambda e: (e + 1) % nblk), key1_spec(lambda e: (e + 1) % nblk), sel_spec, sel_spec, tok,
                  pl.BlockSpec((None, 1, dm), lambda i, e: (i // bpm, 0, 0)),
                  pl.BlockSpec((1, dm), lambda i, e: (0, 0))],
        out_specs=tok,
        scratch_shapes=[pltpu.VMEM((tm, te), BF16)] + [pltpu.VMEM((te // 2, tm), BF16)] * 2,
        compiler_params=_cp(("parallel", "arbitrary")),
        name="peer_dense",
    )(hx, u, v, cnt, e1, cnt, e1, rk, e2, x2d, g_fx, gain)


def kernel(x, c, ctx, c_ctx, w_ada, b_ada, norm_mix, norm_ffn, norm_final, w_in, rw_conv, rw_w0, rw_w_lora_b, rw_a0, rw_a_lora_b, rw_g_lora_b, rw_k_k, rw_k_a, rw_r_k, rw_ln_w, rw_ln_b, hg_lb, hg_norm, w_up_rw, w_up_hg, w_out, peer_wq, peer_k1, peer_k2, peer_u, peer_v):
    batch, seq, dm = x.shape
    ctx_len = ctx.shape[1]
    depth = w_in.shape[0]
    assert depth == 1 and dm == D_MODEL
    assert seq % (CHUNK * SEQ_CHUNKS) == 0 and ctx_len % (CHUNK * SEQ_CHUNKS) == 0
    n_x = seq // CHUNK
    n_c = ctx_len // CHUNK
    l = 0

    w_in_p = _regroup(w_in, l)
    conv_w = jnp.pad(rw_conv[l].reshape(9, RW_COLS), ((0, 0), (0, RW_PAD - RW_COLS)))
    lower_bounds = jnp.cumsum(jax.nn.softmax(hg_lb.astype(F32), axis=0), axis=0)[l]
    row = lambda t: t.reshape(1, -1)
    hg_norm_row = jnp.tile(hg_norm[l], HG_HEADS).reshape(1, HG_WIDTH)
    u_bf = peer_u[l].astype(BF16)
    v_bf = peer_v[l].astype(BF16)

    cvec = jnp.concatenate([c, c_ctx[None, :], jnp.zeros((8 - batch - 1, dm), F32)], axis=0)
    mod = _ada(cvec, w_ada[l], row(b_ada[l]))
    mods = mod.reshape(8, N_MOD, 1, dm)
    sh_mx, sc_mx, g_mx, sh_fx, sc_fx, g_fx = (mods[:batch, i] for i in range(N_MOD))
    sh_mc, sc_mc = mods[batch:batch + 1, 0], mods[batch:batch + 1, 1]

    x2d = x.reshape(batch * seq, dm)
    c2d = ctx.reshape(batch * ctx_len, dm)
    gain_mix = row(norm_mix[l])

    zc_rw, zc_hg = _inproj(c2d, gain_mix, sh_mc, sc_mc, w_in_p, (RW_PAD, 5 * HG_WIDTH), batch * ctx_len)
    g_lora = jnp.pad(rw_g_lora_b[l], ((0, RW_PAD - RW_COLS), (0, 0)))
    rw_args = (conv_w, rw_w0[l], rw_w_lora_b[l], rw_a0[l], rw_a_lora_b[l], g_lora, row(rw_k_k[l]),
               row(rw_k_a[l]), row(rw_r_k[l]))
    pre_c = _rwkv_pre(zc_rw, *rw_args, batch=batch, nchunks=n_c, is_ctx=True)
    zero_rw = jnp.zeros((batch, RW_HEADS, RW_HEAD, RW_HEAD), F32)
    zero_hg = jnp.zeros((batch, HG_HEADS, HG_HEAD, HG_HEAD), F32)
    _, _, st_rw_f, st_rw_b = _rwkv_seq(zero_rw, zero_rw, pre_c[0:8], batch, n_c)
    _, _, st_hg_f, st_hg_b = _hgrn(zero_hg, zero_hg, zc_hg, lower_bounds, batch, n_c)

    z_rw, z_hg, z_g = _inproj(x2d, gain_mix, sh_mx, sc_mx, w_in_p, (RW_PAD, 5 * HG_WIDTH, 2 * dm), seq)
    pre = _rwkv_pre(z_rw, *rw_args, batch=batch, nchunks=n_x, is_ctx=False)
    o_f, o_b, _, _ = _rwkv_seq(st_rw_f, st_rw_b, pre[0:8], batch, n_x)
    h_f, h_b, _, _ = _hgrn(st_hg_f, st_hg_b, z_hg, lower_bounds, batch, n_x)
    x1 = _readout(x2d, o_f, o_b, pre[9], pre[8], h_f, h_b, z_hg, z_g, g_mx, row(rw_ln_w[l]), row(rw_ln_b[l]),
                  hg_norm_row, w_up_rw[l].astype(BF16), w_up_hg[l].astype(BF16), w_out[l].astype(BF16), seq)

    hx, cnt, rk, e1, e2 = _peer_sel(x1, row(norm_ffn[l]), sh_fx, sc_fx, peer_wq[l].astype(BF16), peer_k1[l],
                                    peer_k2[l], seq)
    out = _peer_dense(hx, u_bf, v_bf, cnt, rk, e1, e2, x1, g_fx, row(norm_final), seq)
    return out.reshape(batch, seq, dm)
```

```python
import functools

import jax
import jax.numpy as jnp
import numpy as np
from jax import lax
from jax.experimental import pallas as pl
from jax.experimental.pallas import tpu as pltpu

F32 = jnp.float32
BF16 = jnp.bfloat16

D_MODEL = 2048
N_MOD = 6
NORM_EPS = 1e-6
CHUNK = 64
SUB = 16
SEQ_CHUNKS = 4
RW_WIDTH = 1024
RW_HEAD = 64
RW_HEADS = 16
RW_LORA = 64
RW_GATE_RANK = 160
RW_COLS = 3488
RW_PAD = 3584
CONV_COLS = 512
NORM_ROWS = 128
RW_GN_EPS = 64e-5
HG_WIDTH = 1024
HG_HEAD = 128
HG_HEADS = 8
PEER_HEADS = 8
PEER_NKEYS = 128
PEER_TOPK = 16
BF16_ROWS = 16
MM1_TOKENS = 256
EXP_M05 = float(np.exp(-0.5))
VMEM_LIMIT = 56 * 1024 * 1024


def _cp(sem):
    return pltpu.CompilerParams(dimension_semantics=sem, vmem_limit_bytes=VMEM_LIMIT)


def _split(x):
    hi = x.astype(BF16)
    return hi, (x - hi.astype(F32)).astype(BF16)


def _dot(a, b, dims, exact):
    dg = lambda p, q: lax.dot_general(p, q, dims, preferred_element_type=F32)
    if not exact:
        return dg(a.astype(BF16), b.astype(BF16))
    a_hi, a_lo = _split(a)
    b_hi, b_lo = _split(b)
    return dg(a_hi, b_hi) + (dg(a_hi, b_lo) + dg(a_lo, b_hi))


def _mm(a, b, exact=False):
    return _dot(a, b, (((1,), (0,)), ((), ())), exact)


def _mm_nt(a, b, exact=False):
    return _dot(a, b, (((1,), (1,)), ((), ())), exact)


def _mm_tn(a, b, exact=False):
    return _dot(a, b, (((0,), (0,)), ((), ())), exact)


def _mm_mask(mask01, x):
    m = mask01.astype(BF16)
    hi = x.astype(BF16)
    mid, lo = _split(x - hi.astype(F32))
    dg = lambda q: jnp.dot(m, q, preferred_element_type=F32)
    return dg(hi) + (dg(mid) + dg(lo))


def _sigmoid(x):
    return 1.0 / (1.0 + jnp.exp(-x))


def _norm_mod(x, gain, shift, scale):
    y = x * lax.rsqrt(jnp.mean(x * x, axis=-1, keepdims=True) + NORM_EPS) * gain
    return y * (1.0 + scale) + shift


def _ada_kernel(c_ref, w_ref, b_ref, o_ref):
    cv = c_ref[...]
    o_ref[...] = _mm(cv * _sigmoid(cv), w_ref[...], exact=True) + b_ref[...]


def _ada(cvec, w, b, tn=512):
    k, n = w.shape
    return pl.pallas_call(
        _ada_kernel,
        out_shape=jax.ShapeDtypeStruct((cvec.shape[0], n), F32),
        grid=(n // tn,),
        in_specs=[pl.BlockSpec((cvec.shape[0], k), lambda j: (0, 0)),
                  pl.BlockSpec((k, tn), lambda j: (0, j)),
                  pl.BlockSpec((1, tn), lambda j: (0, j))],
        out_specs=pl.BlockSpec((cvec.shape[0], tn), lambda j: (0, j)),
        compiler_params=_cp(("parallel",)),
        name="ada",
    )(cvec, w, b)


def _regroup_kernel(w_ref, o_ref):
    w = w_ref[...]
    pad = jnp.zeros((w.shape[0], RW_PAD - RW_COLS), F32)
    o_ref[...] = jnp.concatenate([w[:, :RW_COLS], pad, w[:, RW_COLS:]], axis=1).astype(BF16)


def _regroup(w, layer, tk=128):
    _, k, n = w.shape
    n_out = n + RW_PAD - RW_COLS
    return pl.pallas_call(
        _regroup_kernel,
        out_shape=jax.ShapeDtypeStruct((k, n_out), BF16),
        grid=(k // tk,),
        in_specs=[pl.BlockSpec((None, tk, n), lambda i: (layer, i, 0))],
        out_specs=pl.BlockSpec((tk, n_out), lambda i: (i, 0)),
        compiler_params=_cp(("parallel",)),
        name="regroup",
    )(w)


def _inproj_kernel(x_ref, g_ref, sh_ref, sc_ref, w_ref, *refs, bounds):
    out_refs, h_scr = refs[:-1], refs[-1]
    j = pl.program_id(1)

    @pl.when(j == 0)
    def _():
        for r in range(0, h_scr.shape[0], NORM_ROWS):
            rs = slice(r, r + NORM_ROWS)
            h_scr[rs, :] = _norm_mod(x_ref[rs, :], g_ref[...], sh_ref[...], sc_ref[...]).astype(BF16)

    for (lo, hi), o_ref in zip(bounds, out_refs):
        @pl.when((j >= lo) & (j < hi))
        def _(o_ref=o_ref):
            o_ref[...] = jnp.dot(h_scr[...], w_ref[...], preferred_element_type=F32).astype(BF16)


def _inproj(x2d, gain, shift, scale, w, widths, rows_per_mod, tm=2048, tn=512):
    m, k = x2d.shape
    tm = min(tm, rows_per_mod)
    assert rows_per_mod % tm == 0 and m % tm == 0 and all(wd % tn == 0 for wd in widths)
    bpm = rows_per_mod // tm
    mod_map = lambda i, j: (i // bpm, 0, 0)
    edges = np.cumsum([0] + [wd // tn for wd in widths])
    bounds = [(int(lo), int(hi)) for lo, hi in zip(edges[:-1], edges[1:])]
    out_spec = lambda lo, hi: pl.BlockSpec((tm, tn), lambda i, j: (i, jnp.clip(j - lo, 0, hi - lo - 1)))
    return pl.pallas_call(
        functools.partial(_inproj_kernel, bounds=bounds),
        out_shape=[jax.ShapeDtypeStruct((m, wd), BF16) for wd in widths],
        grid=(m // tm, int(edges[-1])),
        in_specs=[pl.BlockSpec((tm, k), lambda i, j: (i, 0), pipeline_mode=pl.Buffered(1)),
                  pl.BlockSpec((1, k), lambda i, j: (0, 0)),
                  pl.BlockSpec((None, 1, k), mod_map),
                  pl.BlockSpec((None, 1, k), mod_map),
                  pl.BlockSpec((k, tn), lambda i, j: (0, j))],
        out_specs=[out_spec(lo, hi) for lo, hi in bounds],
        scratch_shapes=[pltpu.VMEM((tm, k), BF16)],
        compiler_params=_cp(("parallel", "arbitrary")),
        name="inproj",
    )(x2d, gain, shift, scale, w)


def _tri_masks(reverse):
    row = lax.broadcasted_iota(jnp.int32, (CHUNK, CHUNK), 0)
    col = lax.broadcasted_iota(jnp.int32, (CHUNK, CHUNK), 1)
    if reverse:
        strict, incl = col > row, col >= row
    else:
        strict, incl = col < row, col <= row
    same = (row // SUB) == (col // SUB)
    return strict, incl, same, row == col


def _rw_heads(ar_t, bk_t, a_0, r_0, bk_end, v_b, decay_tot, reverse, out_refs):
    strict, incl, same, eye = _tri_masks(reverse)
    row2 = lax.broadcasted_iota(jnp.int32, (CHUNK, 2 * CHUNK), 0)
    col2 = lax.broadcasted_iota(jnp.int32, (CHUNK, 2 * CHUNK), 1) % CHUNK
    incl2 = (col2 >= row2) if reverse else (col2 <= row2)
    ident = jnp.where(eye, 1.0, 0.0)
    heads = range(RW_HEADS)
    hs = lambda h: slice(h * RW_HEAD, (h + 1) * RW_HEAD)
    bf = lambda xs: [x.astype(BF16) for x in xs]
    mm = lambda xs, ys: [jnp.dot(x, y, preferred_element_type=F32) for x, y in zip(xs, ys)]
    nt_dims = (((1,), (1,)), ((), ()))
    tn_dims = (((0,), (0,)), ((), ()))

    s4 = [lax.dot_general(ar_t[:, hs(h)], bk_t[:, hs(h)], nt_dims, preferred_element_type=F32) for h in heads]
    m_ab = [jnp.where(strict, s[:CHUNK, :CHUNK], 0.0) for s in s4]
    m_ak = bf([jnp.where(strict, s[:CHUNK, CHUNK:], 0.0) for s in s4])
    m_r = bf([jnp.where(incl2, s[CHUNK:], 0.0) for s in s4])

    m_bd = [jnp.where(same, m, 0.0) for m in m_ab]
    m_off = bf([jnp.where(same, 0.0, m) for m in m_ab])
    m1 = bf(m_bd)
    m2 = bf(mm(m1, m1))
    m4 = bf(mm(m2, m2))
    m8 = bf(mm(m4, m4))
    p = [ident + m for m in m_bd]
    for mk in (m2, m4, m8):
        p = [x + y for x, y in zip(p, mm(bf(p), mk))]
    p_b = bf(p)
    n1 = mm(p_b, m_off)
    n1_b = bf(n1)
    n2 = mm(n1_b, n1_b)
    n3 = mm(n1_b, bf(n2))
    t_inv = bf(mm(bf([ident + a + b + c for a, b, c in zip(n1, n2, n3)]), p_b))

    x = bf(mm(m_ak, [v_b[:, hs(h)] for h in heads]))
    wu = bf(mm(t_inv, [jnp.concatenate([a_0[:, hs(h)], xh], axis=1) for h, xh in zip(heads, x)]))
    zero = jnp.zeros((CHUNK, RW_HEAD), BF16)
    rhs = [jnp.concatenate([w, jnp.concatenate([zero, v_b[:, hs(h)]], axis=1)], axis=0) for h, w in zip(heads, wu)]
    qo = mm(m_r, rhs)
    gd = [lax.dot_general(bk_end[:, hs(h)], r, tn_dims, preferred_element_type=F32) for h, r in zip(heads, rhs)]
    q_ref, o_ref, g_ref, d_ref = out_refs
    for h in heads:
        q_ref[:, hs(h)] = r_0[:, hs(h)] + qo[h][:, :RW_HEAD]
        o_ref[:, hs(h)] = qo[h][:, RW_HEAD:]
        g_ref[:, hs(h)] = gd[h][:, :RW_HEAD] + jnp.where(eye, decay_tot[:, hs(h)], 0.0)
        d_ref[:, hs(h)] = gd[h][:, RW_HEAD:]


def _rwkv_pre_kernel(zp_ref, zc_ref, zn_ref, conv_ref, w0_ref, wl_ref, a0_ref, al_ref, gl_ref, kk_ref, ka_ref,
                     rk_ref, qf_ref, of_ref, gf_ref, df_ref, qb_ref, ob_ref, gb_ref, db_ref, g_ref, bonus_ref,
                     conv_s, *, is_ctx, nchunks):
    c = pl.program_id(1)
    row = lax.broadcasted_iota(jnp.int32, (CHUNK, 1), 0)
    has_p = (c > 0).astype(F32)
    has_n = (c < nchunks - 1).astype(F32)

    def left(z, fill):
        return jnp.where(row == 0, fill, pltpu.roll(z, 1, 0))

    def right(z, fill):
        return jnp.where(row == CHUNK - 1, fill, pltpu.roll(z, CHUNK - 1, 0))

    for cb in range(RW_PAD // CONV_COLS):
        cs = slice(cb * CONV_COLS, (cb + 1) * CONV_COLS)
        w = conv_ref[:, cs]
        zc = zc_ref[:, cs].astype(F32)
        if is_ctx:
            prev_row = zp_ref[CHUNK - BF16_ROWS:CHUNK, cs].astype(F32)[BF16_ROWS - 1:BF16_ROWS] * has_p
            next_row = zn_ref[0:BF16_ROWS, cs].astype(F32)[0:1] * has_n
            conv = left(zc, prev_row) * w[3:4] + zc * w[4:5] + right(zc, next_row) * w[5:6]
        else:
            zp = zp_ref[:, cs].astype(F32) * has_p
            zn = zn_ref[:, cs].astype(F32) * has_n
            col = lambda c: zp * w[c:c + 1] + zc * w[3 + c:4 + c] + zn * w[6 + c:7 + c]
            conv = left(col(0), 0.0) + col(1) + right(col(2), 0.0)
        conv_s[:, cs] = conv

    r = conv_s[:, 0:RW_WIDTH]
    k = conv_s[:, RW_WIDTH:2 * RW_WIDTH]
    v = conv_s[:, 2 * RW_WIDTH:3 * RW_WIDTH]
    lo = conv_s[:, 3 * RW_WIDTH:RW_PAD]
    g_ref[...] = _mm(_sigmoid(lo[:, 4 * RW_LORA:]), gl_ref[...])

    first = lax.broadcasted_iota(jnp.int32, (1, 2 * RW_HEAD), 1) < RW_HEAD

    def head_sum(t):
        tiles = []
        for i in range(RW_WIDTH // (2 * RW_HEAD)):
            ti = t[:, i * 2 * RW_HEAD:(i + 1) * 2 * RW_HEAD]
            s0 = jnp.sum(jnp.where(first, ti, 0.0), axis=-1, keepdims=True)
            s1 = jnp.sum(jnp.where(first, 0.0, ti), axis=-1, keepdims=True)
            tiles.append(jnp.where(first, s0, s1))
        return jnp.concatenate(tiles, axis=1)

    kkr = k * kk_ref[...]
    kk = kkr / jnp.maximum(jnp.sqrt(head_sum(kkr * kkr)), 1e-12)
    v_b = v.astype(BF16)
    outs = ((qf_ref, of_ref, gf_ref, df_ref), (qb_ref, ob_ref, gb_ref, db_ref))
    trow = lax.broadcasted_iota(jnp.int32, (CHUNK, CHUNK), 0)
    tcol = lax.broadcasted_iota(jnp.int32, (CHUNK, CHUNK), 1)
    bonus = None
    for d in range(2):
        wa = lo[:, d * RW_LORA:(d + 1) * RW_LORA]
        aa = lo[:, (2 + d) * RW_LORA:(3 + d) * RW_LORA]
        w_pre = w0_ref[d:d + 1, :] + _mm(jnp.tanh(wa), wl_ref[d])
        logw = -EXP_M05 * _sigmoid(w_pre)
        a = _sigmoid(a0_ref[d:d + 1, :] + _mm(aa, al_ref[d]))
        tri = jnp.where((tcol >= trow) if d == 1 else (tcol <= trow), 1.0, 0.0)
        cum = _mm_mask(tri, logw)
        prev = cum - logw
        mid = CHUNK // 2 - 1 if d == 1 else CHUNK // 2
        end = 0 if d == 1 else CHUNK - 1
        rho = cum[mid:mid + 1]
        tot = cum[end:end + 1]
        kd = k * (1.0 + (a - 1.0) * ka_ref[...])
        kka = kk * a
        hb = head_sum(r * kd * rk_ref[...])
        bonus = hb if bonus is None else bonus + hb
        e_out = jnp.exp(rho - cum)
        e_end = jnp.exp(tot - cum)
        ar_t = jnp.concatenate([-kk * jnp.exp(prev - rho), r * jnp.exp(cum - rho)], axis=0).astype(BF16)
        bk_t = jnp.concatenate([kka * e_out, kd * e_out], axis=0).astype(BF16)
        bk_end = jnp.concatenate([kka * e_end, kd * e_end], axis=0).astype(BF16)
        a_0 = (-kk * jnp.exp(prev)).astype(BF16)
        r_0 = r * jnp.exp(cum)
        _rw_heads(ar_t, bk_t, a_0, r_0, bk_end, v_b, jnp.exp(tot), d == 1, outs[d])
    bonus_ref[...] = bonus * v


def _rwkv_pre(z_rw, conv_w, w0, wl, a0, al, gl, k_k, k_a, r_k, batch, nchunks, is_ctx):
    rows = z_rw.shape[0]
    last = nchunks - 1
    zspec = lambda f: pl.BlockSpec((CHUNK, RW_PAD), lambda b, c: (b * nchunks + f(c), 0))
    full = lambda shape: pl.BlockSpec(shape, lambda b, c: (0,) * len(shape))
    ospec = pl.BlockSpec((CHUNK, RW_WIDTH), lambda b, c: (b * nchunks + c, 0))
    out = jax.ShapeDtypeStruct((rows, RW_WIDTH), F32)
    return pl.pallas_call(
        functools.partial(_rwkv_pre_kernel, is_ctx=is_ctx, nchunks=nchunks),
        out_shape=[out] * 10,
        grid=(batch, nchunks),
        in_specs=[zspec(lambda c: jnp.maximum(c - 1, 0)), zspec(lambda c: c), zspec(lambda c: jnp.minimum(c + 1, last)),
                  full((9, RW_PAD)), full((2, RW_WIDTH)), full((2, RW_LORA, RW_WIDTH)), full((2, RW_WIDTH)),
                  full((2, RW_LORA, RW_WIDTH)), full((RW_PAD - 3 * RW_WIDTH - 4 * RW_LORA, RW_WIDTH)), full((1, RW_WIDTH)),
                  full((1, RW_WIDTH)), full((1, RW_WIDTH))],
        out_specs=[ospec] * 10,
        scratch_shapes=[pltpu.VMEM((CHUNK, RW_PAD), F32)],
        compiler_params=_cp(("parallel", "parallel")),
        name="rwkv_pre",
    )(z_rw, z_rw, z_rw, conv_w, w0, wl, a0, al, gl, k_k, k_a, r_k)


def _rwkv_seq_kernel(x0f_ref, x0b_ref, qf_ref, of_ref, gf_ref, df_ref, qb_ref, ob_ref, gb_ref, db_ref,
                     outf_ref, outb_ref, xff_ref, xfb_ref, x_s, *, nsteps):
    c = pl.program_id(1)

    @pl.when(c == 0)
    def _():
        x_s[0] = x0f_ref[...]
        x_s[1] = x0b_ref[...]

    dirs = ((qf_ref, of_ref, gf_ref, df_ref, outf_ref), (qb_ref, ob_ref, gb_ref, db_ref, outb_ref))
    hs = lambda h: slice(h * RW_HEAD, (h + 1) * RW_HEAD)
    for j in range(SEQ_CHUNKS):
        for d, (q_ref, o_ref, g_ref, d_ref, out_ref) in enumerate(dirs):
            jj = SEQ_CHUNKS - 1 - j if d == 1 else j
            rows = slice(jj * CHUNK, (jj + 1) * CHUNK)
            res = [_mm(jnp.concatenate([q_ref[rows, hs(h)], g_ref[rows, hs(h)]], axis=0), x_s[d, h], exact=True)
                   for h in range(RW_HEADS)]
            for h in range(RW_HEADS):
                out_ref[rows, hs(h)] = res[h][:CHUNK] + o_ref[rows, hs(h)]
                x_s[d, h] = res[h][CHUNK:] + d_ref[rows, hs(h)]

    @pl.when(c == nsteps - 1)
    def _():
        xff_ref[...] = x_s[0]
        xfb_ref[...] = x_s[1]


def _rwkv_seq(x0f, x0b, pre, batch, nchunks):
    nsteps = nchunks // SEQ_CHUNKS
    blk = (SEQ_CHUNKS * CHUNK, RW_WIDTH)
    fspec = pl.BlockSpec(blk, lambda b, c: (b * nsteps + c, 0))
    bspec = pl.BlockSpec(blk, lambda b, c: (b * nsteps + nsteps - 1 - c, 0))
    sspec = pl.BlockSpec((None, RW_HEADS, RW_HEAD, RW_HEAD), lambda b, c: (b, 0, 0, 0))
    seq = jax.ShapeDtypeStruct(pre[0].shape, F32)
    state = jax.ShapeDtypeStruct(x0f.shape, F32)
    return pl.pallas_call(
        functools.partial(_rwkv_seq_kernel, nsteps=nsteps),
        out_shape=[seq, seq, state, state],
        grid=(batch, nsteps),
        in_specs=[sspec, sspec] + [fspec] * 4 + [bspec] * 4,
        out_specs=[fspec, bspec, sspec, sspec],
        scratch_shapes=[pltpu.VMEM((2, RW_HEADS, RW_HEAD, RW_HEAD), F32)],
        compiler_params=_cp(("parallel", "arbitrary")),
        name="rwkv_seq",
    )(x0f, x0b, *pre)


def _hgrn_prep(d, q_ref, f_ref, i_ref, lb, rows):
    reverse = d == 1
    row = lax.broadcasted_iota(jnp.int32, (CHUNK, CHUNK), 0)
    col = lax.broadcasted_iota(jnp.int32, (CHUNK, CHUNK), 1)
    incl = (col >= row) if reverse else (col <= row)
    mid = CHUNK // 2 - 1 if reverse else CHUNK // 2
    end = 0 if reverse else CHUNK - 1

    forget = lb + (1.0 - lb) * _sigmoid(f_ref[rows, :].astype(F32))
    kf = 1.0 - forget
    qraw = q_ref[rows, :].astype(F32)
    q = qraw * _sigmoid(qraw)
    v = i_ref[rows, :].astype(F32)
    cum = _mm_mask(jnp.where(incl, 1.0, 0.0), jnp.log(forget))
    ref = cum[mid:mid + 1]
    tot = cum[end:end + 1]
    bf = lambda t: t.astype(BF16)
    return dict(incl=incl, qs=bf(q * jnp.exp(cum - ref)), ks=bf(kf * jnp.exp(ref - cum)), qe=bf(q * jnp.exp(cum)),
                ke=bf(kf * jnp.exp(tot - cum)), v=bf(v), dec=jnp.exp(tot))


def _hgrn_heads(st_s, preps, out_refs, rows):
    for h in range(HG_HEADS):
        sl = slice(h * HG_HEAD, (h + 1) * HG_HEAD)
        for d, (p, out_ref, rw) in enumerate(zip(preps, out_refs, rows)):
            st = st_s[d, h]
            scores = jnp.where(p["incl"], _mm_nt(p["qs"][:, sl], p["ks"][:, sl]), 0.0)
            out_ref[rw, sl] = _mm(scores, p["v"][:, sl]) + _mm_nt(p["qe"][:, sl], st)
            st_s[d, h] = st * p["dec"][:, sl] + _mm_tn(p["v"][:, sl], p["ke"][:, sl])


def _hgrn_kernel(s0f_ref, s0b_ref, qf_ref, ff_ref, if_ref, qb_ref, fb_ref, ib_ref, lb_ref, outf_ref, outb_ref,
                 sff_ref, sfb_ref, st_s, *, nsteps):
    c = pl.program_id(1)

    @pl.when(c == 0)
    def _():
        st_s[0] = s0f_ref[...]
        st_s[1] = s0b_ref[...]

    for j in range(SEQ_CHUNKS):
        jj = SEQ_CHUNKS - 1 - j
        rows = (slice(j * CHUNK, (j + 1) * CHUNK), slice(jj * CHUNK, (jj + 1) * CHUNK))
        preps = (_hgrn_prep(0, qf_ref, ff_ref, if_ref, lb_ref[0:1, :], rows[0]),
                 _hgrn_prep(1, qb_ref, fb_ref, ib_ref, lb_ref[1:2, :], rows[1]))
        _hgrn_heads(st_s, preps, (outf_ref, outb_ref), rows)

    @pl.when(c == nsteps - 1)
    def _():
        sff_ref[...] = st_s[0]
        sfb_ref[...] = st_s[1]


def _hgrn(s0f, s0b, z_hg, lb, batch, nchunks):
    nsteps = nchunks // SEQ_CHUNKS
    blk = (SEQ_CHUNKS * CHUNK, HG_WIDTH)
    fspec = lambda col: pl.BlockSpec(blk, lambda b, c: (b * nsteps + c, col))
    bspec = lambda col: pl.BlockSpec(blk, lambda b, c: (b * nsteps + nsteps - 1 - c, col))
    sspec = pl.BlockSpec((None, HG_HEADS, HG_HEAD, HG_HEAD), lambda b, c: (b, 0, 0, 0))
    seq = jax.ShapeDtypeStruct((z_hg.shape[0], HG_WIDTH), F32)
    state = jax.ShapeDtypeStruct(s0f.shape, F32)
    return pl.pallas_call(
        functools.partial(_hgrn_kernel, nsteps=nsteps),
        out_shape=[seq, seq, state, state],
        grid=(batch, nsteps),
        in_specs=[sspec, sspec, fspec(0), fspec(1), fspec(3), bspec(0), bspec(2), bspec(3),
                  pl.BlockSpec((2, HG_WIDTH), lambda b, c: (0, 0))],
        out_specs=[fspec(0), bspec(0), sspec, sspec],
        scratch_shapes=[pltpu.VMEM((2, HG_HEADS, HG_HEAD, HG_HEAD), F32)],
        compiler_params=_cp(("parallel", "arbitrary")),
        name="hgrn",
    )(s0f, s0b, z_hg, z_hg, z_hg, z_hg, z_hg, z_hg, lb)


def _readout_kernel(x_ref, of_ref, ob_ref, bonus_ref, g_ref, hf_ref, hb_ref, og_ref, grw_ref, ghg_ref, gmx_ref,
                    lnw_ref, lnb_ref, hgn_ref, wrw_ref, whg_ref, wout_ref, o_ref, y_s):
    first = lax.broadcasted_iota(jnp.int32, (1, 2 * RW_HEAD), 1) < RW_HEAD
    tiles = [slice(i * 2 * RW_HEAD, (i + 1) * 2 * RW_HEAD) for i in range(RW_WIDTH // (2 * RW_HEAD))]

    def head_mean(ts):
        s0 = [jnp.sum(jnp.where(first, t, 0.0), axis=-1, keepdims=True) for t in ts]
        s1 = [jnp.sum(jnp.where(first, 0.0, t), axis=-1, keepdims=True) for t in ts]
        return [jnp.where(first, a, b) * (1.0 / RW_HEAD) for a, b in zip(s0, s1)]

    o_rw = [of_ref[:, sl] + ob_ref[:, sl] for sl in tiles]
    dv = [o - mu for o, mu in zip(o_rw, head_mean(o_rw))]
    var = head_mean([d * d for d in dv])
    for sl, d, vr in zip(tiles, dv, var):
        y_s[:, sl] = d * lax.rsqrt(vr + RW_GN_EPS)
    y_rw = (y_s[...] * lnw_ref[...] + lnb_ref[...] + bonus_ref[...]) * g_ref[...]
    branch_rw = _mm(y_rw, wrw_ref[...])

    hg_tiles = [slice(h * HG_HEAD, (h + 1) * HG_HEAD) for h in range(HG_HEADS)]
    o_hg = [hf_ref[:, sl] + hb_ref[:, sl] for sl in hg_tiles]
    ms = [jnp.mean(o * o, axis=-1, keepdims=True) for o in o_hg]
    for sl, o, m2 in zip(hg_tiles, o_hg, ms):
        y_s[:, sl] = o * lax.rsqrt(m2 + NORM_EPS)
    og = og_ref[...].astype(F32)
    y_hg = y_s[...] * hgn_ref[...] * (og * _sigmoid(og))
    branch_hg = _mm(y_hg, whg_ref[...])

    merged = _sigmoid(grw_ref[...].astype(F32)) * branch_rw + _sigmoid(ghg_ref[...].astype(F32)) * branch_hg
    o_ref[...] = x_ref[...] + gmx_ref[...] * _mm(merged, wout_ref[...])


def _readout(x2d, o_f, o_b, bonus, g, h_f, h_b, z_hg, z_g, g_mx, ln_w, ln_b, hg_norm, w_rw, w_hg, w_out,
             rows_per_mod, tm=256):
    m, dm = x2d.shape
    bpm = rows_per_mod // tm
    tok = lambda width, blk=0: pl.BlockSpec((tm, width), lambda i: (i, blk))
    const = lambda shape: pl.BlockSpec(shape, lambda i: (0,) * len(shape), pipeline_mode=pl.Buffered(1))
    return pl.pallas_call(
        _readout_kernel,
        out_shape=jax.ShapeDtypeStruct((m, dm), F32),
        grid=(m // tm,),
        in_specs=[tok(dm), tok(RW_WIDTH), tok(RW_WIDTH), tok(RW_WIDTH), tok(RW_WIDTH), tok(HG_WIDTH), tok(HG_WIDTH),
                  tok(HG_WIDTH, 4), tok(dm, 0), tok(dm, 1),
                  pl.BlockSpec((None, 1, dm), lambda i: (i // bpm, 0, 0)),
                  const((1, RW_WIDTH)), const((1, RW_WIDTH)), const((1, HG_WIDTH)),
                  const((RW_WIDTH, dm)), const((HG_WIDTH, dm)), const((dm, dm))],
        out_specs=tok(dm),
        scratch_shapes=[pltpu.VMEM((tm, RW_WIDTH), F32)],
        compiler_params=_cp(("parallel",)),
        name="readout",
    )(x2d, o_f, o_b, bonus, g, h_f, h_b, z_hg, z_g, z_g, g_mx, ln_w, ln_b, hg_norm, w_rw, w_hg, w_out)


def _cand_groups():
    return [(i, PEER_TOPK // (i + 1)) for i in range(PEER_TOPK)]


def _pick_max(work, io, first_only):
    m = jnp.max(work, axis=0, keepdims=True)
    sel = work == m
    if first_only:
        sel = io == jnp.min(jnp.where(sel, io, work.shape[0]), axis=0, keepdims=True)
    return m, sel


def _topk_ranks(s, vals_ref, first_only, need_rank=True):
    io = lax.broadcasted_iota(jnp.int32, s.shape, 0)

    if first_only:
        def body(r, carry):
            work, rank = carry
            m, sel = _pick_max(work, io, True)
            vals_ref[pl.ds(r, 1), :] = m
            return jnp.where(sel, -jnp.inf, work), jnp.where(sel, lax.convert_element_type(r, F32), rank)

        _, rank = lax.fori_loop(0, PEER_TOPK, body, (s, jnp.full(s.shape, float(PEER_TOPK), F32)))
    else:
        def body(r, work):
            m, sel = _pick_max(work, io, False)
            vals_ref[pl.ds(r, 1), :] = m
            return jnp.where(sel, -jnp.inf, work)

        lax.fori_loop(0, PEER_TOPK, body, s)
        if not need_rank:
            taken = jnp.sum(jnp.where(s >= vals_ref[PEER_TOPK - 1:PEER_TOPK, :], 1.0, 0.0), axis=0, keepdims=True)
            return None, jnp.where(taken == float(PEER_TOPK), 0.0, 1.0)
        rank = jnp.zeros(s.shape, F32)
        for r in range(PEER_TOPK):
            rank = rank + jnp.where(vals_ref[r:r + 1, :] > s, 1.0, 0.0)
    taken = jnp.sum(jnp.where(rank < float(PEER_TOPK), 1.0, 0.0), axis=0, keepdims=True)
    return rank, jnp.where(taken == float(PEER_TOPK), 0.0, 1.0)


def _peer_sel_kernel(x_ref, gain_ref, sh_ref, sc_ref, wq_ref, k1_ref, k2_ref, hx_ref, cnt_ref, rk_ref, e1_ref,
                     e2_ref, q_s, v1_s, v2_s):
    hb = _norm_mod(x_ref[...], gain_ref[...], sh_ref[...], sc_ref[...]).astype(BF16)
    hx_ref[...] = hb
    q_s[...] = jnp.dot(hb, wq_ref[...], preferred_element_type=F32)
    tm = q_s.shape[0]
    groups = _cand_groups()
    ncand = sum(n for _, n in groups)
    npad = -ncand % 8

    def select(hd, first_only):
        q1 = q_s[:, (2 * hd) * PEER_NKEYS:(2 * hd + 1) * PEER_NKEYS]
        q2 = q_s[:, (2 * hd + 1) * PEER_NKEYS:(2 * hd + 2) * PEER_NKEYS]
        s1 = _mm_nt(k1_ref[...], q1, exact=True)
        s2 = _mm_nt(k2_ref[...], q2, exact=True)
        rank1, bad1 = _topk_ranks(s1, v1_s, first_only, need_rank=first_only)
        rank2, bad2 = _topk_ranks(s2, v2_s, first_only)
        v1 = v1_s[...]
        v2 = v2_s[...]
        rows = [v1[i:i + 1] + v2[0:n] for i, n in groups]
        if npad:
            rows.append(jnp.full((npad, tm), -jnp.inf, F32))
        cand = jnp.concatenate(rows, axis=0)
        io = lax.broadcasted_iota(jnp.int32, cand.shape, 0)
        best0 = v1[0:1] + v2[0:1]

        def body(r, carry):
            work, z = carry
            m, sel = _pick_max(work, io, first_only)
            return jnp.where(sel, -jnp.inf, work), z + jnp.exp(m - best0)

        work, z = lax.fori_loop(0, PEER_TOPK, body, (cand, jnp.zeros((1, tm), F32)))
        taken = jnp.where((work == -jnp.inf) & (io < ncand), 1.0, 0.0)
        bad3 = jnp.where(jnp.sum(taken, axis=0, keepdims=True) == float(PEER_TOPK), 0.0, 1.0)
        counts = []
        start = 0
        for i, n in groups:
            counts.append(jnp.sum(taken[start:start + n], axis=0, keepdims=True))
            start += n
        cnt1 = jnp.zeros_like(s1)
        for i, c_i in enumerate(counts):
            if first_only:
                cnt1 = cnt1 + jnp.where(rank1 == float(i), c_i, 0.0)
            else:
                step = c_i - counts[i + 1] if i + 1 < len(counts) else c_i
                cnt1 = cnt1 + jnp.where(s1 >= v1[i:i + 1], step, 0.0)
        cnt_ref[hd] = cnt1
        rk_ref[hd] = rank2.astype(BF16)
        e1_ref[hd] = jnp.exp(s1 - v1[0:1])
        e2_ref[hd] = (jnp.exp(s2 - v2[0:1]) / z).astype(BF16)
        return jnp.maximum(bad1, jnp.maximum(bad2, bad3))

    for hd in range(PEER_HEADS):
        bad = select(hd, False)

        @pl.when(jnp.max(bad) > 0.5)
        def _(hd=hd):
            select(hd, True)


def _peer_sel(x2d, gain, shift, scale, wq, k1, k2, rows_per_mod, tm=256):
    m, dm = x2d.shape
    bpm = rows_per_mod // tm
    mod_map = lambda i: (i // bpm, 0, 0)
    const = lambda shape: pl.BlockSpec(shape, lambda i: (0,) * len(shape))
    sel = jax.ShapeDtypeStruct((PEER_HEADS, PEER_NKEYS, m), F32)
    sel_bf = jax.ShapeDtypeStruct((PEER_HEADS, PEER_NKEYS, m), BF16)
    sel_spec = pl.BlockSpec((PEER_HEADS, PEER_NKEYS, tm), lambda i: (0, 0, i))
    return pl.pallas_call(
        _peer_sel_kernel,
        out_shape=[jax.ShapeDtypeStruct((m, dm), BF16), sel, sel_bf, sel, sel_bf],
        grid=(m // tm,),
        in_specs=[pl.BlockSpec((tm, dm), lambda i: (i, 0)), const((1, dm)),
                  pl.BlockSpec((None, 1, dm), mod_map), pl.BlockSpec((None, 1, dm), mod_map),
                  const(wq.shape), const(k1.shape), const(k2.shape)],
        out_specs=[pl.BlockSpec((tm, dm), lambda i: (i, 0)), sel_spec, sel_spec, sel_spec, sel_spec],
        scratch_shapes=[pltpu.VMEM((tm, dm), F32), pltpu.VMEM((PEER_TOPK, tm), F32),
                        pltpu.VMEM((PEER_TOPK, tm), F32)],
        compiler_params=_cp(("parallel",)),
        name="peer_sel",
    )(x2d, gain, shift, scale, wq, k1, k2)


def _gelu(x):
    return 0.5 * x * (1.0 + lax.erf(x * float(np.sqrt(0.5))))


def _peer_dense_kernel(hx_ref, u_ref, v_ref, cnt_ref, e1_ref, cnt_nxt_ref, e1_nxt_ref, rk_ref, e2_ref, x_ref, gfx_ref,
                       gain_ref, o_ref, w_s, ga_s, gb_s, *, te, nblk):
    e = pl.program_id(1)
    half = te // 2
    nsub = half // PEER_NKEYS
    tm = hx_ref.shape[0]

    def gates(cnt_blk, e1_blk, first, dst):
        packed = (PEER_NKEYS // BF16_ROWS, BF16_ROWS, tm)
        for ii in range(nsub):
            gate = None
            for hd in range(PEER_HEADS):
                cnt = jnp.broadcast_to(cnt_blk[hd, first + ii:first + ii + 1, :], packed[1:]).astype(BF16)
                e1 = jnp.broadcast_to(e1_blk[hd, first + ii:first + ii + 1, :], packed[1:]).astype(BF16)
                t = jnp.where(rk_ref[hd].reshape(packed) < cnt[None], e2_ref[hd].reshape(packed) * e1[None],
                              jnp.zeros((), BF16))
                gate = t if gate is None else gate + t
            dst[ii * PEER_NKEYS:(ii + 1) * PEER_NKEYS, :] = gate.reshape(PEER_NKEYS, tm)

    @pl.when(e == 0)
    def _():
        o_ref[...] = jnp.zeros_like(o_ref)
        gates(cnt_ref, e1_ref, 0, ga_s)

    nt_dims = (((1,), (1,)), ((), ()))
    pieces = tm // MM1_TOKENS

    def weighted(u_rows, g_s):
        for t in range(pieces):
            ts = slice(t * MM1_TOKENS, (t + 1) * MM1_TOKENS)
            act = _gelu(lax.dot_general(u_ref[u_rows, :], hx_ref[ts, :], nt_dims, preferred_element_type=F32))
            w_s[ts, u_rows] = (g_s[:, ts] * act.astype(BF16)).T

    gates(cnt_ref, e1_ref, nsub, gb_s)
    weighted(slice(0, half), ga_s)
    weighted(slice(half, te), gb_s)
    gates(cnt_nxt_ref, e1_nxt_ref, 0, ga_s)
    o_ref[...] += jnp.dot(w_s[...], v_ref[...], preferred_element_type=F32)

    @pl.when(e == nblk - 1)
    def _():
        x = x_ref[...] + gfx_ref[...] * o_ref[...]
        o_ref[...] = x * lax.rsqrt(jnp.mean(x * x, axis=-1, keepdims=True) + NORM_EPS) * gain_ref[...]


def _peer_dense(hx, u, v, cnt, rk, e1, e2, x2d, g_fx, gain, rows_per_mod, tm=512, te=1024):
    m, dm = hx.shape
    ne = u.shape[0]
    nblk = ne // te
    bpm = rows_per_mod // tm
    sel_spec = pl.BlockSpec((PEER_HEADS, PEER_NKEYS, tm), lambda i, e: (0, 0, i))
    tok = pl.BlockSpec((tm, dm), lambda i, e: (i, 0))
    tab = pl.BlockSpec((te, dm), lambda i, e: (e, 0))
    key1_spec = lambda blk: pl.BlockSpec((PEER_HEADS, te // PEER_NKEYS, tm), lambda i, e: (0, blk(e), i))
    return pl.pallas_call(
        functools.partial(_peer_dense_kernel, te=te, nblk=nblk),
        out_shape=jax.ShapeDtypeStruct((m, dm), F32),
        grid=(m // tm, nblk),
        in_specs=[tok, tab, tab, key1_spec(lambda e: e), key1_spec(lambda e: e),
                  key1_spec(lambda e: (e + 1) % nblk), key1_spec(lambda e: (e + 1) % nblk), sel_spec, sel_spec, tok,
                  pl.BlockSpec((None, 1, dm), lambda i, e: (i // bpm, 0, 0)),
                  pl.BlockSpec((1, dm), lambda i, e: (0, 0))],
        out_specs=tok,
        scratch_shapes=[pltpu.VMEM((tm, te), BF16)] + [pltpu.VMEM((te // 2, tm), BF16)] * 2,
        compiler_params=_cp(("parallel", "arbitrary")),
        name="peer_dense",
    )(hx, u, v, cnt, e1, cnt, e1, rk, e2, x2d, g_fx, gain)


def kernel(x, c, ctx, c_ctx, w_ada, b_ada, norm_mix, norm_ffn, norm_final, w_in, rw_conv, rw_w0, rw_w_lora_b, rw_a0, rw_a_lora_b, rw_g_lora_b, rw_k_k, rw_k_a, rw_r_k, rw_ln_w, rw_ln_b, hg_lb, hg_norm, w_up_rw, w_up_hg, w_out, peer_wq, peer_k1, peer_k2, peer_u, peer_v):
    batch, seq, dm = x.shape
    ctx_len = ctx.shape[1]
    depth = w_in.shape[0]
    assert depth == 1 and dm == D_MODEL
    assert seq % (CHUNK * SEQ_CHUNKS) == 0 and ctx_len % (CHUNK * SEQ_CHUNKS) == 0
    n_x = seq // CHUNK
    n_c = ctx_len // CHUNK
    l = 0

    w_in_p = _regroup(w_in, l)
    conv_w = jnp.pad(rw_conv[l].reshape(9, RW_COLS), ((0, 0), (0, RW_PAD - RW_COLS)))
    lower_bounds = jnp.cumsum(jax.nn.softmax(hg_lb.astype(F32), axis=0), axis=0)[l]
    row = lambda t: t.reshape(1, -1)
    hg_norm_row = jnp.tile(hg_norm[l], HG_HEADS).reshape(1, HG_WIDTH)
    u_bf = peer_u[l].astype(BF16)
    v_bf = peer_v[l].astype(BF16)

    cvec = jnp.concatenate([c, c_ctx[None, :], jnp.zeros((8 - batch - 1, dm), F32)], axis=0)
    mod = _ada(cvec, w_ada[l], row(b_ada[l]))
    mods = mod.reshape(8, N_MOD, 1, dm)
    sh_mx, sc_mx, g_mx, sh_fx, sc_fx, g_fx = (mods[:batch, i] for i in range(N_MOD))
    sh_mc, sc_mc = mods[batch:batch + 1, 0], mods[batch:batch + 1, 1]

    x2d = x.reshape(batch * seq, dm)
    c2d = ctx.reshape(batch * ctx_len, dm)
    gain_mix = row(norm_mix[l])

    zc_rw, zc_hg = _inproj(c2d, gain_mix, sh_mc, sc_mc, w_in_p, (RW_PAD, 5 * HG_WIDTH), batch * ctx_len)
    g_lora = jnp.pad(rw_g_lora_b[l], ((0, RW_PAD - RW_COLS), (0, 0)))
    rw_args = (conv_w, rw_w0[l], rw_w_lora_b[l], rw_a0[l], rw_a_lora_b[l], g_lora, row(rw_k_k[l]),
               row(rw_k_a[l]), row(rw_r_k[l]))
    pre_c = _rwkv_pre(zc_rw, *rw_args, batch=batch, nchunks=n_c, is_ctx=True)
    zero_rw = jnp.zeros((batch, RW_HEADS, RW_HEAD, RW_HEAD), F32)
    zero_hg = jnp.zeros((batch, HG_HEADS, HG_HEAD, HG_HEAD), F32)
    _, _, st_rw_f, st_rw_b = _rwkv_seq(zero_rw, zero_rw, pre_c[0:8], batch, n_c)
    _, _, st_hg_f, st_hg_b = _hgrn(zero_hg, zero_hg, zc_hg, lower_bounds, batch, n_c)

    z_rw, z_hg, z_g = _inproj(x2d, gain_mix, sh_mx, sc_mx, w_in_p, (RW_PAD, 5 * HG_WIDTH, 2 * dm), seq)
    pre = _rwkv_pre(z_rw, *rw_args, batch=batch, nchunks=n_x, is_ctx=False)
    o_f, o_b, _, _ = _rwkv_seq(st_rw_f, st_rw_b, pre[0:8], batch, n_x)
    h_f, h_b, _, _ = _hgrn(st_hg_f, st_hg_b, z_hg, lower_bounds, batch, n_x)
    x1 = _readout(x2d, o_f, o_b, pre[9], pre[8], h_f, h_b, z_hg, z_g, g_mx, row(rw_ln_w[l]), row(rw_ln_b[l]),
                  hg_norm_row, w_up_rw[l].astype(BF16), w_up_hg[l].astype(BF16), w_out[l].astype(BF16), seq)

    hx, cnt, rk, e1, e2 = _peer_sel(x1, row(norm_ffn[l]), sh_fx, sc_fx, peer_wq[l].astype(BF16), peer_k1[l],
                                    peer_k2[l], seq)
    out = _peer_dense(hx, u_bf, v_bf, cnt, rk, e1, e2, x1, g_fx, row(norm_final), seq)
    return out.reshape(batch, seq, dm)
```

```python
import functools

import jax
import jax.numpy as jnp
import numpy as np
from jax import lax
from jax.experimental import pallas as pl
from jax.experimental.pallas import tpu as pltpu

F32 = jnp.float32
BF16 = jnp.bfloat16

D_MODEL = 2048
N_MOD = 6
NORM_EPS = 1e-6
CHUNK = 64
SUB = 16
SEQ_CHUNKS = 4
RW_WIDTH = 1024
RW_HEAD = 64
RW_HEADS = 16
RW_LORA = 64
RW_GATE_RANK = 160
RW_COLS = 3488
RW_PAD = 3584
CONV_COLS = 512
NORM_ROWS = 128
RW_GN_EPS = 64e-5
HG_WIDTH = 1024
HG_HEAD = 128
HG_HEADS = 8
PEER_HEADS = 8
PEER_NKEYS = 128
PEER_TOPK = 16
BF16_ROWS = 16
EXP_M05 = float(np.exp(-0.5))
VMEM_LIMIT = 56 * 1024 * 1024


def _cp(sem):
    return pltpu.CompilerParams(dimension_semantics=sem, vmem_limit_bytes=VMEM_LIMIT)


def _split(x):
    hi = x.astype(BF16)
    return hi, (x - hi.astype(F32)).astype(BF16)


def _dot(a, b, dims, exact):
    dg = lambda p, q: lax.dot_general(p, q, dims, preferred_element_type=F32)
    if not exact:
        return dg(a.astype(BF16), b.astype(BF16))
    a_hi, a_lo = _split(a)
    b_hi, b_lo = _split(b)
    return dg(a_hi, b_hi) + (dg(a_hi, b_lo) + dg(a_lo, b_hi))


def _mm(a, b, exact=False):
    return _dot(a, b, (((1,), (0,)), ((), ())), exact)


def _mm_nt(a, b, exact=False):
    return _dot(a, b, (((1,), (1,)), ((), ())), exact)


def _mm_tn(a, b, exact=False):
    return _dot(a, b, (((0,), (0,)), ((), ())), exact)


def _mm_mask(mask01, x):
    m = mask01.astype(BF16)
    hi = x.astype(BF16)
    mid, lo = _split(x - hi.astype(F32))
    dg = lambda q: jnp.dot(m, q, preferred_element_type=F32)
    return dg(hi) + (dg(mid) + dg(lo))


def _sigmoid(x):
    return 1.0 / (1.0 + jnp.exp(-x))


def _norm_mod(x, gain, shift, scale):
    y = x * lax.rsqrt(jnp.mean(x * x, axis=-1, keepdims=True) + NORM_EPS) * gain
    return y * (1.0 + scale) + shift


def _ada_kernel(c_ref, w_ref, b_ref, o_ref):
    cv = c_ref[...]
    o_ref[...] = _mm(cv * _sigmoid(cv), w_ref[...], exact=True) + b_ref[...]


def _ada(cvec, w, b, tn=512):
    k, n = w.shape
    return pl.pallas_call(
        _ada_kernel,
        out_shape=jax.ShapeDtypeStruct((cvec.shape[0], n), F32),
        grid=(n // tn,),
        in_specs=[pl.BlockSpec((cvec.shape[0], k), lambda j: (0, 0)),
                  pl.BlockSpec((k, tn), lambda j: (0, j)),
                  pl.BlockSpec((1, tn), lambda j: (0, j))],
        out_specs=pl.BlockSpec((cvec.shape[0], tn), lambda j: (0, j)),
        compiler_params=_cp(("parallel",)),
        name="ada",
    )(cvec, w, b)


def _regroup_kernel(w_ref, o_ref):
    w = w_ref[...]
    pad = jnp.zeros((w.shape[0], RW_PAD - RW_COLS), F32)
    o_ref[...] = jnp.concatenate([w[:, :RW_COLS], pad, w[:, RW_COLS:]], axis=1).astype(BF16)


def _regroup(w, layer, tk=128):
    _, k, n = w.shape
    n_out = n + RW_PAD - RW_COLS
    return pl.pallas_call(
        _regroup_kernel,
        out_shape=jax.ShapeDtypeStruct((k, n_out), BF16),
        grid=(k // tk,),
        in_specs=[pl.BlockSpec((None, tk, n), lambda i: (layer, i, 0))],
        out_specs=pl.BlockSpec((tk, n_out), lambda i: (i, 0)),
        compiler_params=_cp(("parallel",)),
        name="regroup",
    )(w)


def _inproj_kernel(x_ref, g_ref, sh_ref, sc_ref, w_ref, *refs, bounds):
    out_refs, h_scr = refs[:-1], refs[-1]
    j = pl.program_id(1)

    @pl.when(j == 0)
    def _():
        for r in range(0, h_scr.shape[0], NORM_ROWS):
            rs = slice(r, r + NORM_ROWS)
            h_scr[rs, :] = _norm_mod(x_ref[rs, :], g_ref[...], sh_ref[...], sc_ref[...]).astype(BF16)

    for (lo, hi), o_ref in zip(bounds, out_refs):
        @pl.when((j >= lo) & (j < hi))
        def _(o_ref=o_ref):
            o_ref[...] = jnp.dot(h_scr[...], w_ref[...], preferred_element_type=F32).astype(BF16)


def _inproj(x2d, gain, shift, scale, w, widths, rows_per_mod, tm=2048, tn=512):
    m, k = x2d.shape
    tm = min(tm, rows_per_mod)
    assert rows_per_mod % tm == 0 and m % tm == 0 and all(wd % tn == 0 for wd in widths)
    bpm = rows_per_mod // tm
    mod_map = lambda i, j: (i // bpm, 0, 0)
    edges = np.cumsum([0] + [wd // tn for wd in widths])
    bounds = [(int(lo), int(hi)) for lo, hi in zip(edges[:-1], edges[1:])]
    out_spec = lambda lo, hi: pl.BlockSpec((tm, tn), lambda i, j: (i, jnp.clip(j - lo, 0, hi - lo - 1)))
    return pl.pallas_call(
        functools.partial(_inproj_kernel, bounds=bounds),
        out_shape=[jax.ShapeDtypeStruct((m, wd), BF16) for wd in widths],
        grid=(m // tm, int(edges[-1])),
        in_specs=[pl.BlockSpec((tm, k), lambda i, j: (i, 0), pipeline_mode=pl.Buffered(1)),
                  pl.BlockSpec((1, k), lambda i, j: (0, 0)),
                  pl.BlockSpec((None, 1, k), mod_map),
                  pl.BlockSpec((None, 1, k), mod_map),
                  pl.BlockSpec((k, tn), lambda i, j: (0, j))],
        out_specs=[out_spec(lo, hi) for lo, hi in bounds],
        scratch_shapes=[pltpu.VMEM((tm, k), BF16)],
        compiler_params=_cp(("parallel", "arbitrary")),
        name="inproj",
    )(x2d, gain, shift, scale, w)


def _tri_masks(reverse):
    row = lax.broadcasted_iota(jnp.int32, (CHUNK, CHUNK), 0)
    col = lax.broadcasted_iota(jnp.int32, (CHUNK, CHUNK), 1)
    if reverse:
        strict, incl = col > row, col >= row
    else:
        strict, incl = col < row, col <= row
    same = (row // SUB) == (col // SUB)
    return strict, incl, same, row == col


def _rw_heads(ar_t, bk_t, a_0, r_0, bk_end, v_b, decay_tot, reverse, out_refs):
    strict, incl, same, eye = _tri_masks(reverse)
    row2 = lax.broadcasted_iota(jnp.int32, (CHUNK, 2 * CHUNK), 0)
    col2 = lax.broadcasted_iota(jnp.int32, (CHUNK, 2 * CHUNK), 1) % CHUNK
    incl2 = (col2 >= row2) if reverse else (col2 <= row2)
    ident = jnp.where(eye, 1.0, 0.0)
    heads = range(RW_HEADS)
    hs = lambda h: slice(h * RW_HEAD, (h + 1) * RW_HEAD)
    bf = lambda xs: [x.astype(BF16) for x in xs]
    mm = lambda xs, ys: [jnp.dot(x, y, preferred_element_type=F32) for x, y in zip(xs, ys)]
    nt_dims = (((1,), (1,)), ((), ()))
    tn_dims = (((0,), (0,)), ((), ()))

    s4 = [lax.dot_general(ar_t[:, hs(h)], bk_t[:, hs(h)], nt_dims, preferred_element_type=F32) for h in heads]
    m_ab = [jnp.where(strict, s[:CHUNK, :CHUNK], 0.0) for s in s4]
    m_ak = bf([jnp.where(strict, s[:CHUNK, CHUNK:], 0.0) for s in s4])
    m_r = bf([jnp.where(incl2, s[CHUNK:], 0.0) for s in s4])

    m_bd = [jnp.where(same, m, 0.0) for m in m_ab]
    m_off = bf([jnp.where(same, 0.0, m) for m in m_ab])
    m1 = bf(m_bd)
    m2 = bf(mm(m1, m1))
    m4 = bf(mm(m2, m2))
    m8 = bf(mm(m4, m4))
    p = [ident + m for m in m_bd]
    for mk in (m2, m4, m8):
        p = [x + y for x, y in zip(p, mm(bf(p), mk))]
    p_b = bf(p)
    n1 = mm(p_b, m_off)
    n1_b = bf(n1)
    n2 = mm(n1_b, n1_b)
    n3 = mm(n1_b, bf(n2))
    t_inv = bf(mm(bf([ident + a + b + c for a, b, c in zip(n1, n2, n3)]), p_b))

    x = bf(mm(m_ak, [v_b[:, hs(h)] for h in heads]))
    wu = bf(mm(t_inv, [jnp.concatenate([a_0[:, hs(h)], xh], axis=1) for h, xh in zip(heads, x)]))
    zero = jnp.zeros((CHUNK, RW_HEAD), BF16)
    rhs = [jnp.concatenate([w, jnp.concatenate([zero, v_b[:, hs(h)]], axis=1)], axis=0) for h, w in zip(heads, wu)]
    qo = mm(m_r, rhs)
    gd = [lax.dot_general(bk_end[:, hs(h)], r, tn_dims, preferred_element_type=F32) for h, r in zip(heads, rhs)]
    q_ref, o_ref, g_ref, d_ref = out_refs
    for h in heads:
        q_ref[:, hs(h)] = r_0[:, hs(h)] + qo[h][:, :RW_HEAD]
        o_ref[:, hs(h)] = qo[h][:, RW_HEAD:]
        g_ref[:, hs(h)] = gd[h][:, :RW_HEAD] + jnp.where(eye, decay_tot[:, hs(h)], 0.0)
        d_ref[:, hs(h)] = gd[h][:, RW_HEAD:]


def _rwkv_pre_kernel(zp_ref, zc_ref, zn_ref, conv_ref, w0_ref, wl_ref, a0_ref, al_ref, gl_ref, kk_ref, ka_ref,
                     rk_ref, qf_ref, of_ref, gf_ref, df_ref, qb_ref, ob_ref, gb_ref, db_ref, g_ref, bonus_ref,
                     conv_s, *, is_ctx, nchunks):
    c = pl.program_id(1)
    row = lax.broadcasted_iota(jnp.int32, (CHUNK, 1), 0)
    has_p = (c > 0).astype(F32)
    has_n = (c < nchunks - 1).astype(F32)

    def left(z, fill):
        return jnp.where(row == 0, fill, pltpu.roll(z, 1, 0))

    def right(z, fill):
        return jnp.where(row == CHUNK - 1, fill, pltpu.roll(z, CHUNK - 1, 0))

    for cb in range(RW_PAD // CONV_COLS):
        cs = slice(cb * CONV_COLS, (cb + 1) * CONV_COLS)
        w = conv_ref[:, cs]
        zc = zc_ref[:, cs].astype(F32)
        if is_ctx:
            prev_row = zp_ref[CHUNK - BF16_ROWS:CHUNK, cs].astype(F32)[BF16_ROWS - 1:BF16_ROWS] * has_p
            next_row = zn_ref[0:BF16_ROWS, cs].astype(F32)[0:1] * has_n
            conv = left(zc, prev_row) * w[3:4] + zc * w[4:5] + right(zc, next_row) * w[5:6]
        else:
            zp = zp_ref[:, cs].astype(F32) * has_p
            zn = zn_ref[:, cs].astype(F32) * has_n
            col = lambda c: zp * w[c:c + 1] + zc * w[3 + c:4 + c] + zn * w[6 + c:7 + c]
            conv = left(col(0), 0.0) + col(1) + right(col(2), 0.0)
        conv_s[:, cs] = conv

    r = conv_s[:, 0:RW_WIDTH]
    k = conv_s[:, RW_WIDTH:2 * RW_WIDTH]
    v = conv_s[:, 2 * RW_WIDTH:3 * RW_WIDTH]
    lo = conv_s[:, 3 * RW_WIDTH:RW_PAD]
    g_ref[...] = _mm(_sigmoid(lo[:, 4 * RW_LORA:]), gl_ref[...])

    first = lax.broadcasted_iota(jnp.int32, (1, 2 * RW_HEAD), 1) < RW_HEAD

    def head_sum(t):
        tiles = []
        for i in range(RW_WIDTH // (2 * RW_HEAD)):
            ti = t[:, i * 2 * RW_HEAD:(i + 1) * 2 * RW_HEAD]
            s0 = jnp.sum(jnp.where(first, ti, 0.0), axis=-1, keepdims=True)
            s1 = jnp.sum(jnp.where(first, 0.0, ti), axis=-1, keepdims=True)
            tiles.append(jnp.where(first, s0, s1))
        return jnp.concatenate(tiles, axis=1)

    kkr = k * kk_ref[...]
    kk = kkr / jnp.maximum(jnp.sqrt(head_sum(kkr * kkr)), 1e-12)
    v_b = v.astype(BF16)
    outs = ((qf_ref, of_ref, gf_ref, df_ref), (qb_ref, ob_ref, gb_ref, db_ref))
    trow = lax.broadcasted_iota(jnp.int32, (CHUNK, CHUNK), 0)
    tcol = lax.broadcasted_iota(jnp.int32, (CHUNK, CHUNK), 1)
    bonus = None
    for d in range(2):
        wa = lo[:, d * RW_LORA:(d + 1) * RW_LORA]
        aa = lo[:, (2 + d) * RW_LORA:(3 + d) * RW_LORA]
        w_pre = w0_ref[d:d + 1, :] + _mm(jnp.tanh(wa), wl_ref[d])
        logw = -EXP_M05 * _sigmoid(w_pre)
        a = _sigmoid(a0_ref[d:d + 1, :] + _mm(aa, al_ref[d]))
        tri = jnp.where((tcol >= trow) if d == 1 else (tcol <= trow), 1.0, 0.0)
        cum = _mm_mask(tri, logw)
        prev = cum - logw
        mid = CHUNK // 2 - 1 if d == 1 else CHUNK // 2
        end = 0 if d == 1 else CHUNK - 1
        rho = cum[mid:mid + 1]
        tot = cum[end:end + 1]
        kd = k * (1.0 + (a - 1.0) * ka_ref[...])
        kka = kk * a
        hb = head_sum(r * kd * rk_ref[...])
        bonus = hb if bonus is None else bonus + hb
        e_out = jnp.exp(rho - cum)
        e_end = jnp.exp(tot - cum)
        ar_t = jnp.concatenate([-kk * jnp.exp(prev - rho), r * jnp.exp(cum - rho)], axis=0).astype(BF16)
        bk_t = jnp.concatenate([kka * e_out, kd * e_out], axis=0).astype(BF16)
        bk_end = jnp.concatenate([kka * e_end, kd * e_end], axis=0).astype(BF16)
        a_0 = (-kk * jnp.exp(prev)).astype(BF16)
        r_0 = r * jnp.exp(cum)
        _rw_heads(ar_t, bk_t, a_0, r_0, bk_end, v_b, jnp.exp(tot), d == 1, outs[d])
    bonus_ref[...] = bonus * v


def _rwkv_pre(z_rw, conv_w, w0, wl, a0, al, gl, k_k, k_a, r_k, batch, nchunks, is_ctx):
    rows = z_rw.shape[0]
    last = nchunks - 1
    zspec = lambda f: pl.BlockSpec((CHUNK, RW_PAD), lambda b, c: (b * nchunks + f(c), 0))
    full = lambda shape: pl.BlockSpec(shape, lambda b, c: (0,) * len(shape))
    ospec = pl.BlockSpec((CHUNK, RW_WIDTH), lambda b, c: (b * nchunks + c, 0))
    out = jax.ShapeDtypeStruct((rows, RW_WIDTH), F32)
    return pl.pallas_call(
        functools.partial(_rwkv_pre_kernel, is_ctx=is_ctx, nchunks=nchunks),
        out_shape=[out] * 10,
        grid=(batch, nchunks),
        in_specs=[zspec(lambda c: jnp.maximum(c - 1, 0)), zspec(lambda c: c), zspec(lambda c: jnp.minimum(c + 1, last)),
                  full((9, RW_PAD)), full((2, RW_WIDTH)), full((2, RW_LORA, RW_WIDTH)), full((2, RW_WIDTH)),
                  full((2, RW_LORA, RW_WIDTH)), full((RW_PAD - 3 * RW_WIDTH - 4 * RW_LORA, RW_WIDTH)), full((1, RW_WIDTH)),
                  full((1, RW_WIDTH)), full((1, RW_WIDTH))],
        out_specs=[ospec] * 10,
        scratch_shapes=[pltpu.VMEM((CHUNK, RW_PAD), F32)],
        compiler_params=_cp(("parallel", "parallel")),
        name="rwkv_pre",
    )(z_rw, z_rw, z_rw, conv_w, w0, wl, a0, al, gl, k_k, k_a, r_k)


def _rwkv_seq_kernel(x0f_ref, x0b_ref, qf_ref, of_ref, gf_ref, df_ref, qb_ref, ob_ref, gb_ref, db_ref,
                     outf_ref, outb_ref, xff_ref, xfb_ref, x_s, *, nsteps):
    c = pl.program_id(1)

    @pl.when(c == 0)
    def _():
        x_s[0] = x0f_ref[...]
        x_s[1] = x0b_ref[...]

    dirs = ((qf_ref, of_ref, gf_ref, df_ref, outf_ref), (qb_ref, ob_ref, gb_ref, db_ref, outb_ref))
    hs = lambda h: slice(h * RW_HEAD, (h + 1) * RW_HEAD)
    for j in range(SEQ_CHUNKS):
        for d, (q_ref, o_ref, g_ref, d_ref, out_ref) in enumerate(dirs):
            jj = SEQ_CHUNKS - 1 - j if d == 1 else j
            rows = slice(jj * CHUNK, (jj + 1) * CHUNK)
            res = [_mm(jnp.concatenate([q_ref[rows, hs(h)], g_ref[rows, hs(h)]], axis=0), x_s[d, h], exact=True)
                   for h in range(RW_HEADS)]
            for h in range(RW_HEADS):
                out_ref[rows, hs(h)] = res[h][:CHUNK] + o_ref[rows, hs(h)]
                x_s[d, h] = res[h][CHUNK:] + d_ref[rows, hs(h)]

    @pl.when(c == nsteps - 1)
    def _():
        xff_ref[...] = x_s[0]
        xfb_ref[...] = x_s[1]


def _rwkv_seq(x0f, x0b, pre, batch, nchunks):
    nsteps = nchunks // SEQ_CHUNKS
    blk = (SEQ_CHUNKS * CHUNK, RW_WIDTH)
    fspec = pl.BlockSpec(blk, lambda b, c: (b * nsteps + c, 0))
    bspec = pl.BlockSpec(blk, lambda b, c: (b * nsteps + nsteps - 1 - c, 0))
    sspec = pl.BlockSpec((None, RW_HEADS, RW_HEAD, RW_HEAD), lambda b, c: (b, 0, 0, 0))
    seq = jax.ShapeDtypeStruct(pre[0].shape, F32)
    state = jax.ShapeDtypeStruct(x0f.shape, F32)
    return pl.pallas_call(
        functools.partial(_rwkv_seq_kernel, nsteps=nsteps),
        out_shape=[seq, seq, state, state],
        grid=(batch, nsteps),
        in_specs=[sspec, sspec] + [fspec] * 4 + [bspec] * 4,
        out_specs=[fspec, bspec, sspec, sspec],
        scratch_shapes=[pltpu.VMEM((2, RW_HEADS, RW_HEAD, RW_HEAD), F32)],
        compiler_params=_cp(("parallel", "arbitrary")),
        name="rwkv_seq",
    )(x0f, x0b, *pre)


def _hgrn_prep(d, q_ref, f_ref, i_ref, lb, rows):
    reverse = d == 1
    row = lax.broadcasted_iota(jnp.int32, (CHUNK, CHUNK), 0)
    col = lax.broadcasted_iota(jnp.int32, (CHUNK, CHUNK), 1)
    incl = (col >= row) if reverse else (col <= row)
    mid = CHUNK // 2 - 1 if reverse else CHUNK // 2
    end = 0 if reverse else CHUNK - 1

    forget = lb + (1.0 - lb) * _sigmoid(f_ref[rows, :].astype(F32))
    kf = 1.0 - forget
    qraw = q_ref[rows, :].astype(F32)
    q = qraw * _sigmoid(qraw)
    v = i_ref[rows, :].astype(F32)
    cum = _mm_mask(jnp.where(incl, 1.0, 0.0), jnp.log(forget))
    ref = cum[mid:mid + 1]
    tot = cum[end:end + 1]
    bf = lambda t: t.astype(BF16)
    return dict(incl=incl, qs=bf(q * jnp.exp(cum - ref)), ks=bf(kf * jnp.exp(ref - cum)), qe=bf(q * jnp.exp(cum)),
                ke=bf(kf * jnp.exp(tot - cum)), v=bf(v), dec=jnp.exp(tot))


def _hgrn_heads(st_s, preps, out_refs, rows):
    for h in range(HG_HEADS):
        sl = slice(h * HG_HEAD, (h + 1) * HG_HEAD)
        for d, (p, out_ref, rw) in enumerate(zip(preps, out_refs, rows)):
            st = st_s[d, h]
            scores = jnp.where(p["incl"], _mm_nt(p["qs"][:, sl], p["ks"][:, sl]), 0.0)
            out_ref[rw, sl] = _mm(scores, p["v"][:, sl]) + _mm_nt(p["qe"][:, sl], st)
            st_s[d, h] = st * p["dec"][:, sl] + _mm_tn(p["v"][:, sl], p["ke"][:, sl])


def _hgrn_kernel(s0f_ref, s0b_ref, qf_ref, ff_ref, if_ref, qb_ref, fb_ref, ib_ref, lb_ref, outf_ref, outb_ref,
                 sff_ref, sfb_ref, st_s, *, nsteps):
    c = pl.program_id(1)

    @pl.when(c == 0)
    def _():
        st_s[0] = s0f_ref[...]
        st_s[1] = s0b_ref[...]

    for j in range(SEQ_CHUNKS):
        jj = SEQ_CHUNKS - 1 - j
        rows = (slice(j * CHUNK, (j + 1) * CHUNK), slice(jj * CHUNK, (jj + 1) * CHUNK))
        preps = (_hgrn_prep(0, qf_ref, ff_ref, if_ref, lb_ref[0:1, :], rows[0]),
                 _hgrn_prep(1, qb_ref, fb_ref, ib_ref, lb_ref[1:2, :], rows[1]))
        _hgrn_heads(st_s, preps, (outf_ref, outb_ref), rows)

    @pl.when(c == nsteps - 1)
    def _():
        sff_ref[...] = st_s[0]
        sfb_ref[...] = st_s[1]


def _hgrn(s0f, s0b, z_hg, lb, batch, nchunks):
    nsteps = nchunks // SEQ_CHUNKS
    blk = (SEQ_CHUNKS * CHUNK, HG_WIDTH)
    fspec = lambda col: pl.BlockSpec(blk, lambda b, c: (b * nsteps + c, col))
    bspec = lambda col: pl.BlockSpec(blk, lambda b, c: (b * nsteps + nsteps - 1 - c, col))
    sspec = pl.BlockSpec((None, HG_HEADS, HG_HEAD, HG_HEAD), lambda b, c: (b, 0, 0, 0))
    seq = jax.ShapeDtypeStruct((z_hg.shape[0], HG_WIDTH), F32)
    state = jax.ShapeDtypeStruct(s0f.shape, F32)
    return pl.pallas_call(
        functools.partial(_hgrn_kernel, nsteps=nsteps),
        out_shape=[seq, seq, state, state],
        grid=(batch, nsteps),
        in_specs=[sspec, sspec, fspec(0), fspec(1), fspec(3), bspec(0), bspec(2), bspec(3),
                  pl.BlockSpec((2, HG_WIDTH), lambda b, c: (0, 0))],
        out_specs=[fspec(0), bspec(0), sspec, sspec],
        scratch_shapes=[pltpu.VMEM((2, HG_HEADS, HG_HEAD, HG_HEAD), F32)],
        compiler_params=_cp(("parallel", "arbitrary")),
        name="hgrn",
    )(s0f, s0b, z_hg, z_hg, z_hg, z_hg, z_hg, z_hg, lb)


def _readout_kernel(x_ref, of_ref, ob_ref, bonus_ref, g_ref, hf_ref, hb_ref, og_ref, grw_ref, ghg_ref, gmx_ref,
                    lnw_ref, lnb_ref, hgn_ref, wrw_ref, whg_ref, wout_ref, o_ref, y_s):
    first = lax.broadcasted_iota(jnp.int32, (1, 2 * RW_HEAD), 1) < RW_HEAD
    tiles = [slice(i * 2 * RW_HEAD, (i + 1) * 2 * RW_HEAD) for i in range(RW_WIDTH // (2 * RW_HEAD))]

    def head_mean(ts):
        s0 = [jnp.sum(jnp.where(first, t, 0.0), axis=-1, keepdims=True) for t in ts]
        s1 = [jnp.sum(jnp.where(first, 0.0, t), axis=-1, keepdims=True) for t in ts]
        return [jnp.where(first, a, b) * (1.0 / RW_HEAD) for a, b in zip(s0, s1)]

    o_rw = [of_ref[:, sl] + ob_ref[:, sl] for sl in tiles]
    dv = [o - mu for o, mu in zip(o_rw, head_mean(o_rw))]
    var = head_mean([d * d for d in dv])
    for sl, d, vr in zip(tiles, dv, var):
        y_s[:, sl] = d * lax.rsqrt(vr + RW_GN_EPS)
    y_rw = (y_s[...] * lnw_ref[...] + lnb_ref[...] + bonus_ref[...]) * g_ref[...]
    branch_rw = _mm(y_rw, wrw_ref[...])

    hg_tiles = [slice(h * HG_HEAD, (h + 1) * HG_HEAD) for h in range(HG_HEADS)]
    o_hg = [hf_ref[:, sl] + hb_ref[:, sl] for sl in hg_tiles]
    ms = [jnp.mean(o * o, axis=-1, keepdims=True) for o in o_hg]
    for sl, o, m2 in zip(hg_tiles, o_hg, ms):
        y_s[:, sl] = o * lax.rsqrt(m2 + NORM_EPS)
    og = og_ref[...].astype(F32)
    y_hg = y_s[...] * hgn_ref[...] * (og * _sigmoid(og))
    branch_hg = _mm(y_hg, whg_ref[...])

    merged = _sigmoid(grw_ref[...].astype(F32)) * branch_rw + _sigmoid(ghg_ref[...].astype(F32)) * branch_hg
    o_ref[...] = x_ref[...] + gmx_ref[...] * _mm(merged, wout_ref[...])


def _readout(x2d, o_f, o_b, bonus, g, h_f, h_b, z_hg, z_g, g_mx, ln_w, ln_b, hg_norm, w_rw, w_hg, w_out,
             rows_per_mod, tm=256):
    m, dm = x2d.shape
    bpm = rows_per_mod // tm
    tok = lambda width, blk=0: pl.BlockSpec((tm, width), lambda i: (i, blk))
    const = lambda shape: pl.BlockSpec(shape, lambda i: (0,) * len(shape), pipeline_mode=pl.Buffered(1))
    return pl.pallas_call(
        _readout_kernel,
        out_shape=jax.ShapeDtypeStruct((m, dm), F32),
        grid=(m // tm,),
        in_specs=[tok(dm), tok(RW_WIDTH), tok(RW_WIDTH), tok(RW_WIDTH), tok(RW_WIDTH), tok(HG_WIDTH), tok(HG_WIDTH),
                  tok(HG_WIDTH, 4), tok(dm, 0), tok(dm, 1),
                  pl.BlockSpec((None, 1, dm), lambda i: (i // bpm, 0, 0)),
                  const((1, RW_WIDTH)), const((1, RW_WIDTH)), const((1, HG_WIDTH)),
                  const((RW_WIDTH, dm)), const((HG_WIDTH, dm)), const((dm, dm))],
        out_specs=tok(dm),
        scratch_shapes=[pltpu.VMEM((tm, RW_WIDTH), F32)],
        compiler_params=_cp(("parallel",)),
        name="readout",
    )(x2d, o_f, o_b, bonus, g, h_f, h_b, z_hg, z_g, z_g, g_mx, ln_w, ln_b, hg_norm, w_rw, w_hg, w_out)


def _cand_groups():
    return [(i, PEER_TOPK // (i + 1)) for i in range(PEER_TOPK)]


def _pick_max(work, io, first_only):
    m = jnp.max(work, axis=0, keepdims=True)
    sel = work == m
    if first_only:
        sel = io == jnp.min(jnp.where(sel, io, work.shape[0]), axis=0, keepdims=True)
    return m, sel


def _topk_ranks(s, vals_ref, first_only, need_rank=True):
    io = lax.broadcasted_iota(jnp.int32, s.shape, 0)

    if first_only:
        def body(r, carry):
            work, rank = carry
            m, sel = _pick_max(work, io, True)
            vals_ref[pl.ds(r, 1), :] = m
            return jnp.where(sel, -jnp.inf, work), jnp.where(sel, lax.convert_element_type(r, F32), rank)

        _, rank = lax.fori_loop(0, PEER_TOPK, body, (s, jnp.full(s.shape, float(PEER_TOPK), F32)))
    else:
        def body(r, work):
            m, sel = _pick_max(work, io, False)
            vals_ref[pl.ds(r, 1), :] = m
            return jnp.where(sel, -jnp.inf, work)

        lax.fori_loop(0, PEER_TOPK, body, s)
        if not need_rank:
            taken = jnp.sum(jnp.where(s >= vals_ref[PEER_TOPK - 1:PEER_TOPK, :], 1.0, 0.0), axis=0, keepdims=True)
            return None, jnp.where(taken == float(PEER_TOPK), 0.0, 1.0)
        rank = jnp.zeros(s.shape, F32)
        for r in range(PEER_TOPK):
            rank = rank + jnp.where(vals_ref[r:r + 1, :] > s, 1.0, 0.0)
    taken = jnp.sum(jnp.where(rank < float(PEER_TOPK), 1.0, 0.0), axis=0, keepdims=True)
    return rank, jnp.where(taken == float(PEER_TOPK), 0.0, 1.0)


def _peer_sel_kernel(x_ref, gain_ref, sh_ref, sc_ref, wq_ref, k1_ref, k2_ref, hx_ref, cnt_ref, rk_ref, e1_ref,
                     e2_ref, q_s, v1_s, v2_s):
    hb = _norm_mod(x_ref[...], gain_ref[...], sh_ref[...], sc_ref[...]).astype(BF16)
    hx_ref[...] = hb
    q_s[...] = jnp.dot(hb, wq_ref[...], preferred_element_type=F32)
    tm = q_s.shape[0]
    groups = _cand_groups()
    ncand = sum(n for _, n in groups)
    npad = -ncand % 8

    def select(hd, first_only):
        q1 = q_s[:, (2 * hd) * PEER_NKEYS:(2 * hd + 1) * PEER_NKEYS]
        q2 = q_s[:, (2 * hd + 1) * PEER_NKEYS:(2 * hd + 2) * PEER_NKEYS]
        s1 = _mm_nt(k1_ref[...], q1, exact=True)
        s2 = _mm_nt(k2_ref[...], q2, exact=True)
        rank1, bad1 = _topk_ranks(s1, v1_s, first_only, need_rank=first_only)
        rank2, bad2 = _topk_ranks(s2, v2_s, first_only)
        v1 = v1_s[...]
        v2 = v2_s[...]
        rows = [v1[i:i + 1] + v2[0:n] for i, n in groups]
        if npad:
            rows.append(jnp.full((npad, tm), -jnp.inf, F32))
        cand = jnp.concatenate(rows, axis=0)
        io = lax.broadcasted_iota(jnp.int32, cand.shape, 0)
        best0 = v1[0:1] + v2[0:1]

        def body(r, carry):
            work, z = carry
            m, sel = _pick_max(work, io, first_only)
            return jnp.where(sel, -jnp.inf, work), z + jnp.exp(m - best0)

        work, z = lax.fori_loop(0, PEER_TOPK, body, (cand, jnp.zeros((1, tm), F32)))
        taken = jnp.where((work == -jnp.inf) & (io < ncand), 1.0, 0.0)
        bad3 = jnp.where(jnp.sum(taken, axis=0, keepdims=True) == float(PEER_TOPK), 0.0, 1.0)
        counts = []
        start = 0
        for i, n in groups:
            counts.append(jnp.sum(taken[start:start + n], axis=0, keepdims=True))
            start += n
        cnt1 = jnp.zeros_like(s1)
        for i, c_i in enumerate(counts):
            if first_only:
                cnt1 = cnt1 + jnp.where(rank1 == float(i), c_i, 0.0)
            else:
                step = c_i - counts[i + 1] if i + 1 < len(counts) else c_i
                cnt1 = cnt1 + jnp.where(s1 >= v1[i:i + 1], step, 0.0)
        cnt_ref[hd] = cnt1
        rk_ref[hd] = rank2.astype(BF16)
        e1_ref[hd] = jnp.exp(s1 - v1[0:1])
        e2_ref[hd] = (jnp.exp(s2 - v2[0:1]) / z).astype(BF16)
        return jnp.maximum(bad1, jnp.maximum(bad2, bad3))

    for hd in range(PEER_HEADS):
        bad = select(hd, False)

        @pl.when(jnp.max(bad) > 0.5)
        def _(hd=hd):
            select(hd, True)


def _peer_sel(x2d, gain, shift, scale, wq, k1, k2, rows_per_mod, tm=256):
    m, dm = x2d.shape
    bpm = rows_per_mod // tm
    mod_map = lambda i: (i // bpm, 0, 0)
    const = lambda shape: pl.BlockSpec(shape, lambda i: (0,) * len(shape))
    sel = jax.ShapeDtypeStruct((PEER_HEADS, PEER_NKEYS, m), F32)
    sel_bf = jax.ShapeDtypeStruct((PEER_HEADS, PEER_NKEYS, m), BF16)
    sel_spec = pl.BlockSpec((PEER_HEADS, PEER_NKEYS, tm), lambda i: (0, 0, i))
    return pl.pallas_call(
        _peer_sel_kernel,
        out_shape=[jax.ShapeDtypeStruct((m, dm), BF16), sel, sel_bf, sel, sel_bf],
        grid=(m // tm,),
        in_specs=[pl.BlockSpec((tm, dm), lambda i: (i, 0)), const((1, dm)),
                  pl.BlockSpec((None, 1, dm), mod_map), pl.BlockSpec((None, 1, dm), mod_map),
                  const(wq.shape), const(k1.shape), const(k2.shape)],
        out_specs=[pl.BlockSpec((tm, dm), lambda i: (i, 0)), sel_spec, sel_spec, sel_spec, sel_spec],
        scratch_shapes=[pltpu.VMEM((tm, dm), F32), pltpu.VMEM((PEER_TOPK, tm), F32),
                        pltpu.VMEM((PEER_TOPK, tm), F32)],
        compiler_params=_cp(("parallel",)),
        name="peer_sel",
    )(x2d, gain, shift, scale, wq, k1, k2)


def _gelu(x):
    return 0.5 * x * (1.0 + lax.erf(x * float(np.sqrt(0.5))))


def _peer_dense_kernel(hx_ref, u_ref, v_ref, cnt_ref, e1_ref, cnt_nxt_ref, e1_nxt_ref, rk_ref, e2_ref, x_ref, gfx_ref,
                       gain_ref, o_ref, w_s, ga_s, gb_s, *, te, nblk):
    e = pl.program_id(1)
    half = te // 2
    nsub = half // PEER_NKEYS
    tm = hx_ref.shape[0]

    def gates(cnt_blk, e1_blk, first, dst):
        packed = (PEER_NKEYS // BF16_ROWS, BF16_ROWS, tm)
        for ii in range(nsub):
            gate = None
            for hd in range(PEER_HEADS):
                cnt = jnp.broadcast_to(cnt_blk[hd, first + ii:first + ii + 1, :], packed[1:]).astype(BF16)
                e1 = jnp.broadcast_to(e1_blk[hd, first + ii:first + ii + 1, :], packed[1:]).astype(BF16)
                t = jnp.where(rk_ref[hd].reshape(packed) < cnt[None], e2_ref[hd].reshape(packed) * e1[None],
                              jnp.zeros((), BF16))
                gate = t if gate is None else gate + t
            dst[ii * PEER_NKEYS:(ii + 1) * PEER_NKEYS, :] = gate.reshape(PEER_NKEYS, tm)

    @pl.when(e == 0)
    def _():
        o_ref[...] = jnp.zeros_like(o_ref)
        gates(cnt_ref, e1_ref, 0, ga_s)

    nt_dims = (((1,), (1,)), ((), ()))

    def weighted(u_rows, g_s):
        act = _gelu(lax.dot_general(u_ref[u_rows, :], hx_ref[...], nt_dims, preferred_element_type=F32))
        w_s[:, u_rows] = (g_s[...] * act.astype(BF16)).T

    gates(cnt_ref, e1_ref, nsub, gb_s)
    weighted(slice(0, half), ga_s)
    weighted(slice(half, te), gb_s)
    gates(cnt_nxt_ref, e1_nxt_ref, 0, ga_s)
    o_ref[...] += jnp.dot(w_s[...], v_ref[...], preferred_element_type=F32)

    @pl.when(e == nblk - 1)
    def _():
        x = x_ref[...] + gfx_ref[...] * o_ref[...]
        o_ref[...] = x * lax.rsqrt(jnp.mean(x * x, axis=-1, keepdims=True) + NORM_EPS) * gain_ref[...]


def _peer_dense(hx, u, v, cnt, rk, e1, e2, x2d, g_fx, gain, rows_per_mod, tm=512, te=1024):
    m, dm = hx.shape
    ne = u.shape[0]
    nblk = ne // te
    bpm = rows_per_mod // tm
    sel_spec = pl.BlockSpec((PEER_HEADS, PEER_NKEYS, tm), lambda i, e: (0, 0, i))
    tok = pl.BlockSpec((tm, dm), lambda i, e: (i, 0))
    tab = pl.BlockSpec((te, dm), lambda i, e: (e, 0))
    key1_spec = lambda blk: pl.BlockSpec((PEER_HEADS, te // PEER_NKEYS, tm), lambda i, e: (0, blk(e), i))
    return pl.pallas_call(
        functools.partial(_peer_dense_kernel, te=te, nblk=nblk),
        out_shape=jax.ShapeDtypeStruct((m, dm), F32),
        grid=(m // tm, nblk),
        in_specs=[tok, tab, tab, key1_spec(lambda e: e), key1_spec(lambda e: e),
                  key1_spec(lambda e: (e + 1) % nblk), key1_spec(lambda e: (e + 1) % nblk), sel_spec, sel_spec, tok,
                  pl.BlockSpec((None, 1, dm), lambda i, e: (i // bpm, 0, 0)),
                  pl.BlockSpec((1, dm), lambda i, e: (0, 0))],
        out_specs=tok,
        scratch_shapes=[pltpu.VMEM((tm, te), BF16)] + [pltpu.VMEM((te // 2, tm), BF16)] * 2,
        compiler_params=_cp(("parallel", "arbitrary")),
        name="peer_dense",
    )(hx, u, v, cnt, e1, cnt, e1, rk, e2, x2d, g_fx, gain)


def kernel(x, c, ctx, c_ctx, w_ada, b_ada, norm_mix, norm_ffn, norm_final, w_in, rw_conv, rw_w0, rw_w_lora_b, rw_a0, rw_a_lora_b, rw_g_lora_b, rw_k_k, rw_k_a, rw_r_k, rw_ln_w, rw_ln_b, hg_lb, hg_norm, w_up_rw, w_up_hg, w_out, peer_wq, peer_k1, peer_k2, peer_u, peer_v):
    batch, seq, dm = x.shape
    ctx_len = ctx.shape[1]
    depth = w_in.shape[0]
    assert depth == 1 and dm == D_MODEL
    assert seq % (CHUNK * SEQ_CHUNKS) == 0 and ctx_len % (CHUNK * SEQ_CHUNKS) == 0
    n_x = seq // CHUNK
    n_c = ctx_len // CHUNK
    l = 0

    w_in_p = _regroup(w_in, l)
    conv_w = jnp.pad(rw_conv[l].reshape(9, RW_COLS), ((0, 0), (0, RW_PAD - RW_COLS)))
    lower_bounds = jnp.cumsum(jax.nn.softmax(hg_lb.astype(F32), axis=0), axis=0)[l]
    row = lambda t: t.reshape(1, -1)
    hg_norm_row = jnp.tile(hg_norm[l], HG_HEADS).reshape(1, HG_WIDTH)
    u_bf = peer_u[l].astype(BF16)
    v_bf = peer_v[l].astype(BF16)

    cvec = jnp.concatenate([c, c_ctx[None, :], jnp.zeros((8 - batch - 1, dm), F32)], axis=0)
    mod = _ada(cvec, w_ada[l], row(b_ada[l]))
    mods = mod.reshape(8, N_MOD, 1, dm)
    sh_mx, sc_mx, g_mx, sh_fx, sc_fx, g_fx = (mods[:batch, i] for i in range(N_MOD))
    sh_mc, sc_mc = mods[batch:batch + 1, 0], mods[batch:batch + 1, 1]

    x2d = x.reshape(batch * seq, dm)
    c2d = ctx.reshape(batch * ctx_len, dm)
    gain_mix = row(norm_mix[l])

    zc_rw, zc_hg = _inproj(c2d, gain_mix, sh_mc, sc_mc, w_in_p, (RW_PAD, 5 * HG_WIDTH), batch * ctx_len)
    g_lora = jnp.pad(rw_g_lora_b[l], ((0, RW_PAD - RW_COLS), (0, 0)))
    rw_args = (conv_w, rw_w0[l], rw_w_lora_b[l], rw_a0[l], rw_a_lora_b[l], g_lora, row(rw_k_k[l]),
               row(rw_k_a[l]), row(rw_r_k[l]))
    pre_c = _rwkv_pre(zc_rw, *rw_args, batch=batch, nchunks=n_c, is_ctx=True)
    zero_rw = jnp.zeros((batch, RW_HEADS, RW_HEAD, RW_HEAD), F32)
    zero_hg = jnp.zeros((batch, HG_HEADS, HG_HEAD, HG_HEAD), F32)
    _, _, st_rw_f, st_rw_b = _rwkv_seq(zero_rw, zero_rw, pre_c[0:8], batch, n_c)
    _, _, st_hg_f, st_hg_b = _hgrn(zero_hg, zero_hg, zc_hg, lower_bounds, batch, n_c)

    z_rw, z_hg, z_g = _inproj(x2d, gain_mix, sh_mx, sc_mx, w_in_p, (RW_PAD, 5 * HG_WIDTH, 2 * dm), seq)
    pre = _rwkv_pre(z_rw, *rw_args, batch=batch, nchunks=n_x, is_ctx=False)
    o_f, o_b, _, _ = _rwkv_seq(st_rw_f, st_rw_b, pre[0:8], batch, n_x)
    h_f, h_b, _, _ = _hgrn(st_hg_f, st_hg_b, z_hg, lower_bounds, batch, n_x)
    x1 = _readout(x2d, o_f, o_b, pre[9], pre[8], h_f, h_b, z_hg, z_g, g_mx, row(rw_ln_w[l]), row(rw_ln_b[l]),
                  hg_norm_row, w_up_rw[l].astype(BF16), w_up_hg[l].astype(BF16), w_out[l].astype(BF16), seq)

    hx, cnt, rk, e1, e2 = _peer_sel(x1, row(norm_ffn[l]), sh_fx, sc_fx, peer_wq[l].astype(BF16), peer_k1[l],
                                    peer_k2[l], seq)
    out = _peer_dense(hx, u_bf, v_bf, cnt, rk, e1, e2, x1, g_fx, row(norm_final), seq)
    return out.reshape(batch, seq, dm)
```
